```python
import math
import jax, jax.numpy as jnp
from jax import lax
import numpy as np

D_MODEL = 2048
BATCH = 8
SEQ = 4096
DEPTH = 4

N_MEM = 256
D_MIX = D_MODEL
GROUP_W = D_MIX // 4
GLA_HEADS = 4
GLA_DV = GROUP_W // GLA_HEADS
GLA_DK = GLA_DV // 2
GLA_QK = GLA_HEADS * GLA_DK
GLA_V = GLA_HEADS * GLA_DV
GLA_RANK = 16
GLA_GATE_NORM = 16.0
GLA_CHUNK = 64
FNET_GROUPS = 4
FNET_CH = GROUP_W // FNET_GROUPS
HY_W = GROUP_W
HY_ORDER = 2
HY_BANDS = 16
HY_EMB = 2 * HY_BANDS + 1
HY_FFN = 64
HY_DECAY_SLOW = -math.log(1e-2) / 1.5
HY_DECAY_FAST = -math.log(1e-2) / 0.3
SC_W = GROUP_W
SHORT_W = 3
IN_SPLITS = (GLA_QK, GLA_QK, GLA_V, GLA_V, 2 * GLA_RANK, GROUP_W, 3 * HY_W, 3 * SC_W)
D_IN = 2 * GLA_QK + 2 * GLA_V + 2 * GLA_RANK + GROUP_W + 3 * HY_W + 3 * SC_W
XA_HEADS = 4
XA_HD = D_MODEL // XA_HEADS
D_FF = ((8 * D_MODEL // 3 + 255) // 256) * 256
EPS = 1e-6

kernel_name = 'hybrid_parallel_group_encoder'


def rms_norm(x, g):
    xf = x.astype(jnp.float32)
    y = xf * lax.rsqrt(jnp.mean(xf * xf, axis=-1, keepdims=True) + EPS)
    return (y * g.astype(jnp.float32)).astype(x.dtype)


def short_conv(u, w):
    up = jnp.pad(u, ((0, 0), (1, 1), (0, 0)))
    return up[:, :-2] * w[0] + up[:, 1:-1] * w[1] + up[:, 2:] * w[2]


def gla_direction(q, k, v, gk, strict):
    bsz, seq, heads, dk = q.shape
    dv = v.shape[-1]
    n = seq // GLA_CHUNK
    rs = lambda t: t.reshape(bsz, n, GLA_CHUNK, heads, t.shape[-1])
    q, k, v, gk = rs(q), rs(k), rs(v), rs(gk)
    b = jnp.cumsum(gk, axis=2)
    b_last = b[:, :, -1]
    b_ref = b[:, :, GLA_CHUNK // 2:GLA_CHUNK // 2 + 1]
    scores = jnp.einsum('bnihk,bnjhk->bnhij', q * jnp.exp(b - b_ref), k * jnp.exp(b_ref - b))
    mask = jnp.tril(jnp.ones((GLA_CHUNK, GLA_CHUNK), dtype=bool), k=-1 if strict else 0)
    scores = jnp.where(mask, scores, 0.0)
    o_intra = jnp.einsum('bnhij,bnjhv->bnihv', scores, v)
    u = jnp.einsum('bnchk,bnchv->bnhkv', k * jnp.exp(b_last[:, :, None] - b), v)
    decay = jnp.exp(b_last)

    def step(state, inp):
        d, un = inp
        return state * d[..., None] + un, state

    init = jnp.zeros((bsz, heads, dk, dv), dtype=jnp.float32)
    _, states = lax.scan(step, init, (jnp.moveaxis(decay, 1, 0), jnp.moveaxis(u, 1, 0)))
    states = jnp.moveaxis(states, 0, 1)
    o_inter = jnp.einsum('bnchk,bnhkv->bnchv', q * jnp.exp(b), states)
    return (o_intra + o_inter).reshape(bsz, seq, heads, dv)


def gla_mixer(q, k, v, g, lr, gk_w, gk_b, norm_g):
    f32 = jnp.float32
    bsz, seq, _ = q.shape
    heads = lambda t, d: t.astype(f32).reshape(bsz, seq, GLA_HEADS, d)
    q = heads(q, GLA_DK) * GLA_DK ** -0.5
    k = heads(k, GLA_DK)
    v = heads(v, GLA_DV)
    lr = lr.astype(f32)
    gk_fwd = jax.nn.log_sigmoid(lr[..., :GLA_RANK] @ gk_w[0].astype(f32) + gk_b[0].astype(f32)) / GLA_GATE_NORM
    gk_bwd = jax.nn.log_sigmoid(lr[..., GLA_RANK:] @ gk_w[1].astype(f32) + gk_b[1].astype(f32)) / GLA_GATE_NORM
    gk_fwd = gk_fwd.reshape(bsz, seq, GLA_HEADS, GLA_DK)
    gk_bwd = gk_bwd.reshape(bsz, seq, GLA_HEADS, GLA_DK)
    flip = lambda t: jnp.flip(t, axis=1)
    o = gla_direction(q, k, v, gk_fwd, False) + flip(gla_direction(flip(q), flip(k), flip(v), flip(gk_bwd), True))
    o = rms_norm(o, norm_g) * jax.nn.silu(heads(g, GLA_DV))
    return o.reshape(bsz, seq, GLA_V)


def fnet_mixer(u):
    bsz, seq, _ = u.shape
    uf = u.astype(jnp.float32).reshape(bsz, seq, FNET_GROUPS, FNET_CH)
    return jnp.fft.fftn(uf, axes=(1, 3), norm='ortho').real.reshape(bsz, seq, GROUP_W)


def hyena_position_features(seq):
    pos = jnp.arange(seq, dtype=jnp.float32)
    t = pos / seq
    f = jnp.linspace(1e-4, HY_BANDS - 1, HY_BANDS, dtype=jnp.float32)
    ang = (2.0 * math.pi * t)[:, None] * f[None, :]
    return jnp.concatenate([t[:, None], jnp.cos(ang), -jnp.sin(ang)], axis=-1)


def hyena_filters(feats, w1, b1, w2, b2, w3, freq, decay):
    f32 = jnp.float32
    seq = feats.shape[0]
    freq = freq.astype(f32)
    h = jnp.sin(freq * (feats @ w1.astype(f32) + b1.astype(f32)))
    h = jnp.sin(freq * (h @ w2.astype(f32) + b2.astype(f32)))
    h = h @ w3.astype(f32)
    window = jnp.exp(-feats[:, :1] * jnp.abs(decay.astype(f32).reshape(-1)))
    h = (h * window).reshape(seq, HY_ORDER, 2, HY_W)
    h_fwd, h_bwd = h[:, :, 0], h[:, :, 1]
    two_sided = jnp.concatenate([h_fwd, jnp.zeros_like(h_fwd[:1]), h_bwd[:0:-1]], axis=0)
    return jnp.moveaxis(jnp.fft.rfft(two_sided, axis=0), 1, 0)


def long_conv(z, filt, skip):
    seq = z.shape[1]
    zf = jnp.fft.rfft(z, n=2 * seq, axis=1)
    y = jnp.fft.irfft(zf * filt[None], n=2 * seq, axis=1)[:, :seq]
    return y + z * skip.astype(jnp.float32)


def hyena_mixer(u, conv_w, filt, skip):
    u = short_conv(u, conv_w).astype(jnp.float32)
    v, x1, x2 = jnp.split(u, 3, axis=-1)
    z = x1 * long_conv(v, filt[0], skip[0])
    return x2 * long_conv(z, filt[1], skip[1])


def shortconv_mixer(u, conv_w):
    b, c, h = jnp.split(u, 3, axis=-1)
    return b * short_conv(c * h, conv_w)


def cross_attention(h, mem_n, w_q, w_kv, w_o):
    bsz, seq, _ = h.shape
    n_mem = mem_n.shape[1]
    q = (h @ w_q).reshape(bsz, seq, XA_HEADS, XA_HD)
    k, v = jnp.split(mem_n @ w_kv, 2, axis=-1)
    k = k.reshape(bsz, n_mem, XA_HEADS, XA_HD)
    v = v.reshape(bsz, n_mem, XA_HEADS, XA_HD)
    s = jnp.einsum('bshd,bmhd->bhsm', q, k).astype(jnp.float32) * XA_HD ** -0.5
    p = jax.nn.softmax(s, axis=-1).astype(v.dtype)
    o = jnp.einsum('bhsm,bmhd->bshd', p, v).reshape(bsz, seq, D_MODEL)
    return o @ w_o


def swiglu(h, w_gate_up, w_down):
    gate, up = jnp.split(h @ w_gate_up, 2, axis=-1)
    return (jax.nn.silu(gate) * up) @ w_down


def setup_inputs(seed: int = 0) -> dict:
    key = jax.random.key(seed)
    ks = jax.random.split(key, 32)
    nrm = lambda k, shape, scale: jax.random.normal(k, shape, dtype=jnp.float32) * scale
    gain = lambda k, shape: 1.0 + 0.02 * jax.random.normal(k, shape, dtype=jnp.float32)
    decay_base = jnp.linspace(HY_DECAY_SLOW, HY_DECAY_FAST, HY_W, dtype=jnp.float32)
    return {
        'x': nrm(ks[0], (BATCH, SEQ, D_MODEL), 1.0),
        'mem': nrm(ks[1], (BATCH, N_MEM, D_MODEL), 1.0),
        'norm_g': gain(ks[2], (DEPTH, 3, D_MODEL)),
        'w_in': nrm(ks[3], (DEPTH, D_MODEL, D_IN), D_MODEL ** -0.5),
        'gla_gk_w': nrm(ks[4], (DEPTH, 2, GLA_RANK, GLA_QK), GLA_RANK ** -0.5),
        'gla_gk_b': nrm(ks[5], (DEPTH, 2, GLA_QK), 0.1),
        'gla_norm_g': gain(ks[6], (DEPTH, GLA_DV)),
        'hy_conv_w': nrm(ks[7], (DEPTH, SHORT_W, 3 * HY_W), SHORT_W ** -0.5),
        'hy_ffn_w1': nrm(ks[8], (DEPTH, HY_EMB, HY_FFN), HY_EMB ** -0.5),
        'hy_ffn_b1': nrm(ks[9], (DEPTH, HY_FFN), 0.02),
        'hy_ffn_w2': nrm(ks[10], (DEPTH, HY_FFN, HY_FFN), HY_FFN ** -0.5),
        'hy_ffn_b2': nrm(ks[11], (DEPTH, HY_FFN), 0.02),
        'hy_ffn_w3': nrm(ks[12], (DEPTH, HY_FFN, HY_ORDER * 2 * HY_W), 0.1 * HY_FFN ** -0.5),
        'hy_sin_freq': gain(ks[13], (DEPTH, HY_FFN)),
        'hy_decay': decay_base * gain(ks[14], (DEPTH, HY_ORDER, 2, HY_W)),
        'hy_skip': nrm(ks[15], (DEPTH, HY_ORDER, HY_W), 1.0),
        'sc_conv_w': nrm(ks[16], (DEPTH, SHORT_W, SC_W), SHORT_W ** -0.5),
        'grp_norm_g': gain(ks[17], (DEPTH, 3, GROUP_W)),
        'w_out': nrm(ks[18], (DEPTH, D_MIX, D_MODEL), D_MIX ** -0.5),
        'mem_norm_g': gain(ks[19], (D_MODEL,)),
        'w_xq': nrm(ks[20], (DEPTH, D_MODEL, D_MODEL), D_MODEL ** -0.5),
        'w_xkv': nrm(ks[21], (DEPTH, D_MODEL, 2 * D_MODEL), D_MODEL ** -0.5),
        'w_xo': nrm(ks[22], (DEPTH, D_MODEL, D_MODEL), D_MODEL ** -0.5),
        'w_gate_up': nrm(ks[23], (DEPTH, D_MODEL, 2 * D_FF), D_MODEL ** -0.5),
        'w_down': nrm(ks[24], (DEPTH, D_FF, D_MODEL), D_FF ** -0.5),
        'final_norm_g': gain(ks[25], (D_MODEL,)),
    }


def reference(x, mem, norm_g, w_in, gla_gk_w, gla_gk_b, gla_norm_g, hy_conv_w, hy_ffn_w1, hy_ffn_b1,
              hy_ffn_w2, hy_ffn_b2, hy_ffn_w3, hy_sin_freq, hy_decay, hy_skip, sc_conv_w, grp_norm_g,
              w_out, mem_norm_g, w_xq, w_xkv, w_xo, w_gate_up, w_down, final_norm_g):
    feats = hyena_position_features(x.shape[1])
    mem_n = rms_norm(mem, mem_norm_g)
    split_at = np.cumsum(IN_SPLITS)[:-1].tolist()
    for l in range(DEPTH):
        h = rms_norm(x, norm_g[l, 0])
        q, k, v, g, lr, u_f, u_h, u_s = jnp.split(h @ w_in[l], split_at, axis=-1)
        y_a = gla_mixer(q, k, v, g, lr, gla_gk_w[l], gla_gk_b[l], gla_norm_g[l])
        y_b = rms_norm(fnet_mixer(u_f), grp_norm_g[l, 0])
        filt = hyena_filters(feats, hy_ffn_w1[l], hy_ffn_b1[l], hy_ffn_w2[l], hy_ffn_b2[l],
                             hy_ffn_w3[l], hy_sin_freq[l], hy_decay[l])
        y_c = rms_norm(hyena_mixer(u_h, hy_conv_w[l], filt, hy_skip[l]), grp_norm_g[l, 1])
        y_d = rms_norm(shortconv_mixer(u_s, sc_conv_w[l]), grp_norm_g[l, 2])
        mix = jnp.concatenate([y_a.astype(x.dtype), y_b.astype(x.dtype), y_c.astype(x.dtype), y_d.astype(x.dtype)], axis=-1)
        x = x + mix @ w_out[l]
        x = x + cross_attention(rms_norm(x, norm_g[l, 1]), mem_n, w_xq[l], w_xkv[l], w_xo[l])
        x = x + swiglu(rms_norm(x, norm_g[l, 2]), w_gate_up[l], w_down[l])
    return rms_norm(x, final_norm_g)
```

```python
import functools
import math

import jax
import jax.numpy as jnp
from jax import lax
from jax.experimental import pallas as pl
from jax.experimental.pallas import tpu as pltpu

F32 = jnp.float32
BF16 = jnp.bfloat16
EPS = 1e-6

D_MODEL = 2048
GROUP_W = 512
GLA_HEADS = 4
GLA_DK = 64
GLA_DV = 128
GLA_RANK = 16
GLA_GATE_NORM = 16.0
GLA_CHUNK = 64
FNET_CH = 128
HY_BANDS = 16
HY_EMB = 2 * HY_BANDS + 1
HY_FFN = 64
XA_HEADS = 4
XA_HD = D_MODEL // XA_HEADS
N_MEM = 256
D_FF = 5632
LANE = 128
BF16_ROWS = 16
VMEM_LIMIT = 56 * 1024 * 1024

P_UH, P_US, P_Q, P_K, P_V, P_G, P_UF = 0, 1536, 3072, 3584, 4096, 4608, 5120
P_W = 5632

NT = (((1,), (1,)), ((), ()))
TN = (((0,), (0,)), ((), ()))


def _cp(*sem):
    return pltpu.CompilerParams(dimension_semantics=sem, vmem_limit_bytes=VMEM_LIMIT)


def _rms(x, g):
    ms = jnp.mean(x * x, axis=-1, keepdims=True)
    return x * lax.rsqrt(ms + EPS) * g


def _norm_rows_to(x_ref, g_ref, hn_ref):
    rows = min(256, x_ref.shape[0])
    n = x_ref.shape[0] // rows

    def body(i, c):
        r = pl.ds(pl.multiple_of(i * rows, rows), rows)
        hn_ref[r, :] = _rms(x_ref[r, :], g_ref[...]).astype(hn_ref.dtype)
        return c

    lax.fori_loop(0, n, body, 0)


def _norm_matmul_kernel(x_ref, g_ref, w_ref, *rest, with_side):
    if with_side:
        w2_ref, o_ref, o2_ref, hn_ref = rest
    else:
        o_ref, hn_ref = rest

    @pl.when(pl.program_id(1) == 0)
    def _():
        _norm_rows_to(x_ref, g_ref, hn_ref)
        if with_side:
            o2_ref[...] = jnp.dot(hn_ref[...], w2_ref[...], preferred_element_type=F32)

    o_ref[...] = jnp.dot(hn_ref[...], w_ref[...], preferred_element_type=F32).astype(o_ref.dtype)


def _norm_matmul(x, g, w, tm, tn, w_side=None):
    m, k = x.shape
    n = w.shape[1]
    tm, tn = min(tm, m), min(tn, n)
    in_specs = [
        pl.BlockSpec((tm, k), lambda i, j: (i, 0)),
        pl.BlockSpec((1, k), lambda i, j: (0, 0)),
        pl.BlockSpec((k, tn), lambda i, j: (0, j)),
    ]
    out_shape = [jax.ShapeDtypeStruct((m, n), BF16)]
    out_specs = [pl.BlockSpec((tm, tn), lambda i, j: (i, j))]
    args = [x, g.reshape(1, k), w]
    if w_side is not None:
        n2 = w_side.shape[1]
        in_specs.append(pl.BlockSpec((k, n2), lambda i, j: (0, 0)))
        out_shape.append(jax.ShapeDtypeStruct((m, n2), F32))
        out_specs.append(pl.BlockSpec((tm, n2), lambda i, j: (i, 0)))
        args.append(w_side)
    res = pl.pallas_call(
        functools.partial(_norm_matmul_kernel, with_side=w_side is not None),
        grid=(m // tm, n // tn),
        in_specs=in_specs,
        out_specs=out_specs,
        out_shape=out_shape,
        scratch_shapes=[pltpu.VMEM((tm, k), BF16)],
        compiler_params=_cp("parallel", "arbitrary"),
        name="norm_matmul",
    )(*args)
    return res if w_side is not None else res[0]


def _gla_pass(fwd, blk, q_ref, k_ref, v_ref, lr_ref, g_ref, wg_ref, bg_ref, gn_ref, y_ref, st_ref, ob_ref):
    ts = q_ref.shape[0]
    c_sz = GLA_CHUNK
    nc = ts // c_sz
    x = jnp.dot(lr_ref[...].astype(BF16), wg_ref[...], preferred_element_type=F32) + bg_ref[...]
    gk = (jnp.minimum(x, 0.0) - jnp.log1p(jnp.exp(-jnp.abs(x)))) * (1.0 / GLA_GATE_NORM)
    hi = gk.astype(BF16)
    lo = (gk - hi.astype(F32)).astype(BF16)
    ii = lax.broadcasted_iota(jnp.int32, (c_sz, c_sz), 0)
    jj = lax.broadcasted_iota(jnp.int32, (c_sz, c_sz), 1)
    incl = (jj <= ii) if fwd else (jj >= ii)
    mask = incl if fwd else (jj > ii)
    tri = jnp.where(incl, 1.0, 0.0).astype(BF16)
    gn = gn_ref[...]
    for c in (range(nc) if fwd else reversed(range(nc))):
        r = slice(c * c_sz, (c + 1) * c_sz)
        b = jnp.dot(tri, hi[r], preferred_element_type=F32) + jnp.dot(tri, lo[r], preferred_element_type=F32)
        bref = b[c_sz // 2:c_sz // 2 + 1] if fwd else b[c_sz // 2 - 1:c_sz // 2]
        btot = b[c_sz - 1:c_sz] if fwd else b[0:1]
        qc = q_ref[r, :].astype(F32) * (GLA_DK ** -0.5)
        kc = k_ref[r, :].astype(F32)
        vc = v_ref[r, :]
        qe = (qc * jnp.exp(b - bref)).astype(BF16)
        ke = (kc * jnp.exp(bref - b)).astype(BF16)
        qi = (qc * jnp.exp(b)).astype(BF16)
        ku = (kc * jnp.exp(btot - b)).astype(BF16)
        dec = jnp.exp(btot)
        rows = pl.ds(pl.multiple_of(blk * ts + c * c_sz, c_sz), c_sz)
        for h in range(GLA_HEADS):
            hs = slice(h * LANE, (h + 1) * LANE)
            s = lax.dot_general(qe[:, hs], ke[:, hs], NT, preferred_element_type=F32)
            s = jnp.where(mask, s, 0.0).astype(BF16)
            st = st_ref[h]
            o = jnp.dot(s, vc[:, hs], preferred_element_type=F32)
            o = o + lax.dot_general(qi[:, hs], st.astype(BF16), NT, preferred_element_type=F32)
            st_ref[h] = st * dec[:, hs] + lax.dot_general(vc[:, hs], ku[:, hs], TN, preferred_element_type=F32)
            if fwd:
                tot = o + ob_ref[rows, hs]
                gate = g_ref[r, hs].astype(F32)
                y_ref[r, hs] = (_rms(tot, gn) * (gate * jax.nn.sigmoid(gate))).astype(y_ref.dtype)
            else:
                ob_ref[rows, hs] = o


def _gla_kernel(q_ref, k_ref, v_ref, lr_ref, g_ref, wgf_ref, wgb_ref, bgf_ref, bgb_ref, gn_ref,
                y_ref, st_ref, ob_ref, *, nb):
    n = pl.program_id(1)

    @pl.when((n == 0) | (n == nb))
    def _():
        st_ref[...] = jnp.zeros_like(st_ref)

    @pl.when(n < nb)
    def _():
        _gla_pass(False, nb - 1 - n, q_ref, k_ref, v_ref, lr_ref, g_ref, wgb_ref, bgb_ref, gn_ref, y_ref, st_ref, ob_ref)

    @pl.when(n >= nb)
    def _():
        _gla_pass(True, n - nb, q_ref, k_ref, v_ref, lr_ref, g_ref, wgf_ref, bgf_ref, gn_ref, y_ref, st_ref, ob_ref)


def _gla(p, lr, wgf, wgb, bgf, bgb, gn, bsz, seq):
    ts = min(512, seq)
    nb = seq // ts
    blk = lambda n: jnp.where(n < nb, nb - 1 - n, n - nb)
    oblk = lambda n: jnp.where(n < nb, 0, n - nb)
    col = lambda off: off // GROUP_W
    in_specs = [
        pl.BlockSpec((ts, GROUP_W), lambda b, n: (b * nb + blk(n), col(P_Q))),
        pl.BlockSpec((ts, GROUP_W), lambda b, n: (b * nb + blk(n), col(P_K))),
        pl.BlockSpec((ts, GROUP_W), lambda b, n: (b * nb + blk(n), col(P_V))),
        pl.BlockSpec((ts, LANE), lambda b, n: (b * nb + blk(n), 0)),
        pl.BlockSpec((ts, GROUP_W), lambda b, n: (b * nb + oblk(n), col(P_G))),
        pl.BlockSpec((LANE, GROUP_W), lambda b, n: (0, 0)),
        pl.BlockSpec((LANE, GROUP_W), lambda b, n: (0, 0)),
        pl.BlockSpec((1, GROUP_W), lambda b, n: (0, 0)),
        pl.BlockSpec((1, GROUP_W), lambda b, n: (0, 0)),
        pl.BlockSpec((1, GLA_DV), lambda b, n: (0, 0)),
    ]
    return pl.pallas_call(
        functools.partial(_gla_kernel, nb=nb),
        grid=(bsz, 2 * nb),
        in_specs=in_specs,
        out_specs=pl.BlockSpec((ts, GROUP_W), lambda b, n: (b * nb + oblk(n), 0)),
        out_shape=jax.ShapeDtypeStruct((bsz * seq, GROUP_W), BF16),
        scratch_shapes=[pltpu.VMEM((GLA_HEADS, GLA_DV, LANE), F32), pltpu.VMEM((seq, GROUP_W), F32)],
        compiler_params=_cp("parallel", "arbitrary"),
        name="gla",
    )(p, p, p, lr, p, wgf, wgb, bgf, bgb, gn)


def _fnet_kernel(u_ref, tc_ref, t4_ref, gn_ref, y_ref, a2_ref, *, seq, scale):
    @pl.when(pl.program_id(1) == 0)
    def _():
        rows = min(512, seq)

        def body(i, c):
            r0 = pl.multiple_of(i * rows, rows)
            t = jnp.dot(u_ref[pl.ds(r0, rows), :], tc_ref[...], preferred_element_type=F32)
            a2_ref[pl.ds(r0, rows), :] = t[:, :GROUP_W].astype(BF16)
            a2_ref[pl.ds(pl.multiple_of(seq + r0, rows), rows), :] = t[:, GROUP_W:].astype(BF16)
            return c

        lax.fori_loop(0, seq // rows, body, 0)

    y = jnp.dot(t4_ref[...], a2_ref[...], preferred_element_type=F32) * scale
    y_ref[...] = _rms(y, gn_ref[...]).astype(y_ref.dtype)


def _fnet(p, tc, t4, gn, bsz, seq):
    tm = min(512, seq)
    nm = seq // tm
    return pl.pallas_call(
        functools.partial(_fnet_kernel, seq=seq, scale=1.0 / math.sqrt(seq * FNET_CH)),
        grid=(bsz, nm),
        in_specs=[
            pl.BlockSpec((seq, GROUP_W), lambda b, i: (b, P_UF // GROUP_W)),
            pl.BlockSpec((GROUP_W, 2 * GROUP_W), lambda b, i: (0, 0)),
            pl.BlockSpec((tm, 2 * seq), lambda b, i: (i, 0)),
            pl.BlockSpec((1, GROUP_W), lambda b, i: (0, 0)),
        ],
        out_specs=pl.BlockSpec((tm, GROUP_W), lambda b, i: (b * nm + i, 0)),
        out_shape=jax.ShapeDtypeStruct((bsz * seq, GROUP_W), BF16),
        scratch_shapes=[pltpu.VMEM((2 * seq, GROUP_W), BF16)],
        compiler_params=_cp("parallel", "arbitrary"),
        name="fnet",
    )(p, tc, t4, gn)


def _conv3(main, prev_row, next_row, w):
    ts = main.shape[0]
    rid = lax.broadcasted_iota(jnp.int32, main.shape, 0)
    up = jnp.where(rid == 0, prev_row, pltpu.roll(main, 1, 0))
    dn = jnp.where(rid == ts - 1, next_row, pltpu.roll(main, ts - 1, 0))
    return up * w[0:1] + main * w[1:2] + dn * w[2:3]


def _short_kernel(hm_ref, hp_ref, hn_ref, sm_ref, sp_ref, sn_ref, hw_ref, sw_ref, gn_ref,
                  v_ref, x1_ref, x2_ref, yd_ref, *, nblk):
    i = pl.program_id(0)
    keep_prev = jnp.where(i % nblk == 0, 0.0, 1.0)
    keep_next = jnp.where(i % nblk == nblk - 1, 0.0, 1.0)
    last = BF16_ROWS - 1
    w = GROUP_W
    ch = _conv3(hm_ref[...].astype(F32), hp_ref[last:last + 1, :].astype(F32) * keep_prev,
                hn_ref[0:1, :].astype(F32) * keep_next, hw_ref[...])
    v_ref[...] = ch[:, :w].astype(v_ref.dtype)
    x1_ref[...] = ch[:, w:2 * w].astype(x1_ref.dtype)
    x2_ref[...] = ch[:, 2 * w:].astype(x2_ref.dtype)
    us = sm_ref[...].astype(F32)
    sp = sp_ref[last:last + 1, :].astype(F32)
    sn = sn_ref[0:1, :].astype(F32)
    prod = us[:, w:2 * w] * us[:, 2 * w:]
    conv = _conv3(prod, sp[:, w:2 * w] * sp[:, 2 * w:] * keep_prev, sn[:, w:2 * w] * sn[:, 2 * w:] * keep_next,
                  sw_ref[...])
    yd_ref[...] = _rms(us[:, :w] * conv, gn_ref[...]).astype(yd_ref.dtype)


def _short(p, hy_w, sc_w, gn, bsz, seq):
    m = bsz * seq
    ts = min(256, seq)
    nblk = seq // ts
    hb = ts // BF16_ROWS
    nh = m // BF16_ROWS
    wide = 3 * GROUP_W
    main = lambda cb: pl.BlockSpec((ts, wide), lambda i: (i, cb))
    prev = lambda cb: pl.BlockSpec((BF16_ROWS, wide), lambda i: (jnp.maximum(i * hb - 1, 0), cb))
    nxt = lambda cb: pl.BlockSpec((BF16_ROWS, wide), lambda i: (jnp.minimum((i + 1) * hb, nh - 1), cb))
    out = pl.BlockSpec((ts, GROUP_W), lambda i: (i, 0))
    return pl.pallas_call(
        functools.partial(_short_kernel, nblk=nblk),
        grid=(m // ts,),
        in_specs=[main(0), prev(0), nxt(0), main(1), prev(1), nxt(1),
                  pl.BlockSpec((3, wide), lambda i: (0, 0)),
                  pl.BlockSpec((3, GROUP_W), lambda i: (0, 0)),
                  pl.BlockSpec((1, GROUP_W), lambda i: (0, 0))],
        out_specs=[out, out, out, out],
        out_shape=[jax.ShapeDtypeStruct((m, GROUP_W), BF16)] * 4,
        compiler_params=_cp("parallel"),
        name="short_conv",
    )(p, p, p, p, p, p, hy_w, sc_w, gn)


def _filter_kernel(feat_ref, w1_ref, b1_ref, w2_ref, b2_ref, w3_ref, fr_ref, dc_ref, a_ref, b_ref):
    hp = lax.Precision.HIGHEST
    feats = feat_ref[...]
    fr = fr_ref[...]
    h = jnp.sin(fr * (jnp.dot(feats, w1_ref[...], precision=hp, preferred_element_type=F32) + b1_ref[...]))
    h = jnp.sin(fr * (jnp.dot(h, w2_ref[...], precision=hp, preferred_element_type=F32) + b2_ref[...]))
    h = jnp.dot(h, w3_ref[...], precision=hp, preferred_element_type=F32)
    h = h * jnp.exp(-feats[:, 0:1] * jnp.abs(dc_ref[...]))
    tl = h.shape[0]
    pos = lax.broadcasted_iota(jnp.int32, (tl, GROUP_W), 0) + pl.program_id(1) * tl
    for o in range(2):
        hf = h[:, (2 * o) * GROUP_W:(2 * o + 1) * GROUP_W]
        hb = jnp.where(pos == 0, 0.0, h[:, (2 * o + 1) * GROUP_W:(2 * o + 2) * GROUP_W])
        a_ref[:, o * GROUP_W:(o + 1) * GROUP_W] = (hf + hb).astype(a_ref.dtype)
        b_ref[:, o * GROUP_W:(o + 1) * GROUP_W] = (hb - hf).astype(b_ref.dtype)


def _filters(feats, w1, b1, w2, b2, w3, fr, dc, seq):
    depth = w1.shape[0]
    tl = min(512, seq)
    lay = lambda r, c: pl.BlockSpec((None, r, c), lambda l, i: (l, 0, 0))
    out = pl.BlockSpec((None, tl, 2 * GROUP_W), lambda l, i: (l, i, 0))
    return pl.pallas_call(
        _filter_kernel,
        grid=(depth, seq // tl),
        in_specs=[pl.BlockSpec((tl, LANE), lambda l, i: (i, 0)),
                  lay(LANE, HY_FFN), lay(1, HY_FFN), lay(HY_FFN, HY_FFN), lay(1, HY_FFN),
                  lay(HY_FFN, 4 * GROUP_W), lay(1, HY_FFN), lay(1, 4 * GROUP_W)],
        out_specs=[out, out],
        out_shape=[jax.ShapeDtypeStruct((depth, seq, 2 * GROUP_W), BF16)] * 2,
        compiler_params=_cp("parallel", "parallel"),
        name="hyena_filter",
    )(feats, w1, b1, w2, b2, w3, fr, dc)


def _table_matmul_kernel(t_ref, x_ref, o_ref):
    o_ref[...] = jnp.dot(t_ref[...], x_ref[...], preferred_element_type=F32)


def _table_matmul(t, x):
    depth, k, n = x.shape
    m = t.shape[0]
    tm = min(512, m)
    return pl.pallas_call(
        _table_matmul_kernel,
        grid=(m // tm, depth),
        in_specs=[pl.BlockSpec((tm, k), lambda i, l: (i, 0)),
                  pl.BlockSpec((None, k, n), lambda i, l: (l, 0, 0))],
        out_specs=pl.BlockSpec((None, tm, n), lambda i, l: (l, i, 0)),
        out_shape=jax.ShapeDtypeStruct((depth, m, n), F32),
        compiler_params=_cp("parallel", "arbitrary"),
        name="filter_dft",
    )(t, x)


def _spec_kernel(cf_ref, sf_ref, z_ref, hr_ref, hi_ref, gr_ref, gi_ref):
    z = z_ref[...]
    zr = jnp.dot(cf_ref[...], z, preferred_element_type=F32)
    zs = jnp.dot(sf_ref[...], z, preferred_element_type=F32)
    hr = hr_ref[...]
    hi = hi_ref[...]
    gr_ref[...] = (zr * hr + zs * hi).astype(gr_ref.dtype)
    gi_ref[...] = (zr * hi - zs * hr).astype(gi_ref.dtype)


def _spec(cf, sf, z, hr, hi, layer, order, bsz, seq):
    tf = min(512, seq)
    nf = seq // tf
    tab = pl.BlockSpec((tf, seq), lambda i, b: (i, 0))
    filt = pl.BlockSpec((None, tf, GROUP_W), lambda i, b: (layer, i, order))
    out = pl.BlockSpec((tf, GROUP_W), lambda i, b: (b * nf + i, 0))
    return pl.pallas_call(
        _spec_kernel,
        grid=(nf, bsz),
        in_specs=[tab, tab, pl.BlockSpec((seq, GROUP_W), lambda i, b: (b, 0)), filt, filt],
        out_specs=[out, out],
        out_shape=[jax.ShapeDtypeStruct((bsz * seq, GROUP_W), BF16)] * 2,
        compiler_params=_cp("parallel", "arbitrary"),
        name="hyena_spectrum",
    )(cf, sf, z, hr, hi)


def _inv_kernel(ct_ref, st_ref, gr_ref, gi_ref, xm_ref, w_ref, skip_ref, gn_ref, o_ref, *, inv_len, final):
    y = jnp.dot(ct_ref[...], gr_ref[...], preferred_element_type=F32)
    y = y + jnp.dot(st_ref[...], gi_ref[...], preferred_element_type=F32)
    out = xm_ref[...].astype(F32) * (y * inv_len + w_ref[...].astype(F32) * skip_ref[...])
    if final:
        out = _rms(out, gn_ref[...])
    o_ref[...] = out.astype(o_ref.dtype)


def _inv(ct, stn, gr, gi, xm, w, skip, gn, bsz, seq, final):
    tt = min(512, seq)
    nt = seq // tt
    tab = pl.BlockSpec((tt, seq), lambda i, b: (i, 0))
    full = pl.BlockSpec((seq, GROUP_W), lambda i, b: (b, 0))
    tile = pl.BlockSpec((tt, GROUP_W), lambda i, b: (b * nt + i, 0))
    vec = pl.BlockSpec((1, GROUP_W), lambda i, b: (0, 0))
    return pl.pallas_call(
        functools.partial(_inv_kernel, inv_len=1.0 / seq, final=final),
        grid=(nt, bsz),
        in_specs=[tab, tab, full, full, tile, tile, vec, vec],
        out_specs=tile,
        out_shape=jax.ShapeDtypeStruct((bsz * seq, GROUP_W), BF16),
        compiler_params=_cp("parallel", "arbitrary"),
        name="hyena_inverse",
    )(ct, stn, gr, gi, xm, w, skip, gn)


def _out_kernel(ya_ref, yb_ref, yc_ref, yd_ref, w_ref, x_ref, o_ref):
    acc = x_ref[...]
    for g, y_ref in enumerate((ya_ref, yb_ref, yc_ref, yd_ref)):
        acc = acc + jnp.dot(y_ref[...], w_ref[g * GROUP_W:(g + 1) * GROUP_W, :], preferred_element_type=F32)
    o_ref[...] = acc


def _out_proj(ys, w, x):
    m, n = x.shape
    tm, tn = min(1024, m), min(1024, n)
    piece = pl.BlockSpec((tm, GROUP_W), lambda i, j: (i, 0))
    xs = pl.BlockSpec((tm, tn), lambda i, j: (i, j))
    return pl.pallas_call(
        _out_kernel,
        grid=(m // tm, n // tn),
        in_specs=[piece, piece, piece, piece, pl.BlockSpec((w.shape[0], tn), lambda i, j: (0, j)), xs],
        out_specs=xs,
        out_shape=jax.ShapeDtypeStruct((m, n), F32),
        compiler_params=_cp("parallel", "arbitrary"),
        name="out_proj",
    )(*ys, w, x)


def _attn_kernel(q_ref, k_ref, v_ref, wo_ref, x_ref, o_ref, att_ref):
    @pl.when(pl.program_id(1) == 0)
    def _():
        for h in range(XA_HEADS):
            hs = slice(h * XA_HD, (h + 1) * XA_HD)
            s = lax.dot_general(q_ref[:, hs], k_ref[:, hs], NT, preferred_element_type=F32) * (XA_HD ** -0.5)
            e = jnp.exp(s - jnp.max(s, axis=-1, keepdims=True))
            p = e / jnp.sum(e, axis=-1, keepdims=True)
            att_ref[:, hs] = jnp.dot(p.astype(BF16), v_ref[:, hs], preferred_element_type=F32).astype(BF16)

    o_ref[...] = x_ref[...] + jnp.dot(att_ref[...], wo_ref[...], preferred_element_type=F32)


def _attn(q, kv, wo, x, seq):
    m, d = x.shape
    tm, tn = min(512, seq), min(1024, d)
    per = seq // tm
    xs = pl.BlockSpec((tm, tn), lambda i, j: (i, j))
    return pl.pallas_call(
        _attn_kernel,
        grid=(m // tm, d // tn),
        in_specs=[pl.BlockSpec((tm, d), lambda i, j: (i, 0)),
                  pl.BlockSpec((N_MEM, d), lambda i, j: (i // per, 0)),
                  pl.BlockSpec((N_MEM, d), lambda i, j: (i // per, 1)),
                  pl.BlockSpec((d, tn), lambda i, j: (0, j)),
                  xs],
        out_specs=xs,
        out_shape=jax.ShapeDtypeStruct((m, d), F32),
        scratch_shapes=[pltpu.VMEM((tm, d), BF16)],
        compiler_params=_cp("parallel", "arbitrary"),
        name="cross_attn",
    )(q, kv, kv, wo, x)


def _swiglu_kernel(x_ref, g_ref, wg_ref, wu_ref, wd_ref, fg_ref, o_ref, hn_ref, *, final):
    f = pl.program_id(1)

    @pl.when(f == 0)
    def _():
        _norm_rows_to(x_ref, g_ref, hn_ref)
        o_ref[...] = x_ref[...]

    hn = hn_ref[...]
    gate = jnp.dot(hn, wg_ref[...], preferred_element_type=F32)
    up = jnp.dot(hn, wu_ref[...], preferred_element_type=F32)
    act = (gate * jax.nn.sigmoid(gate) * up).astype(BF16)
    o_ref[...] += jnp.dot(act, wd_ref[...], preferred_element_type=F32)

    if final:
        @pl.when(f == pl.num_programs(1) - 1)
        def _():
            _norm_rows_to(o_ref, fg_ref, o_ref)


def _swiglu(x, g, w_gu, w_d, final_g, final):
    m, d = x.shape
    dff = w_d.shape[0]
    tm, tf = min(512, m), 512
    nf = dff // tf
    xs = pl.BlockSpec((tm, d), lambda i, f: (i, 0))
    vec = pl.BlockSpec((1, d), lambda i, f: (0, 0))
    return pl.pallas_call(
        functools.partial(_swiglu_kernel, final=final),
        grid=(m // tm, nf),
        in_specs=[xs, vec,
                  pl.BlockSpec((d, tf), lambda i, f: (0, f)),
                  pl.BlockSpec((d, tf), lambda i, f: (0, nf + f)),
                  pl.BlockSpec((tf, d), lambda i, f: (f, 0)),
                  vec],
        out_specs=xs,
        out_shape=jax.ShapeDtypeStruct((m, d), F32),
        scratch_shapes=[pltpu.VMEM((tm, d), BF16)],
        compiler_params=_cp("parallel", "arbitrary"),
        name="swiglu",
    )(x, g.reshape(1, d), w_gu, w_gu, w_d, final_g.reshape(1, d))


def _angle_table(rows, cols, period):
    m = (rows[:, None] * cols[None, :]) % period
    return m.astype(F32) * (2.0 * math.pi / period)


def _tables(seq):
    ar = jnp.arange(seq, dtype=jnp.int32)
    ang = _angle_table(ar, ar, seq)
    t4 = jnp.concatenate([jnp.cos(ang), -jnp.sin(ang)], axis=1).astype(BF16)
    ch = jnp.arange(GROUP_W, dtype=jnp.int32)
    same = (ch[:, None] // FNET_CH) == (ch[None, :] // FNET_CH)
    angc = _angle_table(ch % FNET_CH, ch % FNET_CH, FNET_CH)
    tc = jnp.concatenate([jnp.where(same, jnp.cos(angc), 0.0), jnp.where(same, jnp.sin(angc), 0.0)], axis=1).astype(BF16)
    angh = _angle_table(2 * ar + 1, ar, 4 * seq)
    anght = _angle_table(ar, 2 * ar + 1, 4 * seq)
    cf, sf = jnp.cos(angh).astype(BF16), jnp.sin(angh).astype(BF16)
    ct, stn = jnp.cos(anght).astype(BF16), (-jnp.sin(anght)).astype(BF16)
    return t4, tc, cf, sf, ct, stn


def _position_features(seq):
    pos = jnp.arange(seq, dtype=F32)
    t = pos / seq
    f = jnp.linspace(1e-4, HY_BANDS - 1, HY_BANDS, dtype=F32)
    ang = (2.0 * math.pi * t)[:, None] * f[None, :]
    feats = jnp.concatenate([t[:, None], jnp.cos(ang), -jnp.sin(ang)], axis=-1)
    return jnp.pad(feats, ((0, 0), (0, LANE - HY_EMB)))


def _pad_heads(t):
    t = t.reshape(t.shape[:-1] + (GLA_HEADS, GLA_DK))
    t = jnp.pad(t, [(0, 0)] * (t.ndim - 1) + [(0, LANE - GLA_DK)])
    return t.reshape(t.shape[:-2] + (GLA_HEADS * LANE,))


def kernel(x, mem, norm_g, w_in, gla_gk_w, gla_gk_b, gla_norm_g, hy_conv_w, hy_ffn_w1, hy_ffn_b1, hy_ffn_w2, hy_ffn_b2, hy_ffn_w3, hy_sin_freq, hy_decay, hy_skip, sc_conv_w, grp_norm_g, w_out, mem_norm_g, w_xq, w_xkv, w_xo, w_gate_up, w_down, final_norm_g):
    bsz, seq, d = x.shape
    depth = norm_g.shape[0]
    m = bsz * seq
    assert d == D_MODEL and seq % GLA_CHUNK == 0 and mem.shape[1] == N_MEM

    q_w, k_w, v_w, g_w, lr_w, uf_w, uh_w, us_w = jnp.split(
        w_in, [256, 512, 1024, 1536, 1568, 2080, 3616], axis=-1)
    w_p = jnp.concatenate([uh_w, us_w, _pad_heads(q_w), _pad_heads(k_w), v_w, g_w, uf_w], axis=-1).astype(BF16)
    w_lr = jnp.pad(lr_w, ((0, 0), (0, 0), (0, LANE - 2 * GLA_RANK))).astype(BF16)
    gkw = _pad_heads(gla_gk_w)
    wgf = jnp.pad(gkw[:, 0], ((0, 0), (0, LANE - GLA_RANK), (0, 0))).astype(BF16)
    wgb = jnp.pad(gkw[:, 1], ((0, 0), (GLA_RANK, LANE - 2 * GLA_RANK), (0, 0))).astype(BF16)
    gkb = _pad_heads(gla_gk_b).astype(F32)
    w_out_b, w_xq_b, w_xkv_b, w_xo_b = (t.astype(BF16) for t in (w_out, w_xq, w_xkv, w_xo))
    w_gu_b, w_d_b = w_gate_up.astype(BF16), w_down.astype(BF16)
    t4, tc, cf, sf, ct, stn = _tables(seq)
    feats = _position_features(seq)

    w1 = jnp.pad(hy_ffn_w1.astype(F32), ((0, 0), (0, LANE - HY_EMB), (0, 0)))
    fa, fb = _filters(feats, w1, hy_ffn_b1.reshape(depth, 1, HY_FFN).astype(F32), hy_ffn_w2.astype(F32),
                      hy_ffn_b2.reshape(depth, 1, HY_FFN).astype(F32), hy_ffn_w3.astype(F32),
                      hy_sin_freq.reshape(depth, 1, HY_FFN).astype(F32),
                      hy_decay.reshape(depth, 1, 4 * GROUP_W).astype(F32), seq)
    h_re = _table_matmul(cf, fa)
    h_im = _table_matmul(sf, fb)

    mem2 = mem.reshape(bsz * N_MEM, d)
    xf = x.reshape(m, d)
    for l in range(depth):
        p, lr = _norm_matmul(xf, norm_g[l, 0], w_p[l], 1024, 1408, w_side=w_lr[l])
        y_a = _gla(p, lr, wgf[l], wgb[l], gkb[l, 0:1], gkb[l, 1:2], gla_norm_g[l].reshape(1, GLA_DV).astype(F32),
                   bsz, seq)
        y_b = _fnet(p, tc, t4, grp_norm_g[l, 0:1].astype(F32), bsz, seq)
        hv, hx1, hx2, y_d = _short(p, hy_conv_w[l].astype(F32), sc_conv_w[l].astype(F32),
                                   grp_norm_g[l, 2:3].astype(F32), bsz, seq)
        gn_c = grp_norm_g[l, 1:2].astype(F32)
        gr, gi = _spec(cf, sf, hv, h_re, h_im, l, 0, bsz, seq)
        z = _inv(ct, stn, gr, gi, hx1, hv, hy_skip[l, 0:1].astype(F32), gn_c, bsz, seq, final=False)
        gr, gi = _spec(cf, sf, z, h_re, h_im, l, 1, bsz, seq)
        y_c = _inv(ct, stn, gr, gi, hx2, z, hy_skip[l, 1:2].astype(F32), gn_c, bsz, seq, final=True)
        xf = _out_proj((y_a, y_b, y_c, y_d), w_out_b[l], xf)
        q = _norm_matmul(xf, norm_g[l, 1], w_xq_b[l], 1024, 1024)
        kv = _norm_matmul(mem2, mem_norm_g, w_xkv_b[l], 1024, 1024)
        xf = _attn(q, kv, w_xo_b[l], xf, seq)
        xf = _swiglu(xf, norm_g[l, 2], w_gu_b[l], w_d_b[l], final_norm_g, final=(l == depth - 1))
    return xf.reshape(bsz, seq, d)
```

```python
import functools
import math

import jax
import jax.numpy as jnp
from jax import lax
from jax.experimental import pallas as pl
from jax.experimental.pallas import tpu as pltpu

F32 = jnp.float32
BF16 = jnp.bfloat16
EPS = 1e-6

D_MODEL = 2048
GROUP_W = 512
GLA_HEADS = 4
GLA_DK = 64
GLA_DV = 128
GLA_RANK = 16
GLA_GATE_NORM = 16.0
GLA_CHUNK = 64
FNET_CH = 128
HY_BANDS = 16
HY_EMB = 2 * HY_BANDS + 1
HY_FFN = 64
XA_HEADS = 4
XA_HD = D_MODEL // XA_HEADS
N_MEM = 256
D_FF = 5632
RADIX = 4
LANE = 128
BF16_ROWS = 16
VMEM_LIMIT = 56 * 1024 * 1024

P_UH, P_US, P_Q, P_K, P_V, P_G, P_UF = 0, 1536, 3072, 3584, 4096, 4608, 5120
P_W = 5632

NT = (((1,), (1,)), ((), ()))
TN = (((0,), (0,)), ((), ()))


def _cp(*sem):
    return pltpu.CompilerParams(dimension_semantics=sem, vmem_limit_bytes=VMEM_LIMIT)


def _rms(x, g):
    ms = jnp.mean(x * x, axis=-1, keepdims=True)
    return x * lax.rsqrt(ms + EPS) * g


def _norm_rows_to(x_ref, g_ref, hn_ref):
    rows = min(256, x_ref.shape[0])
    n = x_ref.shape[0] // rows

    def body(i, c):
        r = pl.ds(pl.multiple_of(i * rows, rows), rows)
        hn_ref[r, :] = _rms(x_ref[r, :], g_ref[...]).astype(hn_ref.dtype)
        return c

    lax.fori_loop(0, n, body, 0)


def _norm_matmul_kernel(x_ref, g_ref, w_ref, *rest, with_side):
    if with_side:
        w2_ref, o_ref, o2_ref, hn_ref = rest
    else:
        o_ref, hn_ref = rest

    @pl.when(pl.program_id(1) == 0)
    def _():
        _norm_rows_to(x_ref, g_ref, hn_ref)
        if with_side:
            o2_ref[...] = jnp.dot(hn_ref[...], w2_ref[...], preferred_element_type=F32)

    o_ref[...] = jnp.dot(hn_ref[...], w_ref[...], preferred_element_type=F32).astype(o_ref.dtype)


def _norm_matmul(x, g, w, tm, tn, w_side=None):
    m, k = x.shape
    n = w.shape[1]
    tm, tn = min(tm, m), min(tn, n)
    in_specs = [
        pl.BlockSpec((tm, k), lambda i, j: (i, 0)),
        pl.BlockSpec((1, k), lambda i, j: (0, 0)),
        pl.BlockSpec((k, tn), lambda i, j: (0, j)),
    ]
    out_shape = [jax.ShapeDtypeStruct((m, n), BF16)]
    out_specs = [pl.BlockSpec((tm, tn), lambda i, j: (i, j))]
    args = [x, g.reshape(1, k), w]
    if w_side is not None:
        n2 = w_side.shape[1]
        in_specs.append(pl.BlockSpec((k, n2), lambda i, j: (0, 0)))
        out_shape.append(jax.ShapeDtypeStruct((m, n2), F32))
        out_specs.append(pl.BlockSpec((tm, n2), lambda i, j: (i, 0)))
        args.append(w_side)
    res = pl.pallas_call(
        functools.partial(_norm_matmul_kernel, with_side=w_side is not None),
        grid=(m // tm, n // tn),
        in_specs=in_specs,
        out_specs=out_specs,
        out_shape=out_shape,
        scratch_shapes=[pltpu.VMEM((tm, k), BF16)],
        compiler_params=_cp("parallel", "arbitrary"),
        name="norm_matmul",
    )(*args)
    return res if w_side is not None else res[0]


def _gla_pass(fwd, blk, q_ref, k_ref, v_ref, lr_ref, g_ref, wg_ref, bg_ref, gn_ref, y_ref, st_ref, ob_ref):
    ts = q_ref.shape[0]
    c_sz = GLA_CHUNK
    nc = ts // c_sz
    x = jnp.dot(lr_ref[...].astype(BF16), wg_ref[...], preferred_element_type=F32) + bg_ref[...]
    gk = (jnp.minimum(x, 0.0) - jnp.log1p(jnp.exp(-jnp.abs(x)))) * (1.0 / GLA_GATE_NORM)
    hi = gk.astype(BF16)
    lo = (gk - hi.astype(F32)).astype(BF16)
    ii = lax.broadcasted_iota(jnp.int32, (c_sz, c_sz), 0)
    jj = lax.broadcasted_iota(jnp.int32, (c_sz, c_sz), 1)
    incl = (jj <= ii) if fwd else (jj >= ii)
    mask = incl if fwd else (jj > ii)
    tri = jnp.where(incl, 1.0, 0.0).astype(BF16)
    gn = gn_ref[...]
    for c in (range(nc) if fwd else reversed(range(nc))):
        r = slice(c * c_sz, (c + 1) * c_sz)
        b = jnp.dot(tri, hi[r], preferred_element_type=F32) + jnp.dot(tri, lo[r], preferred_element_type=F32)
        bref = b[c_sz // 2:c_sz // 2 + 1] if fwd else b[c_sz // 2 - 1:c_sz // 2]
        btot = b[c_sz - 1:c_sz] if fwd else b[0:1]
        qc = q_ref[r, :].astype(F32) * (GLA_DK ** -0.5)
        kc = k_ref[r, :].astype(F32)
        vc = v_ref[r, :]
        qe_f = qc * jnp.exp(b - bref)
        ke_f = kc * jnp.exp(bref - b)
        qe = qe_f.astype(BF16)
        ke = ke_f.astype(BF16)
        qi = (qe_f * jnp.exp(bref)).astype(BF16)
        ku = (ke_f * jnp.exp(btot - bref)).astype(BF16)
        dec = jnp.exp(btot)
        rows = pl.ds(pl.multiple_of(blk * ts + c * c_sz, c_sz), c_sz)
        for h in range(GLA_HEADS):
            hs = slice(h * LANE, (h + 1) * LANE)
            s = lax.dot_general(qe[:, hs], ke[:, hs], NT, preferred_element_type=F32)
            s = jnp.where(mask, s, 0.0).astype(BF16)
            st = st_ref[h]
            o = jnp.dot(s, vc[:, hs], preferred_element_type=F32)
            o = o + lax.dot_general(qi[:, hs], st.astype(BF16), NT, preferred_element_type=F32)
            st_ref[h] = st * dec[:, hs] + lax.dot_general(vc[:, hs], ku[:, hs], TN, preferred_element_type=F32)
            if fwd:
                tot = o + ob_ref[rows, hs]
                gate = g_ref[r, hs].astype(F32)
                y_ref[r, hs] = (_rms(tot, gn) * (gate * jax.nn.sigmoid(gate))).astype(y_ref.dtype)
            else:
                ob_ref[rows, hs] = o


def _gla_kernel(q_ref, k_ref, v_ref, lr_ref, g_ref, wgf_ref, wgb_ref, bgf_ref, bgb_ref, gn_ref,
                y_ref, st_ref, ob_ref, *, nb):
    n = pl.program_id(1)

    @pl.when((n == 0) | (n == nb))
    def _():
        st_ref[...] = jnp.zeros_like(st_ref)

    @pl.when(n < nb)
    def _():
        _gla_pass(False, nb - 1 - n, q_ref, k_ref, v_ref, lr_ref, g_ref, wgb_ref, bgb_ref, gn_ref, y_ref, st_ref, ob_ref)

    @pl.when(n >= nb)
    def _():
        _gla_pass(True, n - nb, q_ref, k_ref, v_ref, lr_ref, g_ref, wgf_ref, bgf_ref, gn_ref, y_ref, st_ref, ob_ref)


def _gla(p, lr, wgf, wgb, bgf, bgb, gn, bsz, seq):
    ts = min(512, seq)
    nb = seq // ts
    blk = lambda n: jnp.where(n < nb, nb - 1 - n, n - nb)
    oblk = lambda n: jnp.where(n < nb, 0, n - nb)
    col = lambda off: off // GROUP_W
    in_specs = [
        pl.BlockSpec((ts, GROUP_W), lambda b, n: (b * nb + blk(n), col(P_Q))),
        pl.BlockSpec((ts, GROUP_W), lambda b, n: (b * nb + blk(n), col(P_K))),
        pl.BlockSpec((ts, GROUP_W), lambda b, n: (b * nb + blk(n), col(P_V))),
        pl.BlockSpec((ts, LANE), lambda b, n: (b * nb + blk(n), 0)),
        pl.BlockSpec((ts, GROUP_W), lambda b, n: (b * nb + oblk(n), col(P_G))),
        pl.BlockSpec((LANE, GROUP_W), lambda b, n: (0, 0)),
        pl.BlockSpec((LANE, GROUP_W), lambda b, n: (0, 0)),
        pl.BlockSpec((1, GROUP_W), lambda b, n: (0, 0)),
        pl.BlockSpec((1, GROUP_W), lambda b, n: (0, 0)),
        pl.BlockSpec((1, GLA_DV), lambda b, n: (0, 0)),
    ]
    return pl.pallas_call(
        functools.partial(_gla_kernel, nb=nb),
        grid=(bsz, 2 * nb),
        in_specs=in_specs,
        out_specs=pl.BlockSpec((ts, GROUP_W), lambda b, n: (b * nb + oblk(n), 0)),
        out_shape=jax.ShapeDtypeStruct((bsz * seq, GROUP_W), BF16),
        scratch_shapes=[pltpu.VMEM((GLA_HEADS, GLA_DV, LANE), F32), pltpu.VMEM((seq, GROUP_W), F32)],
        compiler_params=_cp("parallel", "arbitrary"),
        name="gla",
    )(p, p, p, lr, p, wgf, wgb, bgf, bgb, gn)


def _residue_products(tab_ref, z_refs):
    a = [jnp.dot(tab_ref[r], z_refs[r][...], preferred_element_type=F32) for r in range(RADIX)]
    b = [jnp.dot(tab_ref[RADIX + r], z_refs[r][...], preferred_element_type=F32) for r in range(RADIX)]
    return a, b


def _butterflies(a, b):
    a02p, a02m, a13p, a13m = a[0] + a[2], a[0] - a[2], a[1] + a[3], a[1] - a[3]
    b02p, b02m, b13p, b13m = b[0] + b[2], b[0] - b[2], b[1] + b[3], b[1] - b[3]
    re = [a02p + a13p, a02m - b13m, a02p - a13p, a02m + b13m]
    nim = [b02p + b13p, b02m + a13m, b02p - b13p, b02m - a13m]
    return re, nim


def _fnet_kernel(tab_ref, x0, x1, x2, x3, cc_ref, sc_ref, gn_ref, y_ref, *, scale):
    a, b = _residue_products(tab_ref, (x0, x1, x2, x3))
    re, nim = _butterflies(a, b)
    for m in range(RADIX):
        y = jnp.dot(re[m].astype(BF16), cc_ref[...], preferred_element_type=F32)
        y = y - jnp.dot(nim[m].astype(BF16), sc_ref[...], preferred_element_type=F32)
        y_ref[m] = _rms(y * scale, gn_ref[...]).astype(y_ref.dtype)


def _fnet(p, tab, cc, sc, gn, bsz, seq):
    q = seq // RADIX
    tq = min(256, q)
    nq = q // tq
    p4 = p.reshape(bsz * q, RADIX * P_W)
    cpb = P_W // GROUP_W
    res = lambda r: pl.BlockSpec((q, GROUP_W), lambda i, b: (b, r * cpb + P_UF // GROUP_W))
    sq = pl.BlockSpec((GROUP_W, GROUP_W), lambda i, b: (0, 0))
    out = pl.pallas_call(
        functools.partial(_fnet_kernel, scale=1.0 / math.sqrt(seq * FNET_CH)),
        grid=(nq, bsz),
        in_specs=[pl.BlockSpec((2 * RADIX, tq, q), lambda i, b: (0, i, 0)),
                  res(0), res(1), res(2), res(3), sq, sq,
                  pl.BlockSpec((1, GROUP_W), lambda i, b: (0, 0))],
        out_specs=pl.BlockSpec((None, RADIX, tq, GROUP_W), lambda i, b: (b, 0, i, 0)),
        out_shape=jax.ShapeDtypeStruct((bsz, RADIX, q, GROUP_W), BF16),
        compiler_params=_cp("parallel", "arbitrary"),
        name="fnet",
    )(tab, p4, p4, p4, p4, cc, sc, gn)
    return out.reshape(bsz * seq, GROUP_W)


def _conv3(main, prev_row, next_row, w):
    ts = main.shape[0]
    rid = lax.broadcasted_iota(jnp.int32, main.shape, 0)
    up = jnp.where(rid == 0, prev_row, pltpu.roll(main, 1, 0))
    dn = jnp.where(rid == ts - 1, next_row, pltpu.roll(main, ts - 1, 0))
    return up * w[0:1] + main * w[1:2] + dn * w[2:3]


def _short_kernel(hm_ref, hp_ref, hn_ref, sm_ref, sp_ref, sn_ref, hw_ref, sw_ref, gn_ref,
                  v_ref, x1_ref, x2_ref, yd_ref, *, nblk):
    i = pl.program_id(0)
    keep_prev = jnp.where(i % nblk == 0, 0.0, 1.0)
    keep_next = jnp.where(i % nblk == nblk - 1, 0.0, 1.0)
    last = BF16_ROWS - 1
    w = GROUP_W
    ch = _conv3(hm_ref[...].astype(F32), hp_ref[last:last + 1, :].astype(F32) * keep_prev,
                hn_ref[0:1, :].astype(F32) * keep_next, hw_ref[...])
    v_ref[...] = ch[:, :w].astype(v_ref.dtype)
    x1_ref[...] = ch[:, w:2 * w].astype(x1_ref.dtype)
    x2_ref[...] = ch[:, 2 * w:].astype(x2_ref.dtype)
    us = sm_ref[...].astype(F32)
    sp = sp_ref[last:last + 1, :].astype(F32)
    sn = sn_ref[0:1, :].astype(F32)
    prod = us[:, w:2 * w] * us[:, 2 * w:]
    conv = _conv3(prod, sp[:, w:2 * w] * sp[:, 2 * w:] * keep_prev, sn[:, w:2 * w] * sn[:, 2 * w:] * keep_next,
                  sw_ref[...])
    yd_ref[...] = _rms(us[:, :w] * conv, gn_ref[...]).astype(yd_ref.dtype)


def _short(p, hy_w, sc_w, gn, bsz, seq):
    m = bsz * seq
    ts = min(256, seq)
    nblk = seq // ts
    hb = ts // BF16_ROWS
    nh = m // BF16_ROWS
    wide = 3 * GROUP_W
    main = lambda cb: pl.BlockSpec((ts, wide), lambda i: (i, cb))
    prev = lambda cb: pl.BlockSpec((BF16_ROWS, wide), lambda i: (jnp.maximum(i * hb - 1, 0), cb))
    nxt = lambda cb: pl.BlockSpec((BF16_ROWS, wide), lambda i: (jnp.minimum((i + 1) * hb, nh - 1), cb))
    out = pl.BlockSpec((ts, GROUP_W), lambda i: (i, 0))
    return pl.pallas_call(
        functools.partial(_short_kernel, nblk=nblk),
        grid=(m // ts,),
        in_specs=[main(0), prev(0), nxt(0), main(1), prev(1), nxt(1),
                  pl.BlockSpec((3, wide), lambda i: (0, 0)),
                  pl.BlockSpec((3, GROUP_W), lambda i: (0, 0)),
                  pl.BlockSpec((1, GROUP_W), lambda i: (0, 0))],
        out_specs=[out, out, out, out],
        out_shape=[jax.ShapeDtypeStruct((m, GROUP_W), BF16)] * 4,
        compiler_params=_cp("parallel"),
        name="short_conv",
    )(p, p, p, p, p, p, hy_w, sc_w, gn)


def _filter_kernel(feat_ref, w1_ref, b1_ref, w2_ref, b2_ref, w3_ref, fr_ref, dc_ref, a_ref, b_ref):
    hp = lax.Precision.HIGHEST
    feats = feat_ref[...]
    fr = fr_ref[...]
    h = jnp.sin(fr * (jnp.dot(feats, w1_ref[...], precision=hp, preferred_element_type=F32) + b1_ref[...]))
    h = jnp.sin(fr * (jnp.dot(h, w2_ref[...], precision=hp, preferred_element_type=F32) + b2_ref[...]))
    h = jnp.dot(h, w3_ref[...], precision=hp, preferred_element_type=F32)
    h = h * jnp.exp(-feats[:, 0:1] * jnp.abs(dc_ref[...]))
    tl = h.shape[0]
    pos = lax.broadcasted_iota(jnp.int32, (tl, GROUP_W), 0) + pl.program_id(1) * tl
    for o in range(2):
        hf = h[:, (2 * o) * GROUP_W:(2 * o + 1) * GROUP_W]
        hb = jnp.where(pos == 0, 0.0, h[:, (2 * o + 1) * GROUP_W:(2 * o + 2) * GROUP_W])
        a_ref[:, o * GROUP_W:(o + 1) * GROUP_W] = (hf + hb).astype(a_ref.dtype)
        b_ref[:, o * GROUP_W:(o + 1) * GROUP_W] = (hb - hf).astype(b_ref.dtype)


def _filters(feats, w1, b1, w2, b2, w3, fr, dc, seq):
    depth = w1.shape[0]
    tl = min(512, seq)
    lay = lambda r, c: pl.BlockSpec((None, r, c), lambda l, i: (l, 0, 0))
    out = pl.BlockSpec((None, tl, 2 * GROUP_W), lambda l, i: (l, i, 0))
    return pl.pallas_call(
        _filter_kernel,
        grid=(depth, seq // tl),
        in_specs=[pl.BlockSpec((tl, LANE), lambda l, i: (i, 0)),
                  lay(LANE, HY_FFN), lay(1, HY_FFN), lay(HY_FFN, HY_FFN), lay(1, HY_FFN),
                  lay(HY_FFN, 4 * GROUP_W), lay(1, HY_FFN), lay(1, 4 * GROUP_W)],
        out_specs=[out, out],
        out_shape=[jax.ShapeDtypeStruct((depth, seq, 2 * GROUP_W), BF16)] * 2,
        compiler_params=_cp("parallel", "parallel"),
        name="hyena_filter",
    )(feats, w1, b1, w2, b2, w3, fr, dc)


def _hyena_classes(a, b):
    re, nim = _butterflies(a, b)
    return re, [nim[0], nim[1], -nim[2], -nim[3]]


def _spectrum_kernel(tab_ref, z0, z1, z2, z3, o_ref):
    a, b = _residue_products(tab_ref, (z0, z1, z2, z3))
    zr, zs = _hyena_classes(a, b)
    for m in range(RADIX):
        o_ref[m] = zr[m]
        o_ref[RADIX + m] = zs[m]


def _filter_spectrum(tab, filt, seq):
    depth, _, cols = filt.shape
    q = seq // RADIX
    tq = min(256, q)
    ncb = cols // GROUP_W
    f4 = filt.reshape(depth, q, RADIX * cols)
    res = lambda r: pl.BlockSpec((None, q, GROUP_W), lambda i, n: (n // ncb, 0, r * ncb + n % ncb))
    return pl.pallas_call(
        _spectrum_kernel,
        grid=(q // tq, depth * ncb),
        in_specs=[pl.BlockSpec((2 * RADIX, tq, q), lambda i, n: (0, i, 0)), res(0), res(1), res(2), res(3)],
        out_specs=pl.BlockSpec((None, 2 * RADIX, tq, GROUP_W), lambda i, n: (n // ncb, 0, i, n % ncb)),
        out_shape=jax.ShapeDtypeStruct((depth, 2 * RADIX, q, cols), F32),
        compiler_params=_cp("parallel", "arbitrary"),
        name="filter_spectrum",
    )(tab, f4, f4, f4, f4)


def _conv_spectrum_kernel(tab_ref, z0, z1, z2, z3, hr_ref, hi_ref, p_ref):
    a, b = _residue_products(tab_ref, (z0, z1, z2, z3))
    zr, zs = _hyena_classes(a, b)
    gr, gi = [], []
    for m in range(RADIX):
        hr, hi = hr_ref[m], hi_ref[m]
        gr.append(zr[m] * hr + zs[m] * hi)
        gi.append(zr[m] * hi - zs[m] * hr)
    s02, d02, s13, d13 = gr[0] + gr[2], gr[0] - gr[2], gr[1] + gr[3], gr[1] - gr[3]
    t02, e02, t13, e13 = gi[0] + gi[2], gi[0] - gi[2], gi[1] + gi[3], gi[1] - gi[3]
    pc = [s02 + s13, d02 - t13, s02 - s13, d02 + t13]
    ps = [-e02 - e13, -t02 - d13, e13 - e02, d13 - t02]
    for r in range(RADIX):
        p_ref[r] = pc[r].astype(p_ref.dtype)
        p_ref[RADIX + r] = ps[r].astype(p_ref.dtype)


def _conv_spectrum(tab, z, h_cos, h_sin, layer, order, bsz, seq):
    q = seq // RADIX
    tq = min(256, q)
    z4 = z.reshape(bsz * q, RADIX * GROUP_W)
    res = lambda r: pl.BlockSpec((q, GROUP_W), lambda i, b: (b, r))
    filt = lambda half: pl.BlockSpec((None, RADIX, tq, GROUP_W), lambda i, b: (layer, half, i, order))
    return pl.pallas_call(
        _conv_spectrum_kernel,
        grid=(q // tq, bsz),
        in_specs=[pl.BlockSpec((2 * RADIX, tq, q), lambda i, b: (0, i, 0)), res(0), res(1), res(2), res(3),
                  filt(0), filt(1)],
        out_specs=pl.BlockSpec((None, 2 * RADIX, tq, GROUP_W), lambda i, b: (b, 0, i, 0)),
        out_shape=jax.ShapeDtypeStruct((bsz, 2 * RADIX, q, GROUP_W), BF16),
        compiler_params=_cp("parallel", "arbitrary"),
        name="hyena_spectrum",
    )(tab, z4, z4, z4, z4, h_cos, h_sin)


def _inverse_kernel(tab_ref, p_ref, xm_ref, w_ref, skip_ref, gn_ref, o_ref, *, inv_len, final):
    for r in range(RADIX):
        cs = slice(r * GROUP_W, (r + 1) * GROUP_W)
        y = jnp.dot(tab_ref[r], p_ref[r], preferred_element_type=F32)
        y = y + jnp.dot(tab_ref[RADIX + r], p_ref[RADIX + r], preferred_element_type=F32)
        out = xm_ref[:, cs].astype(F32) * (y * inv_len + w_ref[:, cs].astype(F32) * skip_ref[...])
        if final:
            out = _rms(out, gn_ref[...])
        o_ref[:, cs] = out.astype(o_ref.dtype)


def _inverse(tab_t, pcs, xm, w, skip, gn, bsz, seq, final):
    q = seq // RADIX
    tj = min(256, q)
    nj = q // tj
    wide = RADIX * GROUP_W
    tile = pl.BlockSpec((tj, wide), lambda i, b: (b * nj + i, 0))
    vec = pl.BlockSpec((1, GROUP_W), lambda i, b: (0, 0))
    out = pl.pallas_call(
        functools.partial(_inverse_kernel, inv_len=1.0 / seq, final=final),
        grid=(nj, bsz),
        in_specs=[pl.BlockSpec((2 * RADIX, tj, q), lambda i, b: (0, i, 0)),
                  pl.BlockSpec((None, 2 * RADIX, q, GROUP_W), lambda i, b: (b, 0, 0, 0)),
                  tile, tile, vec, vec],
        out_specs=tile,
        out_shape=jax.ShapeDtypeStruct((bsz * q, wide), BF16),
        compiler_params=_cp("parallel", "arbitrary"),
        name="hyena_inverse",
    )(tab_t, pcs, xm.reshape(bsz * q, wide), w.reshape(bsz * q, wide), skip, gn)
    return out.reshape(bsz * seq, GROUP_W)


def _out_kernel(ya_ref, yb_ref, yc_ref, yd_ref, w_ref, x_ref, o_ref):
    acc = x_ref[...]
    for g, y_ref in enumerate((ya_ref, yb_ref, yc_ref, yd_ref)):
        acc = acc + jnp.dot(y_ref[...], w_ref[g * GROUP_W:(g + 1) * GROUP_W, :], preferred_element_type=F32)
    o_ref[...] = acc


def _out_proj(ys, w, x):
    m, n = x.shape
    tm, tn = min(1024, m), min(1024, n)
    piece = pl.BlockSpec((tm, GROUP_W), lambda i, j: (i, 0))
    xs = pl.BlockSpec((tm, tn), lambda i, j: (i, j))
    return pl.pallas_call(
        _out_kernel,
        grid=(m // tm, n // tn),
        in_specs=[piece, piece, piece, piece, pl.BlockSpec((w.shape[0], tn), lambda i, j: (0, j)), xs],
        out_specs=xs,
        out_shape=jax.ShapeDtypeStruct((m, n), F32),
        compiler_params=_cp("parallel", "arbitrary"),
        name="out_proj",
    )(*ys, w, x)


def _attn_kernel(q_ref, k_ref, v_ref, wo_ref, x_ref, o_ref, att_ref):
    @pl.when(pl.program_id(1) == 0)
    def _():
        for h in range(XA_HEADS):
            hs = slice(h * XA_HD, (h + 1) * XA_HD)
            s = lax.dot_general(q_ref[:, hs], k_ref[:, hs], NT, preferred_element_type=F32) * (XA_HD ** -0.5)
            e = jnp.exp(s - jnp.max(s, axis=-1, keepdims=True))
            p = e / jnp.sum(e, axis=-1, keepdims=True)
            att_ref[:, hs] = jnp.dot(p.astype(BF16), v_ref[:, hs], preferred_element_type=F32).astype(BF16)

    o_ref[...] = x_ref[...] + jnp.dot(att_ref[...], wo_ref[...], preferred_element_type=F32)


def _attn(q, kv, wo, x, seq):
    m, d = x.shape
    tm, tn = min(512, seq), min(1024, d)
    per = seq // tm
    xs = pl.BlockSpec((tm, tn), lambda i, j: (i, j))
    return pl.pallas_call(
        _attn_kernel,
        grid=(m // tm, d // tn),
        in_specs=[pl.BlockSpec((tm, d), lambda i, j: (i, 0)),
                  pl.BlockSpec((N_MEM, d), lambda i, j: (i // per, 0)),
                  pl.BlockSpec((N_MEM, d), lambda i, j: (i // per, 1)),
                  pl.BlockSpec((d, tn), lambda i, j: (0, j)),
                  xs],
        out_specs=xs,
        out_shape=jax.ShapeDtypeStruct((m, d), F32),
        scratch_shapes=[pltpu.VMEM((tm, d), BF16)],
        compiler_params=_cp("parallel", "arbitrary"),
        name="cross_attn",
    )(q, kv, kv, wo, x)


def _swiglu_kernel(x_ref, g_ref, wg_ref, wu_ref, wd_ref, fg_ref, o_ref, hn_ref, *, final):
    f = pl.program_id(1)

    @pl.when(f == 0)
    def _():
        _norm_rows_to(x_ref, g_ref, hn_ref)
        o_ref[...] = x_ref[...]

    hn = hn_ref[...]
    gate = jnp.dot(hn, wg_ref[...], preferred_element_type=F32)
    up = jnp.dot(hn, wu_ref[...], preferred_element_type=F32)
    act = (gate * jax.nn.sigmoid(gate) * up).astype(BF16)
    o_ref[...] += jnp.dot(act, wd_ref[...], preferred_element_type=F32)

    if final:
        @pl.when(f == pl.num_programs(1) - 1)
        def _():
            _norm_rows_to(o_ref, fg_ref, o_ref)


def _swiglu(x, g, w_gu, w_d, final_g, final):
    m, d = x.shape
    dff = w_d.shape[0]
    tm, tf = min(512, m), 512
    nf = dff // tf
    xs = pl.BlockSpec((tm, d), lambda i, f: (i, 0))
    vec = pl.BlockSpec((1, d), lambda i, f: (0, 0))
    return pl.pallas_call(
        functools.partial(_swiglu_kernel, final=final),
        grid=(m // tm, nf),
        in_specs=[xs, vec,
                  pl.BlockSpec((d, tf), lambda i, f: (0, f)),
                  pl.BlockSpec((d, tf), lambda i, f: (0, nf + f)),
                  pl.BlockSpec((tf, d), lambda i, f: (f, 0)),
                  vec],
        out_specs=xs,
        out_shape=jax.ShapeDtypeStruct((m, d), F32),
        scratch_shapes=[pltpu.VMEM((tm, d), BF16)],
        compiler_params=_cp("parallel", "arbitrary"),
        name="swiglu",
    )(x, g.reshape(1, d), w_gu, w_gu, w_d, final_g.reshape(1, d))


def _angles(rows, cols, period):
    m = (rows[..., :, None] * cols[..., None, :]) % period
    return m.astype(F32) * (2.0 * math.pi / period)


def _cos_sin(ang, sin_sign=1.0):
    return jnp.concatenate([jnp.cos(ang), sin_sign * jnp.sin(ang)], axis=0).astype(BF16)


def _tables(seq):
    q = seq // RADIX
    base = jnp.arange(q, dtype=jnp.int32)
    times = RADIX * base[None, :] + jnp.arange(RADIX, dtype=jnp.int32)[:, None]
    rep = jnp.broadcast_to(base, (RADIX, q))
    fnet_tab = _cos_sin(_angles(rep, times, seq))
    ch = jnp.arange(GROUP_W, dtype=jnp.int32)
    same = (ch[:, None] // FNET_CH) == (ch[None, :] // FNET_CH)
    angc = _angles(ch % FNET_CH, ch % FNET_CH, FNET_CH)
    cc = jnp.where(same, jnp.cos(angc), 0.0).astype(BF16)
    sc = jnp.where(same, jnp.sin(angc), 0.0).astype(BF16)
    hy_tab = _cos_sin(_angles(2 * rep + 1, times, 4 * seq))
    hy_tab_t = _cos_sin(_angles(times, 2 * rep + 1, 4 * seq))
    return fnet_tab, cc, sc, hy_tab, hy_tab_t


def _position_features(seq):
    pos = jnp.arange(seq, dtype=F32)
    t = pos / seq
    f = jnp.linspace(1e-4, HY_BANDS - 1, HY_BANDS, dtype=F32)
    ang = (2.0 * math.pi * t)[:, None] * f[None, :]
    feats = jnp.concatenate([t[:, None], jnp.cos(ang), -jnp.sin(ang)], axis=-1)
    return jnp.pad(feats, ((0, 0), (0, LANE - HY_EMB)))


def _pad_heads(t):
    t = t.reshape(t.shape[:-1] + (GLA_HEADS, GLA_DK))
    t = jnp.pad(t, [(0, 0)] * (t.ndim - 1) + [(0, LANE - GLA_DK)])
    return t.reshape(t.shape[:-2] + (GLA_HEADS * LANE,))


def kernel(x, mem, norm_g, w_in, gla_gk_w, gla_gk_b, gla_norm_g, hy_conv_w, hy_ffn_w1, hy_ffn_b1, hy_ffn_w2, hy_ffn_b2, hy_ffn_w3, hy_sin_freq, hy_decay, hy_skip, sc_conv_w, grp_norm_g, w_out, mem_norm_g, w_xq, w_xkv, w_xo, w_gate_up, w_down, final_norm_g):
    bsz, seq, d = x.shape
    depth = norm_g.shape[0]
    m = bsz * seq
    assert d == D_MODEL and seq % (RADIX * GLA_CHUNK) == 0 and mem.shape[1] == N_MEM

    q_w, k_w, v_w, g_w, lr_w, uf_w, uh_w, us_w = jnp.split(
        w_in, [256, 512, 1024, 1536, 1568, 2080, 3616], axis=-1)
    w_p = jnp.concatenate([uh_w, us_w, _pad_heads(q_w), _pad_heads(k_w), v_w, g_w, uf_w], axis=-1).astype(BF16)
    w_lr = jnp.pad(lr_w, ((0, 0), (0, 0), (0, LANE - 2 * GLA_RANK))).astype(BF16)
    gkw = _pad_heads(gla_gk_w)
    wgf = jnp.pad(gkw[:, 0], ((0, 0), (0, LANE - GLA_RANK), (0, 0))).astype(BF16)
    wgb = jnp.pad(gkw[:, 1], ((0, 0), (GLA_RANK, LANE - 2 * GLA_RANK), (0, 0))).astype(BF16)
    gkb = _pad_heads(gla_gk_b).astype(F32)
    w_out_b, w_xq_b, w_xkv_b, w_xo_b = (t.astype(BF16) for t in (w_out, w_xq, w_xkv, w_xo))
    w_gu_b, w_d_b = w_gate_up.astype(BF16), w_down.astype(BF16)
    fnet_tab, cc, sc, hy_tab, hy_tab_t = _tables(seq)
    feats = _position_features(seq)

    w1 = jnp.pad(hy_ffn_w1.astype(F32), ((0, 0), (0, LANE - HY_EMB), (0, 0)))
    fa, fb = _filters(feats, w1, hy_ffn_b1.reshape(depth, 1, HY_FFN).astype(F32), hy_ffn_w2.astype(F32),
                      hy_ffn_b2.reshape(depth, 1, HY_FFN).astype(F32), hy_ffn_w3.astype(F32),
                      hy_sin_freq.reshape(depth, 1, HY_FFN).astype(F32),
                      hy_decay.reshape(depth, 1, 4 * GROUP_W).astype(F32), seq)
    h_cos = _filter_spectrum(hy_tab, fa, seq)
    h_sin = _filter_spectrum(hy_tab, fb, seq)

    mem2 = mem.reshape(bsz * N_MEM, d)
    xf = x.reshape(m, d)
    for l in range(depth):
        p, lr = _norm_matmul(xf, norm_g[l, 0], w_p[l], 1024, 1408, w_side=w_lr[l])
        y_a = _gla(p, lr, wgf[l], wgb[l], gkb[l, 0:1], gkb[l, 1:2], gla_norm_g[l].reshape(1, GLA_DV).astype(F32),
                   bsz, seq)
        y_b = _fnet(p, fnet_tab, cc, sc, grp_norm_g[l, 0:1].astype(F32), bsz, seq)
        hv, hx1, hx2, y_d = _short(p, hy_conv_w[l].astype(F32), sc_conv_w[l].astype(F32),
                                   grp_norm_g[l, 2:3].astype(F32), bsz, seq)
        gn_c = grp_norm_g[l, 1:2].astype(F32)
        pcs = _conv_spectrum(hy_tab, hv, h_cos, h_sin, l, 0, bsz, seq)
        z = _inverse(hy_tab_t, pcs, hx1, hv, hy_skip[l, 0:1].astype(F32), gn_c, bsz, seq, final=False)
        pcs = _conv_spectrum(hy_tab, z, h_cos, h_sin, l, 1, bsz, seq)
        y_c = _inverse(hy_tab_t, pcs, hx2, z, hy_skip[l, 1:2].astype(F32), gn_c, bsz, seq, final=True)
        xf = _out_proj((y_a, y_b, y_c, y_d), w_out_b[l], xf)
        q = _norm_matmul(xf, norm_g[l, 1], w_xq_b[l], 1024, 1024)
        kv = _norm_matmul(mem2, mem_norm_g, w_xkv_b[l], 1024, 1024)
        xf = _attn(q, kv, w_xo_b[l], xf, seq)
        xf = _swiglu(xf, norm_g[l, 2], w_gu_b[l], w_d_b[l], final_norm_g, final=(l == depth - 1))
    return xf.reshape(bsz, seq, d)
```

```python
import functools
import math

import jax
import jax.numpy as jnp
from jax import lax
from jax.experimental import pallas as pl
from jax.experimental.pallas import tpu as pltpu

F32 = jnp.float32
BF16 = jnp.bfloat16
EPS = 1e-6

D_MODEL = 2048
GROUP_W = 512
GLA_HEADS = 4
GLA_DK = 64
GLA_DV = 128
GLA_RANK = 16
GLA_GATE_NORM = 16.0
GLA_CHUNK = 64
FNET_CH = 128
HY_BANDS = 16
HY_EMB = 2 * HY_BANDS + 1
HY_FFN = 64
XA_HEADS = 4
XA_HD = D_MODEL // XA_HEADS
N_MEM = 256
D_FF = 5632
RADIX = 4
LANE = 128
BF16_ROWS = 16
VMEM_LIMIT = 56 * 1024 * 1024

P_UH, P_US, P_Q, P_K, P_V, P_G, P_UF = 0, 1536, 3072, 3584, 4096, 4608, 5120
P_W = 5632

NT = (((1,), (1,)), ((), ()))
TN = (((0,), (0,)), ((), ()))


def _cp(*sem):
    return pltpu.CompilerParams(dimension_semantics=sem, vmem_limit_bytes=VMEM_LIMIT)


def _resident(shape, index_map):
    return pl.BlockSpec(shape, index_map, pipeline_mode=pl.Buffered(1))


def _rms(x, g):
    ms = jnp.mean(x * x, axis=-1, keepdims=True)
    return x * lax.rsqrt(ms + EPS) * g


def _norm_rows_to(x_ref, g_ref, hn_ref):
    rows = min(256, x_ref.shape[0])
    n = x_ref.shape[0] // rows

    def body(i, c):
        r = pl.ds(pl.multiple_of(i * rows, rows), rows)
        hn_ref[r, :] = _rms(x_ref[r, :], g_ref[...]).astype(hn_ref.dtype)
        return c

    lax.fori_loop(0, n, body, 0)


def _norm_matmul_kernel(x_ref, g_ref, w_ref, *rest, with_side):
    if with_side:
        w2_ref, o_ref, o2_ref, hn_ref = rest
    else:
        o_ref, hn_ref = rest

    @pl.when(pl.program_id(1) == 0)
    def _():
        _norm_rows_to(x_ref, g_ref, hn_ref)
        if with_side:
            o2_ref[...] = jnp.dot(hn_ref[...], w2_ref[...], preferred_element_type=F32)

    o_ref[...] = jnp.dot(hn_ref[...], w_ref[...], preferred_element_type=F32).astype(o_ref.dtype)


def _norm_matmul(x, g, w, tm, tn, w_side=None):
    m, k = x.shape
    n = w.shape[1]
    tm, tn = min(tm, m), min(tn, n)
    in_specs = [
        pl.BlockSpec((tm, k), lambda i, j: (i, 0)),
        pl.BlockSpec((1, k), lambda i, j: (0, 0)),
        _resident((k, tn), lambda i, j: (0, j)) if tn == n else pl.BlockSpec((k, tn), lambda i, j: (0, j)),
    ]
    out_shape = [jax.ShapeDtypeStruct((m, n), BF16)]
    out_specs = [pl.BlockSpec((tm, tn), lambda i, j: (i, j))]
    args = [x, g.reshape(1, k), w]
    if w_side is not None:
        n2 = w_side.shape[1]
        in_specs.append(pl.BlockSpec((k, n2), lambda i, j: (0, 0)))
        out_shape.append(jax.ShapeDtypeStruct((m, n2), F32))
        out_specs.append(pl.BlockSpec((tm, n2), lambda i, j: (i, 0)))
        args.append(w_side)
    res = pl.pallas_call(
        functools.partial(_norm_matmul_kernel, with_side=w_side is not None),
        grid=(m // tm, n // tn),
        in_specs=in_specs,
        out_specs=out_specs,
        out_shape=out_shape,
        scratch_shapes=[pltpu.VMEM((tm, k), BF16)],
        compiler_params=_cp("parallel", "arbitrary"),
        name="norm_matmul",
    )(*args)
    return res if w_side is not None else res[0]


def _gla_pass(fwd, blk, q_ref, k_ref, v_ref, lr_ref, g_ref, wg_ref, bg_ref, gn_ref, y_ref, st_ref, ob_ref):
    ts = q_ref.shape[0]
    c_sz = GLA_CHUNK
    nc = ts // c_sz
    x = jnp.dot(lr_ref[...].astype(BF16), wg_ref[...], preferred_element_type=F32) + bg_ref[...]
    gk = (jnp.minimum(x, 0.0) - jnp.log1p(jnp.exp(-jnp.abs(x)))) * (1.0 / GLA_GATE_NORM)
    hi = gk.astype(BF16)
    lo = (gk - hi.astype(F32)).astype(BF16)
    ii = lax.broadcasted_iota(jnp.int32, (c_sz, c_sz), 0)
    jj = lax.broadcasted_iota(jnp.int32, (c_sz, c_sz), 1)
    incl = (jj <= ii) if fwd else (jj >= ii)
    mask = incl if fwd else (jj > ii)
    tri = jnp.where(incl, 1.0, 0.0).astype(BF16)
    gn = gn_ref[...]
    for c in (range(nc) if fwd else reversed(range(nc))):
        r = slice(c * c_sz, (c + 1) * c_sz)
        b = jnp.dot(tri, hi[r], preferred_element_type=F32) + jnp.dot(tri, lo[r], preferred_element_type=F32)
        bref = b[c_sz // 2:c_sz // 2 + 1] if fwd else b[c_sz // 2 - 1:c_sz // 2]
        btot = b[c_sz - 1:c_sz] if fwd else b[0:1]
        qc = q_ref[r, :].astype(F32) * (GLA_DK ** -0.5)
        kc = k_ref[r, :].astype(F32)
        vc = v_ref[r, :]
        qe_f = qc * jnp.exp(b - bref)
        ke_f = kc * jnp.exp(bref - b)
        qe = qe_f.astype(BF16)
        ke = ke_f.astype(BF16)
        qi = (qe_f * jnp.exp(bref)).astype(BF16)
        ku = (ke_f * jnp.exp(btot - bref)).astype(BF16)
        dec = jnp.exp(btot)
        rows = pl.ds(pl.multiple_of(blk * ts + c * c_sz, c_sz), c_sz)
        for h in range(GLA_HEADS):
            hs = slice(h * LANE, (h + 1) * LANE)
            s = lax.dot_general(qe[:, hs], ke[:, hs], NT, preferred_element_type=F32)
            s = jnp.where(mask, s, 0.0).astype(BF16)
            st = st_ref[h]
            o = jnp.dot(s, vc[:, hs], preferred_element_type=F32)
            o = o + lax.dot_general(qi[:, hs], st.astype(BF16), NT, preferred_element_type=F32)
            st_ref[h] = st * dec[:, hs] + lax.dot_general(vc[:, hs], ku[:, hs], TN, preferred_element_type=F32)
            if fwd:
                tot = o + ob_ref[rows, hs]
                gate = g_ref[r, hs].astype(F32)
                y_ref[r, hs] = (_rms(tot, gn) * (gate * jax.nn.sigmoid(gate))).astype(y_ref.dtype)
            else:
                ob_ref[rows, hs] = o


def _gla_kernel(q_ref, k_ref, v_ref, lr_ref, g_ref, wgf_ref, wgb_ref, bgf_ref, bgb_ref, gn_ref,
                y_ref, st_ref, ob_ref, *, nb):
    n = pl.program_id(1)

    @pl.when((n == 0) | (n == nb))
    def _():
        st_ref[...] = jnp.zeros_like(st_ref)

    @pl.when(n < nb)
    def _():
        _gla_pass(False, nb - 1 - n, q_ref, k_ref, v_ref, lr_ref, g_ref, wgb_ref, bgb_ref, gn_ref, y_ref, st_ref, ob_ref)

    @pl.when(n >= nb)
    def _():
        _gla_pass(True, n - nb, q_ref, k_ref, v_ref, lr_ref, g_ref, wgf_ref, bgf_ref, gn_ref, y_ref, st_ref, ob_ref)


def _gla(p, lr, wgf, wgb, bgf, bgb, gn, bsz, seq):
    ts = min(512, seq)
    nb = seq // ts
    blk = lambda n: jnp.where(n < nb, nb - 1 - n, n - nb)
    oblk = lambda n: jnp.where(n < nb, 0, n - nb)
    col = lambda off: off // GROUP_W
    in_specs = [
        pl.BlockSpec((ts, GROUP_W), lambda b, n: (b * nb + blk(n), col(P_Q))),
        pl.BlockSpec((ts, GROUP_W), lambda b, n: (b * nb + blk(n), col(P_K))),
        pl.BlockSpec((ts, GROUP_W), lambda b, n: (b * nb + blk(n), col(P_V))),
        pl.BlockSpec((ts, LANE), lambda b, n: (b * nb + blk(n), 0)),
        pl.BlockSpec((ts, GROUP_W), lambda b, n: (b * nb + oblk(n), col(P_G))),
        pl.BlockSpec((LANE, GROUP_W), lambda b, n: (0, 0)),
        pl.BlockSpec((LANE, GROUP_W), lambda b, n: (0, 0)),
        pl.BlockSpec((1, GROUP_W), lambda b, n: (0, 0)),
        pl.BlockSpec((1, GROUP_W), lambda b, n: (0, 0)),
        pl.BlockSpec((1, GLA_DV), lambda b, n: (0, 0)),
    ]
    return pl.pallas_call(
        functools.partial(_gla_kernel, nb=nb),
        grid=(bsz, 2 * nb),
        in_specs=in_specs,
        out_specs=pl.BlockSpec((ts, GROUP_W), lambda b, n: (b * nb + oblk(n), 0)),
        out_shape=jax.ShapeDtypeStruct((bsz * seq, GROUP_W), BF16),
        scratch_shapes=[pltpu.VMEM((GLA_HEADS, GLA_DV, LANE), F32), pltpu.VMEM((seq, GROUP_W), F32)],
        compiler_params=_cp("parallel", "arbitrary"),
        name="gla",
    )(p, p, p, lr, p, wgf, wgb, bgf, bgb, gn)


def _residue_products(tab_ref, z_refs):
    a = [jnp.dot(tab_ref[r], z_refs[r][...], preferred_element_type=F32) for r in range(RADIX)]
    b = [jnp.dot(tab_ref[RADIX + r], z_refs[r][...], preferred_element_type=F32) for r in range(RADIX)]
    return a, b


def _butterflies(a, b):
    a02p, a02m, a13p, a13m = a[0] + a[2], a[0] - a[2], a[1] + a[3], a[1] - a[3]
    b02p, b02m, b13p, b13m = b[0] + b[2], b[0] - b[2], b[1] + b[3], b[1] - b[3]
    re = [a02p + a13p, a02m - b13m, a02p - a13p, a02m + b13m]
    nim = [b02p + b13p, b02m + a13m, b02p - b13p, b02m - a13m]
    return re, nim


def _fnet_kernel(tab_ref, x0, x1, x2, x3, cc_ref, sc_ref, gn_ref, y_ref, *, scale):
    a, b = _residue_products(tab_ref, (x0, x1, x2, x3))
    re, nim = _butterflies(a, b)
    for m in range(RADIX):
        y = jnp.dot(re[m].astype(BF16), cc_ref[...], preferred_element_type=F32)
        y = y - jnp.dot(nim[m].astype(BF16), sc_ref[...], preferred_element_type=F32)
        y_ref[m] = _rms(y * scale, gn_ref[...]).astype(y_ref.dtype)


def _fnet(uf, tab, cc, sc, gn, bsz, seq):
    q = seq // RADIX
    tq = min(256, q)
    nq = q // tq
    res = lambda r: pl.BlockSpec((None, None, q, GROUP_W), lambda i, b: (b, r, 0, 0))
    sq = pl.BlockSpec((GROUP_W, GROUP_W), lambda i, b: (0, 0))
    out = pl.pallas_call(
        functools.partial(_fnet_kernel, scale=1.0 / math.sqrt(seq * FNET_CH)),
        grid=(nq, bsz),
        in_specs=[pl.BlockSpec((2 * RADIX, tq, q), lambda i, b: (0, i, 0)),
                  res(0), res(1), res(2), res(3), sq, sq,
                  pl.BlockSpec((1, GROUP_W), lambda i, b: (0, 0))],
        out_specs=pl.BlockSpec((None, RADIX, tq, GROUP_W), lambda i, b: (b, 0, i, 0)),
        out_shape=jax.ShapeDtypeStruct((bsz, RADIX, q, GROUP_W), BF16),
        compiler_params=_cp("parallel", "arbitrary"),
        name="fnet",
    )(tab, uf, uf, uf, uf, cc, sc, gn)
    return out.reshape(bsz * seq, GROUP_W)


def _conv3(main, prev_row, next_row, w):
    ts = main.shape[0]
    rid = lax.broadcasted_iota(jnp.int32, main.shape, 0)
    up = jnp.where(rid == 0, prev_row, pltpu.roll(main, 1, 0))
    dn = jnp.where(rid == ts - 1, next_row, pltpu.roll(main, ts - 1, 0))
    return up * w[0:1] + main * w[1:2] + dn * w[2:3]


def _short_kernel(hm_ref, hp_ref, hn_ref, sm_ref, sp_ref, sn_ref, uf_ref, hw_ref, sw_ref, gn_ref,
                  v_ref, x1_ref, x2_ref, ufr_ref, yd_ref, nat_ref, *, nblk):
    i = pl.program_id(0)
    keep_prev = jnp.where(i % nblk == 0, 0.0, 1.0)
    keep_next = jnp.where(i % nblk == nblk - 1, 0.0, 1.0)
    last = BF16_ROWS - 1
    w = GROUP_W
    ts = hm_ref.shape[0]
    ch = _conv3(hm_ref[...].astype(F32), hp_ref[last:last + 1, :].astype(F32) * keep_prev,
                hn_ref[0:1, :].astype(F32) * keep_next, hw_ref[...])
    nat = jnp.concatenate([ch, uf_ref[...].astype(F32)], axis=1)
    per = w // LANE
    for c in range(4 * per):
        nat_ref[c] = nat[:, c * LANE:(c + 1) * LANE]
    for g, dst in enumerate((v_ref, x1_ref, x2_ref, ufr_ref)):
        for r in range(RADIX):
            for c in range(per):
                piece = nat_ref[g * per + c, pl.ds(r, ts // RADIX, stride=RADIX), :]
                dst[r, :, c * LANE:(c + 1) * LANE] = piece.astype(dst.dtype)
    us = sm_ref[...].astype(F32)
    sp = sp_ref[last:last + 1, :].astype(F32)
    sn = sn_ref[0:1, :].astype(F32)
    prod = us[:, w:2 * w] * us[:, 2 * w:]
    conv = _conv3(prod, sp[:, w:2 * w] * sp[:, 2 * w:] * keep_prev, sn[:, w:2 * w] * sn[:, 2 * w:] * keep_next,
                  sw_ref[...])
    yd_ref[...] = _rms(us[:, :w] * conv, gn_ref[...]).astype(yd_ref.dtype)


def _short(p, hy_w, sc_w, gn, bsz, seq):
    m = bsz * seq
    ts = min(256, seq)
    nblk = seq // ts
    hb = ts // BF16_ROWS
    nh = m // BF16_ROWS
    wide = 3 * GROUP_W
    main = lambda cb: pl.BlockSpec((ts, wide), lambda i: (i, cb))
    prev = lambda cb: pl.BlockSpec((BF16_ROWS, wide), lambda i: (jnp.maximum(i * hb - 1, 0), cb))
    nxt = lambda cb: pl.BlockSpec((BF16_ROWS, wide), lambda i: (jnp.minimum((i + 1) * hb, nh - 1), cb))
    nat = pl.BlockSpec((ts, GROUP_W), lambda i: (i, 0))
    res = pl.BlockSpec((None, RADIX, ts // RADIX, GROUP_W), lambda i: (i // nblk, 0, i % nblk, 0))
    res_shape = jax.ShapeDtypeStruct((bsz, RADIX, seq // RADIX, GROUP_W), BF16)
    return pl.pallas_call(
        functools.partial(_short_kernel, nblk=nblk),
        grid=(m // ts,),
        in_specs=[main(0), prev(0), nxt(0), main(1), prev(1), nxt(1),
                  pl.BlockSpec((ts, GROUP_W), lambda i: (i, P_UF // GROUP_W)),
                  pl.BlockSpec((3, wide), lambda i: (0, 0)),
                  pl.BlockSpec((3, GROUP_W), lambda i: (0, 0)),
                  pl.BlockSpec((1, GROUP_W), lambda i: (0, 0))],
        out_specs=[res, res, res, res, nat],
        out_shape=[res_shape] * 4 + [jax.ShapeDtypeStruct((m, GROUP_W), BF16)],
        scratch_shapes=[pltpu.VMEM((4 * GROUP_W // LANE, ts, LANE), F32)],
        compiler_params=_cp("parallel"),
        name="short_conv",
    )(p, p, p, p, p, p, p, hy_w, sc_w, gn)


def _filter_kernel(feat_ref, w1_ref, b1_ref, w2_ref, b2_ref, w3_ref, fr_ref, dc_ref, a_ref, b_ref):
    hp = lax.Precision.HIGHEST
    feats = feat_ref[...]
    fr = fr_ref[...]
    h = jnp.sin(fr * (jnp.dot(feats, w1_ref[...], precision=hp, preferred_element_type=F32) + b1_ref[...]))
    h = jnp.sin(fr * (jnp.dot(h, w2_ref[...], precision=hp, preferred_element_type=F32) + b2_ref[...]))
    h = jnp.dot(h, w3_ref[...], precision=hp, preferred_element_type=F32)
    h = h * jnp.exp(-feats[:, 0:1] * jnp.abs(dc_ref[...]))
    tl = h.shape[0]
    pos = lax.broadcasted_iota(jnp.int32, (tl, GROUP_W), 0) + pl.program_id(1) * tl
    for o in range(2):
        hf = h[:, (2 * o) * GROUP_W:(2 * o + 1) * GROUP_W]
        hb = jnp.where(pos == 0, 0.0, h[:, (2 * o + 1) * GROUP_W:(2 * o + 2) * GROUP_W])
        a_ref[:, o * GROUP_W:(o + 1) * GROUP_W] = (hf + hb).astype(a_ref.dtype)
        b_ref[:, o * GROUP_W:(o + 1) * GROUP_W] = (hb - hf).astype(b_ref.dtype)


def _filters(feats, w1, b1, w2, b2, w3, fr, dc, seq):
    depth = w1.shape[0]
    tl = min(512, seq)
    lay = lambda r, c: pl.BlockSpec((None, r, c), lambda l, i: (l, 0, 0))
    out = pl.BlockSpec((None, tl, 2 * GROUP_W), lambda l, i: (l, i, 0))
    return pl.pallas_call(
        _filter_kernel,
        grid=(depth, seq // tl),
        in_specs=[pl.BlockSpec((tl, LANE), lambda l, i: (i, 0)),
                  lay(LANE, HY_FFN), lay(1, HY_FFN), lay(HY_FFN, HY_FFN), lay(1, HY_FFN),
                  lay(HY_FFN, 4 * GROUP_W), lay(1, HY_FFN), lay(1, 4 * GROUP_W)],
        out_specs=[out, out],
        out_shape=[jax.ShapeDtypeStruct((depth, seq, 2 * GROUP_W), BF16)] * 2,
        compiler_params=_cp("parallel", "parallel"),
        name="hyena_filter",
    )(feats, w1, b1, w2, b2, w3, fr, dc)


def _hyena_classes(a, b):
    re, nim = _butterflies(a, b)
    return re, [nim[0], nim[1], -nim[2], -nim[3]]


def _spectrum_kernel(tab_ref, z0, z1, z2, z3, o_ref):
    a, b = _residue_products(tab_ref, (z0, z1, z2, z3))
    zr, zs = _hyena_classes(a, b)
    for m in range(RADIX):
        o_ref[m] = zr[m]
        o_ref[RADIX + m] = zs[m]


def _filter_spectrum(tab, filt, seq):
    depth, _, _, cols = filt.shape
    q = seq // RADIX
    tq = min(256, q)
    ncb = cols // GROUP_W
    res = lambda r: pl.BlockSpec((None, None, q, GROUP_W), lambda i, n: (n // ncb, r, 0, n % ncb))
    return pl.pallas_call(
        _spectrum_kernel,
        grid=(q // tq, depth * ncb),
        in_specs=[pl.BlockSpec((2 * RADIX, tq, q), lambda i, n: (0, i, 0)), res(0), res(1), res(2), res(3)],
        out_specs=pl.BlockSpec((None, 2 * RADIX, tq, GROUP_W), lambda i, n: (n // ncb, 0, i, n % ncb)),
        out_shape=jax.ShapeDtypeStruct((depth, 2 * RADIX, q, cols), F32),
        compiler_params=_cp("parallel", "arbitrary"),
        name="filter_spectrum",
    )(tab, filt, filt, filt, filt)


def _conv_spectrum_kernel(tab_ref, z0, z1, z2, z3, hr_ref, hi_ref, p_ref):
    a, b = _residue_products(tab_ref, (z0, z1, z2, z3))
    zr, zs = _hyena_classes(a, b)
    gr, gi = [], []
    for m in range(RADIX):
        hr, hi = hr_ref[m], hi_ref[m]
        gr.append(zr[m] * hr + zs[m] * hi)
        gi.append(zr[m] * hi - zs[m] * hr)
    s02, d02, s13, d13 = gr[0] + gr[2], gr[0] - gr[2], gr[1] + gr[3], gr[1] - gr[3]
    t02, e02, t13, e13 = gi[0] + gi[2], gi[0] - gi[2], gi[1] + gi[3], gi[1] - gi[3]
    pc = [s02 + s13, d02 - t13, s02 - s13, d02 + t13]
    ps = [-e02 - e13, -t02 - d13, e13 - e02, d13 - t02]
    for r in range(RADIX):
        p_ref[r] = pc[r].astype(p_ref.dtype)
        p_ref[RADIX + r] = ps[r].astype(p_ref.dtype)


def _conv_spectrum(tab, z, h_cos, h_sin, layer, order, bsz, seq):
    q = seq // RADIX
    tq = min(256, q)
    res = lambda r: pl.BlockSpec((None, None, q, GROUP_W), lambda i, b: (b, r, 0, 0))
    filt = lambda half: pl.BlockSpec((None, RADIX, tq, GROUP_W), lambda i, b: (layer, half, i, order))
    return pl.pallas_call(
        _conv_spectrum_kernel,
        grid=(q // tq, bsz),
        in_specs=[pl.BlockSpec((2 * RADIX, tq, q), lambda i, b: (0, i, 0)), res(0), res(1), res(2), res(3),
                  filt(0), filt(1)],
        out_specs=pl.BlockSpec((None, 2 * RADIX, tq, GROUP_W), lambda i, b: (b, 0, i, 0)),
        out_shape=jax.ShapeDtypeStruct((bsz, 2 * RADIX, q, GROUP_W), BF16),
        compiler_params=_cp("parallel", "arbitrary"),
        name="hyena_spectrum",
    )(tab, z, z, z, z, h_cos, h_sin)


def _inverse_kernel(tab_ref, p_ref, xm_ref, w_ref, skip_ref, gn_ref, o_ref, *scratch, inv_len, final):
    tj = xm_ref.shape[1]
    for r in range(RADIX):
        y = jnp.dot(tab_ref[r], p_ref[r], preferred_element_type=F32)
        y = y + jnp.dot(tab_ref[RADIX + r], p_ref[RADIX + r], preferred_element_type=F32)
        out = xm_ref[r].astype(F32) * (y * inv_len + w_ref[r].astype(F32) * skip_ref[...])
        if final:
            out = _rms(out, gn_ref[...])
            for c in range(GROUP_W // LANE):
                scratch[0][c, pl.ds(r, tj, stride=RADIX), :] = out[:, c * LANE:(c + 1) * LANE]
        else:
            o_ref[r] = out.astype(o_ref.dtype)
    if final:
        for c in range(GROUP_W // LANE):
            o_ref[:, c * LANE:(c + 1) * LANE] = scratch[0][c].astype(o_ref.dtype)


def _inverse(tab_t, pcs, xm, w, skip, gn, bsz, seq, final):
    q = seq // RADIX
    tj = min(256, q)
    nj = q // tj
    tile = pl.BlockSpec((None, RADIX, tj, GROUP_W), lambda i, b: (b, 0, i, 0))
    vec = pl.BlockSpec((1, GROUP_W), lambda i, b: (0, 0))
    if final:
        out_spec = pl.BlockSpec((RADIX * tj, GROUP_W), lambda i, b: (b * nj + i, 0))
        out_shape = jax.ShapeDtypeStruct((bsz * seq, GROUP_W), BF16)
        scratch = [pltpu.VMEM((GROUP_W // LANE, RADIX * tj, LANE), F32)]
    else:
        out_spec, out_shape, scratch = tile, jax.ShapeDtypeStruct((bsz, RADIX, q, GROUP_W), BF16), []
    return pl.pallas_call(
        functools.partial(_inverse_kernel, inv_len=1.0 / seq, final=final),
        grid=(nj, bsz),
        in_specs=[pl.BlockSpec((2 * RADIX, tj, q), lambda i, b: (0, i, 0)),
                  pl.BlockSpec((None, 2 * RADIX, q, GROUP_W), lambda i, b: (b, 0, 0, 0)),
                  tile, tile, vec, vec],
        out_specs=out_spec,
        out_shape=out_shape,
        scratch_shapes=scratch,
        compiler_params=_cp("parallel", "arbitrary"),
        name="hyena_inverse",
    )(tab_t, pcs, xm, w, skip, gn)


def _out_kernel(ya_ref, yb_ref, yc_ref, yd_ref, w_ref, x_ref, o_ref):
    acc = x_ref[...]
    for g, y_ref in enumerate((ya_ref, yb_ref, yc_ref, yd_ref)):
        acc = acc + jnp.dot(y_ref[...], w_ref[g * GROUP_W:(g + 1) * GROUP_W, :], preferred_element_type=F32)
    o_ref[...] = acc


def _out_proj(ys, w, x):
    m, n = x.shape
    tm = min(512, m)
    piece = pl.BlockSpec((tm, GROUP_W), lambda i: (i, 0))
    xs = pl.BlockSpec((tm, n), lambda i: (i, 0))
    return pl.pallas_call(
        _out_kernel,
        grid=(m // tm,),
        in_specs=[piece, piece, piece, piece, _resident(w.shape, lambda i: (0, 0)), xs],
        out_specs=xs,
        out_shape=jax.ShapeDtypeStruct((m, n), F32),
        compiler_params=_cp("parallel"),
        name="out_proj",
    )(*ys, w, x)


def _attn_kernel(q_ref, k_ref, v_ref, wo_ref, x_ref, o_ref, att_ref):
    for h in range(XA_HEADS):
        hs = slice(h * XA_HD, (h + 1) * XA_HD)
        s = lax.dot_general(q_ref[:, hs], k_ref[:, hs], NT, preferred_element_type=F32) * (XA_HD ** -0.5)
        e = jnp.exp(s - jnp.max(s, axis=-1, keepdims=True))
        p = e / jnp.sum(e, axis=-1, keepdims=True)
        att_ref[:, hs] = jnp.dot(p.astype(BF16), v_ref[:, hs], preferred_element_type=F32).astype(BF16)
    o_ref[...] = x_ref[...] + jnp.dot(att_ref[...], wo_ref[...], preferred_element_type=F32)


def _attn(q, kv, wo, x, seq):
    m, d = x.shape
    tm = min(512, seq)
    per = seq // tm
    xs = pl.BlockSpec((tm, d), lambda i: (i, 0))
    return pl.pallas_call(
        _attn_kernel,
        grid=(m // tm,),
        in_specs=[xs,
                  pl.BlockSpec((N_MEM, d), lambda i: (i // per, 0)),
                  pl.BlockSpec((N_MEM, d), lambda i: (i // per, 1)),
                  _resident((d, d), lambda i: (0, 0)),
                  xs],
        out_specs=xs,
        out_shape=jax.ShapeDtypeStruct((m, d), F32),
        scratch_shapes=[pltpu.VMEM((tm, d), BF16)],
        compiler_params=_cp("parallel"),
        name="cross_attn",
    )(q, kv, kv, wo, x)


def _swiglu_kernel(x_ref, g_ref, wg_ref, wu_ref, wd_ref, fg_ref, o_ref, hn_ref, *, final):
    f = pl.program_id(1)

    @pl.when(f == 0)
    def _():
        _norm_rows_to(x_ref, g_ref, hn_ref)
        o_ref[...] = x_ref[...]

    hn = hn_ref[...]
    gate = jnp.dot(hn, wg_ref[...], preferred_element_type=F32)
    up = jnp.dot(hn, wu_ref[...], preferred_element_type=F32)
    act = (gate * jax.nn.sigmoid(gate) * up).astype(BF16)
    o_ref[...] += jnp.dot(act, wd_ref[...], preferred_element_type=F32)

    if final:
        @pl.when(f == pl.num_programs(1) - 1)
        def _():
            _norm_rows_to(o_ref, fg_ref, o_ref)


def _swiglu(x, g, w_gu, w_d, final_g, final):
    m, d = x.shape
    dff = w_d.shape[0]
    tm, tf = min(512, m), 512
    nf = dff // tf
    xs = pl.BlockSpec((tm, d), lambda i, f: (i, 0))
    vec = pl.BlockSpec((1, d), lambda i, f: (0, 0))
    return pl.pallas_call(
        functools.partial(_swiglu_kernel, final=final),
        grid=(m // tm, nf),
        in_specs=[xs, vec,
                  pl.BlockSpec((d, tf), lambda i, f: (0, f)),
                  pl.BlockSpec((d, tf), lambda i, f: (0, nf + f)),
                  pl.BlockSpec((tf, d), lambda i, f: (f, 0)),
                  vec],
        out_specs=xs,
        out_shape=jax.ShapeDtypeStruct((m, d), F32),
        scratch_shapes=[pltpu.VMEM((tm, d), BF16)],
        compiler_params=_cp("parallel", "arbitrary"),
        name="swiglu",
    )(x, g.reshape(1, d), w_gu, w_gu, w_d, final_g.reshape(1, d))


def _angles(rows, cols, period):
    m = (rows[..., :, None] * cols[..., None, :]) % period
    return m.astype(F32) * (2.0 * math.pi / period)


def _cos_sin(ang, sin_sign=1.0):
    return jnp.concatenate([jnp.cos(ang), sin_sign * jnp.sin(ang)], axis=0).astype(BF16)


def _tables(seq):
    q = seq // RADIX
    base = jnp.arange(q, dtype=jnp.int32)
    times = RADIX * base[None, :] + jnp.arange(RADIX, dtype=jnp.int32)[:, None]
    rep = jnp.broadcast_to(base, (RADIX, q))
    fnet_tab = _cos_sin(_angles(rep, times, seq))
    ch = jnp.arange(GROUP_W, dtype=jnp.int32)
    same = (ch[:, None] // FNET_CH) == (ch[None, :] // FNET_CH)
    angc = _angles(ch % FNET_CH, ch % FNET_CH, FNET_CH)
    cc = jnp.where(same, jnp.cos(angc), 0.0).astype(BF16)
    sc = jnp.where(same, jnp.sin(angc), 0.0).astype(BF16)
    hy_tab = _cos_sin(_angles(2 * rep + 1, times, 4 * seq))
    hy_tab_t = _cos_sin(_angles(times, 2 * rep + 1, 4 * seq))
    return fnet_tab, cc, sc, hy_tab, hy_tab_t


def _position_features(seq):
    pos = jnp.arange(seq, dtype=F32)
    t = pos / seq
    f = jnp.linspace(1e-4, HY_BANDS - 1, HY_BANDS, dtype=F32)
    ang = (2.0 * math.pi * t)[:, None] * f[None, :]
    feats = jnp.concatenate([t[:, None], jnp.cos(ang), -jnp.sin(ang)], axis=-1)
    return jnp.pad(feats, ((0, 0), (0, LANE - HY_EMB)))


def _pad_heads(t):
    t = t.reshape(t.shape[:-1] + (GLA_HEADS, GLA_DK))
    t = jnp.pad(t, [(0, 0)] * (t.ndim - 1) + [(0, LANE - GLA_DK)])
    return t.reshape(t.shape[:-2] + (GLA_HEADS * LANE,))


def kernel(x, mem, norm_g, w_in, gla_gk_w, gla_gk_b, gla_norm_g, hy_conv_w, hy_ffn_w1, hy_ffn_b1, hy_ffn_w2, hy_ffn_b2, hy_ffn_w3, hy_sin_freq, hy_decay, hy_skip, sc_conv_w, grp_norm_g, w_out, mem_norm_g, w_xq, w_xkv, w_xo, w_gate_up, w_down, final_norm_g):
    bsz, seq, d = x.shape
    depth = norm_g.shape[0]
    m = bsz * seq
    assert d == D_MODEL and seq % (RADIX * GLA_CHUNK) == 0 and mem.shape[1] == N_MEM

    q_w, k_w, v_w, g_w, lr_w, uf_w, uh_w, us_w = jnp.split(
        w_in, [256, 512, 1024, 1536, 1568, 2080, 3616], axis=-1)
    w_p = jnp.concatenate([uh_w, us_w, _pad_heads(q_w), _pad_heads(k_w), v_w, g_w, uf_w], axis=-1).astype(BF16)
    w_lr = jnp.pad(lr_w, ((0, 0), (0, 0), (0, LANE - 2 * GLA_RANK))).astype(BF16)
    gkw = _pad_heads(gla_gk_w)
    wgf = jnp.pad(gkw[:, 0], ((0, 0), (0, LANE - GLA_RANK), (0, 0))).astype(BF16)
    wgb = jnp.pad(gkw[:, 1], ((0, 0), (GLA_RANK, LANE - 2 * GLA_RANK), (0, 0))).astype(BF16)
    gkb = _pad_heads(gla_gk_b).astype(F32)
    w_out_b, w_xq_b, w_xkv_b, w_xo_b = (t.astype(BF16) for t in (w_out, w_xq, w_xkv, w_xo))
    w_gu_b, w_d_b = w_gate_up.astype(BF16), w_down.astype(BF16)
    fnet_tab, cc, sc, hy_tab, hy_tab_t = _tables(seq)
    qlen = seq // RADIX
    feats = _position_features(seq).reshape(qlen, RADIX, LANE).transpose(1, 0, 2).reshape(seq, LANE)

    w1 = jnp.pad(hy_ffn_w1.astype(F32), ((0, 0), (0, LANE - HY_EMB), (0, 0)))
    fa, fb = _filters(feats, w1, hy_ffn_b1.reshape(depth, 1, HY_FFN).astype(F32), hy_ffn_w2.astype(F32),
                      hy_ffn_b2.reshape(depth, 1, HY_FFN).astype(F32), hy_ffn_w3.astype(F32),
                      hy_sin_freq.reshape(depth, 1, HY_FFN).astype(F32),
                      hy_decay.reshape(depth, 1, 4 * GROUP_W).astype(F32), seq)
    res_major = lambda t: t.reshape(depth, RADIX, qlen, t.shape[-1])
    h_cos = _filter_spectrum(hy_tab, res_major(fa), seq)
    h_sin = _filter_spectrum(hy_tab, res_major(fb), seq)

    mem2 = mem.reshape(bsz * N_MEM, d)
    xf = x.reshape(m, d)
    for l in range(depth):
        p, lr = _norm_matmul(xf, norm_g[l, 0], w_p[l], 1024, 1408, w_side=w_lr[l])
        y_a = _gla(p, lr, wgf[l], wgb[l], gkb[l, 0:1], gkb[l, 1:2], gla_norm_g[l].reshape(1, GLA_DV).astype(F32),
                   bsz, seq)
        hv, hx1, hx2, uf, y_d = _short(p, hy_conv_w[l].astype(F32), sc_conv_w[l].astype(F32),
                                       grp_norm_g[l, 2:3].astype(F32), bsz, seq)
        y_b = _fnet(uf, fnet_tab, cc, sc, grp_norm_g[l, 0:1].astype(F32), bsz, seq)
        gn_c = grp_norm_g[l, 1:2].astype(F32)
        pcs = _conv_spectrum(hy_tab, hv, h_cos, h_sin, l, 0, bsz, seq)
        z = _inverse(hy_tab_t, pcs, hx1, hv, hy_skip[l, 0:1].astype(F32), gn_c, bsz, seq, final=False)
        pcs = _conv_spectrum(hy_tab, z, h_cos, h_sin, l, 1, bsz, seq)
        y_c = _inverse(hy_tab_t, pcs, hx2, z, hy_skip[l, 1:2].astype(F32), gn_c, bsz, seq, final=True)
        xf = _out_proj((y_a, y_b, y_c, y_d), w_out_b[l], xf)
        q = _norm_matmul(xf, norm_g[l, 1], w_xq_b[l], 1024, D_MODEL)
        kv = _norm_matmul(mem2, mem_norm_g, w_xkv_b[l], 1024, 1024)
        xf = _attn(q, kv, w_xo_b[l], xf, seq)
        xf = _swiglu(xf, norm_g[l, 2], w_gu_b[l], w_d_b[l], final_norm_g, final=(l == depth - 1))
    return xf.reshape(bsz, seq, d)
```

```python
import functools
import math

import jax
import jax.numpy as jnp
from jax import lax
from jax.experimental import pallas as pl
from jax.experimental.pallas import tpu as pltpu

F32 = jnp.float32
BF16 = jnp.bfloat16
EPS = 1e-6

D_MODEL = 2048
GROUP_W = 512
GLA_HEADS = 4
GLA_DK = 64
GLA_DV = 128
GLA_RANK = 16
GLA_GATE_NORM = 16.0
GLA_CHUNK = 64
FNET_CH = 128
HY_BANDS = 16
HY_EMB = 2 * HY_BANDS + 1
HY_FFN = 64
XA_HEADS = 4
XA_HD = D_MODEL // XA_HEADS
N_MEM = 256
D_FF = 5632
RADIX = 4
LANE = 128
BF16_ROWS = 16
VMEM_LIMIT = 56 * 1024 * 1024

GLA_QK = GLA_HEADS * GLA_DK
P_UH, P_US, P_Q, P_K, P_V, P_G, P_UF = 0, 1536, 3072, 3328, 3584, 4096, 4608
P_W = 5120

NT = (((1,), (1,)), ((), ()))
TN = (((0,), (0,)), ((), ()))


def _cp(*sem):
    return pltpu.CompilerParams(dimension_semantics=sem, vmem_limit_bytes=VMEM_LIMIT)


def _resident(shape, index_map):
    return pl.BlockSpec(shape, index_map, pipeline_mode=pl.Buffered(1))


def _rms(x, g):
    ms = jnp.mean(x * x, axis=-1, keepdims=True)
    return x * lax.rsqrt(ms + EPS) * g


def _norm_rows_to(x_ref, g_ref, hn_ref):
    rows = min(256, x_ref.shape[0])
    n = x_ref.shape[0] // rows

    def body(i, c):
        r = pl.ds(pl.multiple_of(i * rows, rows), rows)
        hn_ref[r, :] = _rms(x_ref[r, :], g_ref[...]).astype(hn_ref.dtype)
        return c

    lax.fori_loop(0, n, body, 0)


def _norm_matmul_kernel(x_ref, g_ref, w_ref, *rest, with_side, row_tiles, col_tiles):
    if with_side:
        w2_ref, o_ref, o2_ref, hn_ref = rest
    else:
        o_ref, hn_ref = rest
    i, j = pl.program_id(0), pl.program_id(1)
    part = x_ref.shape[0] // col_tiles

    def norm_part():
        g = g_ref[...]
        for r in range(0, part, BF16_ROWS):
            rows = pl.ds(pl.multiple_of(j * part + r, BF16_ROWS), BF16_ROWS)
            hn_ref[i % 2, rows, :] = _rms(x_ref[rows, :], g).astype(BF16)

    def matmul():
        o_ref[...] = jnp.dot(hn_ref[(i + 1) % 2], w_ref[...], preferred_element_type=F32).astype(o_ref.dtype)

    @pl.when(i == 0)
    def _():
        norm_part()

    @pl.when((i > 0) & (i < row_tiles))
    def _():
        matmul()
        norm_part()

    @pl.when(i == row_tiles)
    def _():
        matmul()

    if with_side:
        @pl.when((i > 0) & (j == 0))
        def _():
            o2_ref[...] = jnp.dot(hn_ref[(i + 1) % 2], w2_ref[...], preferred_element_type=F32)


def _norm_matmul(x, g, w, tm, tn, w_side=None):
    m, k = x.shape
    n = w.shape[1]
    tm, tn = min(tm, m), min(tn, n)
    ni, nj = m // tm, n // tn
    assert (tm // nj) % BF16_ROWS == 0
    prev = lambda i: jnp.maximum(i - 1, 0)
    in_specs = [
        pl.BlockSpec((tm, k), lambda i, j: (jnp.minimum(i, ni - 1), 0)),
        pl.BlockSpec((1, k), lambda i, j: (0, 0)),
        _resident((k, tn), lambda i, j: (0, j)) if tn == n else pl.BlockSpec((k, tn), lambda i, j: (0, j)),
    ]
    out_shape = [jax.ShapeDtypeStruct((m, n), BF16)]
    out_specs = [pl.BlockSpec((tm, tn), lambda i, j: (prev(i), jnp.where(i == 0, 0, j)))]
    args = [x, g.reshape(1, k), w]
    if w_side is not None:
        n2 = w_side.shape[1]
        in_specs.append(pl.BlockSpec((k, n2), lambda i, j: (0, 0)))
        out_shape.append(jax.ShapeDtypeStruct((m, n2), F32))
        out_specs.append(pl.BlockSpec((tm, n2), lambda i, j: (prev(i), 0)))
        args.append(w_side)
    res = pl.pallas_call(
        functools.partial(_norm_matmul_kernel, with_side=w_side is not None, row_tiles=ni, col_tiles=nj),
        grid=(ni + 1, nj),
        in_specs=in_specs,
        out_specs=out_specs,
        out_shape=out_shape,
        scratch_shapes=[pltpu.VMEM((2, tm, k), BF16)],
        compiler_params=_cp("arbitrary", "arbitrary"),
        name="norm_matmul",
    )(*args)
    return res if w_side is not None else res[0]


def _gla_pass(fwd, blk, q_ref, k_ref, v_ref, lr_ref, g_ref, wg_ref, bg_ref, gn_ref, y_ref, st_ref, ob_ref):
    ns, ts = q_ref.shape[0], q_ref.shape[1]
    c_sz = GLA_CHUNK
    nc = ts // c_sz
    ii = lax.broadcasted_iota(jnp.int32, (c_sz, c_sz), 0)
    jj = lax.broadcasted_iota(jnp.int32, (c_sz, c_sz), 1)
    incl = (jj <= ii) if fwd else (jj >= ii)
    mask = incl if fwd else (jj > ii)
    tri = jnp.where(incl, 1.0, 0.0).astype(BF16)
    gn = gn_ref[...]
    chunk = lambda c: slice(c * c_sz, (c + 1) * c_sz)
    order = list(range(nc)) if fwd else list(reversed(range(nc)))
    problems = [(s, h) for s in range(ns) for h in range(GLA_HEADS)]
    pair = lambda h: slice((h // 2) * LANE, (h // 2 + 1) * LANE)
    val = lambda h: slice(h * GLA_DV, (h + 1) * GLA_DV)
    first = lax.broadcasted_iota(jnp.int32, (c_sz, LANE), 1) < GLA_DK
    own = lambda h: first if h % 2 == 0 else jnp.logical_not(first)

    qe, ke, qi, ku, dec = {}, {}, {}, {}, {}
    for s in range(ns):
        x = jnp.dot(lr_ref[s].astype(BF16), wg_ref[...], preferred_element_type=F32) + bg_ref[...]
        gk = (jnp.minimum(x, 0.0) - jnp.log1p(jnp.exp(-jnp.abs(x)))) * (1.0 / GLA_GATE_NORM)
        hi = gk.astype(BF16)
        lo = (gk - hi.astype(F32)).astype(BF16)
        for c in range(nc):
            r = chunk(c)
            b = jnp.dot(tri, hi[r], preferred_element_type=F32) + jnp.dot(tri, lo[r], preferred_element_type=F32)
            bref = b[c_sz // 2:c_sz // 2 + 1] if fwd else b[c_sz // 2 - 1:c_sz // 2]
            btot = b[c_sz - 1:c_sz] if fwd else b[0:1]
            qe_f = q_ref[s, r, :].astype(F32) * (GLA_DK ** -0.5) * jnp.exp(b - bref)
            ke_f = k_ref[s, r, :].astype(F32) * jnp.exp(bref - b)
            qe[s, c] = qe_f
            ke[s, c] = ke_f.astype(BF16)
            qi[s, c] = (qe_f * jnp.exp(bref)).astype(BF16)
            ku[s, c] = ke_f * jnp.exp(btot - bref)
            dec[s, c] = jnp.exp(btot)

    kv = {}
    for c in order:
        for s, h in problems:
            kv[s, h, c] = lax.dot_general(v_ref[s, chunk(c), val(h)],
                                          jnp.where(own(h), ku[s, c][:, pair(h)], 0.0).astype(BF16), TN,
                                          preferred_element_type=F32)
    st = {(s, h): st_ref[s, h] for s, h in problems}
    st_in = {}
    for c in order:
        for s, h in problems:
            st_in[s, h, c] = st[s, h].astype(BF16)
            st[s, h] = st[s, h] * dec[s, c][:, pair(h)] + kv[s, h, c]
    for s, h in problems:
        st_ref[s, h] = st[s, h]
    for c in order:
        r = chunk(c)
        rows = pl.ds(pl.multiple_of(blk * ts + c * c_sz, c_sz), c_sz)
        scores = {}
        for s, h in problems:
            sc = lax.dot_general(jnp.where(own(h), qe[s, c][:, pair(h)], 0.0).astype(BF16), ke[s, c][:, pair(h)], NT,
                                 preferred_element_type=F32)
            scores[s, h] = jnp.where(mask, sc, 0.0).astype(BF16)
        for s, h in problems:
            o = jnp.dot(scores[s, h], v_ref[s, r, val(h)], preferred_element_type=F32)
            o = o + lax.dot_general(qi[s, c][:, pair(h)], st_in[s, h, c], NT, preferred_element_type=F32)
            if fwd:
                tot = o + ob_ref[s, rows, val(h)]
                gate = g_ref[s, r, val(h)].astype(F32)
                y_ref[s, r, val(h)] = (_rms(tot, gn) * (gate * jax.nn.sigmoid(gate))).astype(y_ref.dtype)
            else:
                ob_ref[s, rows, val(h)] = o


def _gla_kernel(q_ref, k_ref, v_ref, lr_ref, g_ref, wgf_ref, wgb_ref, bgf_ref, bgb_ref, gn_ref,
                y_ref, st_ref, ob_ref, *, nb):
    n = pl.program_id(1)

    @pl.when((n == 0) | (n == nb))
    def _():
        st_ref[...] = jnp.zeros_like(st_ref)

    @pl.when(n < nb)
    def _():
        _gla_pass(False, nb - 1 - n, q_ref, k_ref, v_ref, lr_ref, g_ref, wgb_ref, bgb_ref, gn_ref, y_ref, st_ref, ob_ref)

    @pl.when(n >= nb)
    def _():
        _gla_pass(True, n - nb, q_ref, k_ref, v_ref, lr_ref, g_ref, wgf_ref, bgf_ref, gn_ref, y_ref, st_ref, ob_ref)


def _gla(p, lr, wgf, wgb, bgf, bgb, gn, bsz, seq):
    ts = min(512, seq)
    nb = seq // ts
    ns = 2 if bsz % 2 == 0 else 1
    blk = lambda n: jnp.where(n < nb, nb - 1 - n, n - nb)
    oblk = lambda n: jnp.where(n < nb, 0, n - nb)
    in_specs = [
        pl.BlockSpec((ns, ts, GLA_QK), lambda b, n: (b, blk(n), P_Q // GLA_QK)),
        pl.BlockSpec((ns, ts, GLA_QK), lambda b, n: (b, blk(n), P_K // GLA_QK)),
        pl.BlockSpec((ns, ts, GROUP_W), lambda b, n: (b, blk(n), P_V // GROUP_W)),
        pl.BlockSpec((ns, ts, LANE), lambda b, n: (b, blk(n), 0)),
        pl.BlockSpec((ns, ts, GROUP_W), lambda b, n: (b, oblk(n), P_G // GROUP_W)),
        pl.BlockSpec((LANE, GLA_QK), lambda b, n: (0, 0)),
        pl.BlockSpec((LANE, GLA_QK), lambda b, n: (0, 0)),
        pl.BlockSpec((1, GLA_QK), lambda b, n: (0, 0)),
        pl.BlockSpec((1, GLA_QK), lambda b, n: (0, 0)),
        pl.BlockSpec((1, GLA_DV), lambda b, n: (0, 0)),
    ]
    p3 = p.reshape(bsz, seq, P_W)
    y = pl.pallas_call(
        functools.partial(_gla_kernel, nb=nb),
        grid=(bsz // ns, 2 * nb),
        in_specs=in_specs,
        out_specs=pl.BlockSpec((ns, ts, GROUP_W), lambda b, n: (b, oblk(n), 0)),
        out_shape=jax.ShapeDtypeStruct((bsz, seq, GROUP_W), BF16),
        scratch_shapes=[pltpu.VMEM((ns, GLA_HEADS, GLA_DV, LANE), F32), pltpu.VMEM((ns, seq, GROUP_W), F32)],
        compiler_params=_cp("parallel", "arbitrary"),
        name="gla",
    )(p3, p3, p3, lr.reshape(bsz, seq, LANE), p3, wgf, wgb, bgf, bgb, gn)
    return y.reshape(bsz * seq, GROUP_W)


def _residue_products(tab_ref, z_refs):
    a = [jnp.dot(tab_ref[r], z_refs[r][...], preferred_element_type=F32) for r in range(RADIX)]
    b = [jnp.dot(tab_ref[RADIX + r], z_refs[r][...], preferred_element_type=F32) for r in range(RADIX)]
    return a, b


def _butterflies(a, b):
    a02p, a02m, a13p, a13m = a[0] + a[2], a[0] - a[2], a[1] + a[3], a[1] - a[3]
    b02p, b02m, b13p, b13m = b[0] + b[2], b[0] - b[2], b[1] + b[3], b[1] - b[3]
    re = [a02p + a13p, a02m - b13m, a02p - a13p, a02m + b13m]
    nim = [b02p + b13p, b02m + a13m, b02p - b13p, b02m - a13m]
    return re, nim


def _fnet_kernel(tab_ref, x0, x1, x2, x3, cc_ref, sc_ref, gn_ref, y_ref, *, scale):
    a, b = _residue_products(tab_ref, (x0, x1, x2, x3))
    re, nim = _butterflies(a, b)
    for m in range(RADIX):
        y = jnp.dot(re[m].astype(BF16), cc_ref[...], preferred_element_type=F32)
        y = y - jnp.dot(nim[m].astype(BF16), sc_ref[...], preferred_element_type=F32)
        y_ref[m] = _rms(y * scale, gn_ref[...]).astype(y_ref.dtype)


def _fnet(uf, tab, cc, sc, gn, bsz, seq):
    q = seq // RADIX
    tq = min(256, q)
    nq = q // tq
    res = lambda r: pl.BlockSpec((None, None, q, GROUP_W), lambda i, b: (b, r, 0, 0))
    sq = pl.BlockSpec((GROUP_W, GROUP_W), lambda i, b: (0, 0))
    out = pl.pallas_call(
        functools.partial(_fnet_kernel, scale=1.0 / math.sqrt(seq * FNET_CH)),
        grid=(nq, bsz),
        in_specs=[pl.BlockSpec((2 * RADIX, tq, q), lambda i, b: (0, i, 0)),
                  res(0), res(1), res(2), res(3), sq, sq,
                  pl.BlockSpec((1, GROUP_W), lambda i, b: (0, 0))],
        out_specs=pl.BlockSpec((None, RADIX, tq, GROUP_W), lambda i, b: (b, 0, i, 0)),
        out_shape=jax.ShapeDtypeStruct((bsz, RADIX, q, GROUP_W), BF16),
        compiler_params=_cp("parallel", "arbitrary"),
        name="fnet",
    )(tab, uf, uf, uf, uf, cc, sc, gn)
    return out.reshape(bsz * seq, GROUP_W)


def _conv3(main, prev_row, next_row, w):
    ts = main.shape[0]
    rid = lax.broadcasted_iota(jnp.int32, main.shape, 0)
    up = jnp.where(rid == 0, prev_row, pltpu.roll(main, 1, 0))
    dn = jnp.where(rid == ts - 1, next_row, pltpu.roll(main, ts - 1, 0))
    return up * w[0:1] + main * w[1:2] + dn * w[2:3]


def _short_kernel(hm_ref, hp_ref, hn_ref, sm_ref, sp_ref, sn_ref, uf_ref, hw_ref, sw_ref, gn_ref,
                  v_ref, x1_ref, x2_ref, ufr_ref, yd_ref, nat_ref, *, nblk):
    i = pl.program_id(0)
    keep_prev = jnp.where(i % nblk == 0, 0.0, 1.0)
    keep_next = jnp.where(i % nblk == nblk - 1, 0.0, 1.0)
    last = BF16_ROWS - 1
    w = GROUP_W
    ts = hm_ref.shape[0]
    ch = _conv3(hm_ref[...].astype(F32), hp_ref[last:last + 1, :].astype(F32) * keep_prev,
                hn_ref[0:1, :].astype(F32) * keep_next, hw_ref[...])
    nat = jnp.concatenate([ch, uf_ref[...].astype(F32)], axis=1)
    per = w // LANE
    for c in range(4 * per):
        nat_ref[c] = nat[:, c * LANE:(c + 1) * LANE]
    for g, dst in enumerate((v_ref, x1_ref, x2_ref, ufr_ref)):
        for r in range(RADIX):
            for c in range(per):
                piece = nat_ref[g * per + c, pl.ds(r, ts // RADIX, stride=RADIX), :]
                dst[r, :, c * LANE:(c + 1) * LANE] = piece.astype(dst.dtype)
    us = sm_ref[...].astype(F32)
    sp = sp_ref[last:last + 1, :].astype(F32)
    sn = sn_ref[0:1, :].astype(F32)
    prod = us[:, w:2 * w] * us[:, 2 * w:]
    conv = _conv3(prod, sp[:, w:2 * w] * sp[:, 2 * w:] * keep_prev, sn[:, w:2 * w] * sn[:, 2 * w:] * keep_next,
                  sw_ref[...])
    yd_ref[...] = _rms(us[:, :w] * conv, gn_ref[...]).astype(yd_ref.dtype)


def _short(p, hy_w, sc_w, gn, bsz, seq):
    m = bsz * seq
    ts = min(256, seq)
    nblk = seq // ts
    hb = ts // BF16_ROWS
    nh = m // BF16_ROWS
    wide = 3 * GROUP_W
    main = lambda cb: pl.BlockSpec((ts, wide), lambda i: (i, cb))
    prev = lambda cb: pl.BlockSpec((BF16_ROWS, wide), lambda i: (jnp.maximum(i * hb - 1, 0), cb))
    nxt = lambda cb: pl.BlockSpec((BF16_ROWS, wide), lambda i: (jnp.minimum((i + 1) * hb, nh - 1), cb))
    nat = pl.BlockSpec((ts, GROUP_W), lambda i: (i, 0))
    res = pl.BlockSpec((None, RADIX, ts // RADIX, GROUP_W), lambda i: (i // nblk, 0, i % nblk, 0))
    res_shape = jax.ShapeDtypeStruct((bsz, RADIX, seq // RADIX, GROUP_W), BF16)
    return pl.pallas_call(
        functools.partial(_short_kernel, nblk=nblk),
        grid=(m // ts,),
        in_specs=[main(0), prev(0), nxt(0), main(1), prev(1), nxt(1),
                  pl.BlockSpec((ts, GROUP_W), lambda i: (i, P_UF // GROUP_W)),
                  pl.BlockSpec((3, wide), lambda i: (0, 0)),
                  pl.BlockSpec((3, GROUP_W), lambda i: (0, 0)),
                  pl.BlockSpec((1, GROUP_W), lambda i: (0, 0))],
        out_specs=[res, res, res, res, nat],
        out_shape=[res_shape] * 4 + [jax.ShapeDtypeStruct((m, GROUP_W), BF16)],
        scratch_shapes=[pltpu.VMEM((4 * GROUP_W // LANE, ts, LANE), F32)],
        compiler_params=_cp("parallel"),
        name="short_conv",
    )(p, p, p, p, p, p, p, hy_w, sc_w, gn)


def _filter_kernel(feat_ref, w1_ref, b1_ref, w2_ref, b2_ref, w3_ref, fr_ref, dc_ref, a_ref, b_ref):
    hp = lax.Precision.HIGHEST
    feats = feat_ref[...]
    fr = fr_ref[...]
    h = jnp.sin(fr * (jnp.dot(feats, w1_ref[...], precision=hp, preferred_element_type=F32) + b1_ref[...]))
    h = jnp.sin(fr * (jnp.dot(h, w2_ref[...], precision=hp, preferred_element_type=F32) + b2_ref[...]))
    h = jnp.dot(h, w3_ref[...], precision=hp, preferred_element_type=F32)
    h = h * jnp.exp(-feats[:, 0:1] * jnp.abs(dc_ref[...]))
    tl = h.shape[0]
    pos = lax.broadcasted_iota(jnp.int32, (tl, GROUP_W), 0) + pl.program_id(1) * tl
    for o in range(2):
        hf = h[:, (2 * o) * GROUP_W:(2 * o + 1) * GROUP_W]
        hb = jnp.where(pos == 0, 0.0, h[:, (2 * o + 1) * GROUP_W:(2 * o + 2) * GROUP_W])
        a_ref[:, o * GROUP_W:(o + 1) * GROUP_W] = (hf + hb).astype(a_ref.dtype)
        b_ref[:, o * GROUP_W:(o + 1) * GROUP_W] = (hb - hf).astype(b_ref.dtype)


def _filters(feats, w1, b1, w2, b2, w3, fr, dc, seq):
    depth = w1.shape[0]
    tl = min(512, seq)
    lay = lambda r, c: pl.BlockSpec((None, r, c), lambda l, i: (l, 0, 0))
    out = pl.BlockSpec((None, tl, 2 * GROUP_W), lambda l, i: (l, i, 0))
    return pl.pallas_call(
        _filter_kernel,
        grid=(depth, seq // tl),
        in_specs=[pl.BlockSpec((tl, LANE), lambda l, i: (i, 0)),
                  lay(LANE, HY_FFN), lay(1, HY_FFN), lay(HY_FFN, HY_FFN), lay(1, HY_FFN),
                  lay(HY_FFN, 4 * GROUP_W), lay(1, HY_FFN), lay(1, 4 * GROUP_W)],
        out_specs=[out, out],
        out_shape=[jax.ShapeDtypeStruct((depth, seq, 2 * GROUP_W), BF16)] * 2,
        compiler_params=_cp("parallel", "parallel"),
        name="hyena_filter",
    )(feats, w1, b1, w2, b2, w3, fr, dc)


def _hyena_classes(a, b):
    re, nim = _butterflies(a, b)
    return re, [nim[0], nim[1], -nim[2], -nim[3]]


def _spectrum_kernel(tab_ref, z0, z1, z2, z3, o_ref):
    a, b = _residue_products(tab_ref, (z0, z1, z2, z3))
    zr, zs = _hyena_classes(a, b)
    for m in range(RADIX):
        o_ref[m] = zr[m]
        o_ref[RADIX + m] = zs[m]


def _filter_spectrum(tab, filt, seq):
    depth, _, _, cols = filt.shape
    q = seq // RADIX
    tq = min(256, q)
    ncb = cols // GROUP_W
    res = lambda r: pl.BlockSpec((None, None, q, GROUP_W), lambda i, n: (n // ncb, r, 0, n % ncb))
    return pl.pallas_call(
        _spectrum_kernel,
        grid=(q // tq, depth * ncb),
        in_specs=[pl.BlockSpec((2 * RADIX, tq, q), lambda i, n: (0, i, 0)), res(0), res(1), res(2), res(3)],
        out_specs=pl.BlockSpec((None, 2 * RADIX, tq, GROUP_W), lambda i, n: (n // ncb, 0, i, n % ncb)),
        out_shape=jax.ShapeDtypeStruct((depth, 2 * RADIX, q, cols), F32),
        compiler_params=_cp("parallel", "arbitrary"),
        name="filter_spectrum",
    )(tab, filt, filt, filt, filt)


def _conv_spectrum_kernel(tab_ref, z0, z1, z2, z3, hr_ref, hi_ref, p_ref):
    a, b = _residue_products(tab_ref, (z0, z1, z2, z3))
    zr, zs = _hyena_classes(a, b)
    gr, gi = [], []
    for m in range(RADIX):
        hr, hi = hr_ref[m], hi_ref[m]
        gr.append(zr[m] * hr + zs[m] * hi)
        gi.append(zr[m] * hi - zs[m] * hr)
    s02, d02, s13, d13 = gr[0] + gr[2], gr[0] - gr[2], gr[1] + gr[3], gr[1] - gr[3]
    t02, e02, t13, e13 = gi[0] + gi[2], gi[0] - gi[2], gi[1] + gi[3], gi[1] - gi[3]
    pc = [s02 + s13, d02 - t13, s02 - s13, d02 + t13]
    ps = [-e02 - e13, -t02 - d13, e13 - e02, d13 - t02]
    for r in range(RADIX):
        p_ref[r] = pc[r].astype(p_ref.dtype)
        p_ref[RADIX + r] = ps[r].astype(p_ref.dtype)


def _conv_spectrum(tab, z, h_cos, h_sin, layer, order, bsz, seq):
    q = seq // RADIX
    tq = min(256, q)
    res = lambda r: pl.BlockSpec((None, None, q, GROUP_W), lambda i, b: (b, r, 0, 0))
    filt = lambda half: pl.BlockSpec((None, RADIX, tq, GROUP_W), lambda i, b: (layer, half, i, order))
    return pl.pallas_call(
        _conv_spectrum_kernel,
        grid=(q // tq, bsz),
        in_specs=[pl.BlockSpec((2 * RADIX, tq, q), lambda i, b: (0, i, 0)), res(0), res(1), res(2), res(3),
                  filt(0), filt(1)],
        out_specs=pl.BlockSpec((None, 2 * RADIX, tq, GROUP_W), lambda i, b: (b, 0, i, 0)),
        out_shape=jax.ShapeDtypeStruct((bsz, 2 * RADIX, q, GROUP_W), BF16),
        compiler_params=_cp("parallel", "arbitrary"),
        name="hyena_spectrum",
    )(tab, z, z, z, z, h_cos, h_sin)


def _inverse_kernel(tab_ref, p_ref, xm_ref, w_ref, skip_ref, gn_ref, o_ref, *scratch, inv_len, final):
    tj = xm_ref.shape[1]
    for r in range(RADIX):
        y = jnp.dot(tab_ref[r], p_ref[r], preferred_element_type=F32)
        y = y + jnp.dot(tab_ref[RADIX + r], p_ref[RADIX + r], preferred_element_type=F32)
        out = xm_ref[r].astype(F32) * (y * inv_len + w_ref[r].astype(F32) * skip_ref[...])
        if final:
            out = _rms(out, gn_ref[...])
            for c in range(GROUP_W // LANE):
                scratch[0][c, pl.ds(r, tj, stride=RADIX), :] = out[:, c * LANE:(c + 1) * LANE]
        else:
            o_ref[r] = out.astype(o_ref.dtype)
    if final:
        for c in range(GROUP_W // LANE):
            o_ref[:, c * LANE:(c + 1) * LANE] = scratch[0][c].astype(o_ref.dtype)


def _inverse(tab_t, pcs, xm, w, skip, gn, bsz, seq, final):
    q = seq // RADIX
    tj = min(256, q)
    nj = q // tj
    tile = pl.BlockSpec((None, RADIX, tj, GROUP_W), lambda i, b: (b, 0, i, 0))
    vec = pl.BlockSpec((1, GROUP_W), lambda i, b: (0, 0))
    if final:
        out_spec = pl.BlockSpec((RADIX * tj, GROUP_W), lambda i, b: (b * nj + i, 0))
        out_shape = jax.ShapeDtypeStruct((bsz * seq, GROUP_W), BF16)
        scratch = [pltpu.VMEM((GROUP_W // LANE, RADIX * tj, LANE), F32)]
    else:
        out_spec, out_shape, scratch = tile, jax.ShapeDtypeStruct((bsz, RADIX, q, GROUP_W), BF16), []
    return pl.pallas_call(
        functools.partial(_inverse_kernel, inv_len=1.0 / seq, final=final),
        grid=(nj, bsz),
        in_specs=[pl.BlockSpec((2 * RADIX, tj, q), lambda i, b: (0, i, 0)),
                  pl.BlockSpec((None, 2 * RADIX, q, GROUP_W), lambda i, b: (b, 0, 0, 0)),
                  tile, tile, vec, vec],
        out_specs=out_spec,
        out_shape=out_shape,
        scratch_shapes=scratch,
        compiler_params=_cp("parallel", "arbitrary"),
        name="hyena_inverse",
    )(tab_t, pcs, xm, w, skip, gn)


def _out_kernel(ya_ref, yb_ref, yc_ref, yd_ref, w_ref, x_ref, o_ref):
    acc = x_ref[...]
    for g, y_ref in enumerate((ya_ref, yb_ref, yc_ref, yd_ref)):
        acc = acc + jnp.dot(y_ref[...], w_ref[g * GROUP_W:(g + 1) * GROUP_W, :], preferred_element_type=F32)
    o_ref[...] = acc


def _out_proj(ys, w, x):
    m, n = x.shape
    tm = min(512, m)
    piece = pl.BlockSpec((tm, GROUP_W), lambda i: (i, 0))
    xs = pl.BlockSpec((tm, n), lambda i: (i, 0))
    return pl.pallas_call(
        _out_kernel,
        grid=(m // tm,),
        in_specs=[piece, piece, piece, piece, _resident(w.shape, lambda i: (0, 0)), xs],
        out_specs=xs,
        out_shape=jax.ShapeDtypeStruct((m, n), F32),
        compiler_params=_cp("parallel"),
        name="out_proj",
    )(*ys, w, x)


def _attn_kernel(q_ref, k_ref, v_ref, wo_ref, x_ref, o_ref, att_ref):
    for h in range(XA_HEADS):
        hs = slice(h * XA_HD, (h + 1) * XA_HD)
        s = lax.dot_general(q_ref[:, hs], k_ref[:, hs], NT, preferred_element_type=F32) * (XA_HD ** -0.5)
        e = jnp.exp(s - jnp.max(s, axis=-1, keepdims=True))
        p = e / jnp.sum(e, axis=-1, keepdims=True)
        att_ref[:, hs] = jnp.dot(p.astype(BF16), v_ref[:, hs], preferred_element_type=F32).astype(BF16)
    o_ref[...] = x_ref[...] + jnp.dot(att_ref[...], wo_ref[...], preferred_element_type=F32)


def _attn(q, kv, wo, x, seq):
    m, d = x.shape
    tm = min(512, seq)
    per = seq // tm
    xs = pl.BlockSpec((tm, d), lambda i: (i, 0))
    return pl.pallas_call(
        _attn_kernel,
        grid=(m // tm,),
        in_specs=[xs,
                  pl.BlockSpec((N_MEM, d), lambda i: (i // per, 0)),
                  pl.BlockSpec((N_MEM, d), lambda i: (i // per, 1)),
                  _resident((d, d), lambda i: (0, 0)),
                  xs],
        out_specs=xs,
        out_shape=jax.ShapeDtypeStruct((m, d), F32),
        scratch_shapes=[pltpu.VMEM((tm, d), BF16)],
        compiler_params=_cp("parallel"),
        name="cross_attn",
    )(q, kv, kv, wo, x)


def _swiglu_kernel(x_ref, g_ref, wg_ref, wu_ref, wd_ref, fg_ref, o_ref, hn_ref, *, final):
    f = pl.program_id(1)

    @pl.when(f == 0)
    def _():
        _norm_rows_to(x_ref, g_ref, hn_ref)
        o_ref[...] = x_ref[...]

    hn = hn_ref[...]
    gate = jnp.dot(hn, wg_ref[...], preferred_element_type=F32)
    up = jnp.dot(hn, wu_ref[...], preferred_element_type=F32)
    act = (gate * jax.nn.sigmoid(gate) * up).astype(BF16)
    o_ref[...] += jnp.dot(act, wd_ref[...], preferred_element_type=F32)

    if final:
        @pl.when(f == pl.num_programs(1) - 1)
        def _():
            _norm_rows_to(o_ref, fg_ref, o_ref)


def _swiglu(x, g, w_gu, w_d, final_g, final):
    m, d = x.shape
    dff = w_d.shape[0]
    tm, tf = min(512, m), 512
    nf = dff // tf
    xs = pl.BlockSpec((tm, d), lambda i, f: (i, 0))
    vec = pl.BlockSpec((1, d), lambda i, f: (0, 0))
    return pl.pallas_call(
        functools.partial(_swiglu_kernel, final=final),
        grid=(m // tm, nf),
        in_specs=[xs, vec,
                  pl.BlockSpec((d, tf), lambda i, f: (0, f)),
                  pl.BlockSpec((d, tf), lambda i, f: (0, nf + f)),
                  pl.BlockSpec((tf, d), lambda i, f: (f, 0)),
                  vec],
        out_specs=xs,
        out_shape=jax.ShapeDtypeStruct((m, d), F32),
        scratch_shapes=[pltpu.VMEM((tm, d), BF16)],
        compiler_params=_cp("parallel", "arbitrary"),
        name="swiglu",
    )(x, g.reshape(1, d), w_gu, w_gu, w_d, final_g.reshape(1, d))


def _angles(rows, cols, period):
    m = (rows[..., :, None] * cols[..., None, :]) % period
    return m.astype(F32) * (2.0 * math.pi / period)


def _cos_sin(ang, sin_sign=1.0):
    return jnp.concatenate([jnp.cos(ang), sin_sign * jnp.sin(ang)], axis=0).astype(BF16)


def _tables(seq):
    q = seq // RADIX
    base = jnp.arange(q, dtype=jnp.int32)
    times = RADIX * base[None, :] + jnp.arange(RADIX, dtype=jnp.int32)[:, None]
    rep = jnp.broadcast_to(base, (RADIX, q))
    fnet_tab = _cos_sin(_angles(rep, times, seq))
    ch = jnp.arange(GROUP_W, dtype=jnp.int32)
    same = (ch[:, None] // FNET_CH) == (ch[None, :] // FNET_CH)
    angc = _angles(ch % FNET_CH, ch % FNET_CH, FNET_CH)
    cc = jnp.where(same, jnp.cos(angc), 0.0).astype(BF16)
    sc = jnp.where(same, jnp.sin(angc), 0.0).astype(BF16)
    hy_tab = _cos_sin(_angles(2 * rep + 1, times, 4 * seq))
    hy_tab_t = _cos_sin(_angles(times, 2 * rep + 1, 4 * seq))
    return fnet_tab, cc, sc, hy_tab, hy_tab_t


def _position_features(seq):
    pos = jnp.arange(seq, dtype=F32)
    t = pos / seq
    f = jnp.linspace(1e-4, HY_BANDS - 1, HY_BANDS, dtype=F32)
    ang = (2.0 * math.pi * t)[:, None] * f[None, :]
    feats = jnp.concatenate([t[:, None], jnp.cos(ang), -jnp.sin(ang)], axis=-1)
    return jnp.pad(feats, ((0, 0), (0, LANE - HY_EMB)))


def kernel(x, mem, norm_g, w_in, gla_gk_w, gla_gk_b, gla_norm_g, hy_conv_w, hy_ffn_w1, hy_ffn_b1, hy_ffn_w2, hy_ffn_b2, hy_ffn_w3, hy_sin_freq, hy_decay, hy_skip, sc_conv_w, grp_norm_g, w_out, mem_norm_g, w_xq, w_xkv, w_xo, w_gate_up, w_down, final_norm_g):
    bsz, seq, d = x.shape
    depth = norm_g.shape[0]
    m = bsz * seq
    assert d == D_MODEL and seq % (RADIX * GLA_CHUNK) == 0 and mem.shape[1] == N_MEM

    q_w, k_w, v_w, g_w, lr_w, uf_w, uh_w, us_w = jnp.split(
        w_in.astype(BF16), [256, 512, 1024, 1536, 1568, 2080, 3616], axis=-1)
    w_p = jnp.concatenate([uh_w, us_w, q_w, k_w, v_w, g_w, uf_w], axis=-1)
    w_lr = jnp.pad(lr_w, ((0, 0), (0, 0), (0, LANE - 2 * GLA_RANK)))
    wgf = jnp.pad(gla_gk_w[:, 0], ((0, 0), (0, LANE - GLA_RANK), (0, 0))).astype(BF16)
    wgb = jnp.pad(gla_gk_w[:, 1], ((0, 0), (GLA_RANK, LANE - 2 * GLA_RANK), (0, 0))).astype(BF16)
    gkb = gla_gk_b.astype(F32)
    w_out_b, w_xq_b, w_xkv_b, w_xo_b = (t.astype(BF16) for t in (w_out, w_xq, w_xkv, w_xo))
    w_gu_b, w_d_b = w_gate_up.astype(BF16), w_down.astype(BF16)
    fnet_tab, cc, sc, hy_tab, hy_tab_t = _tables(seq)
    qlen = seq // RADIX
    feats = _position_features(seq).reshape(qlen, RADIX, LANE).transpose(1, 0, 2).reshape(seq, LANE)

    w1 = jnp.pad(hy_ffn_w1.astype(F32), ((0, 0), (0, LANE - HY_EMB), (0, 0)))
    fa, fb = _filters(feats, w1, hy_ffn_b1.reshape(depth, 1, HY_FFN).astype(F32), hy_ffn_w2.astype(F32),
                      hy_ffn_b2.reshape(depth, 1, HY_FFN).astype(F32), hy_ffn_w3.astype(F32),
                      hy_sin_freq.reshape(depth, 1, HY_FFN).astype(F32),
                      hy_decay.reshape(depth, 1, 4 * GROUP_W).astype(F32), seq)
    res_major = lambda t: t.reshape(depth, RADIX, qlen, t.shape[-1])
    h_cos = _filter_spectrum(hy_tab, res_major(fa), seq)
    h_sin = _filter_spectrum(hy_tab, res_major(fb), seq)

    mem2 = mem.reshape(bsz * N_MEM, d)
    xf = x.reshape(m, d)
    for l in range(depth):
        p, lr = _norm_matmul(xf, norm_g[l, 0], w_p[l], 1024, 1280, w_side=w_lr[l])
        y_a = _gla(p, lr, wgf[l], wgb[l], gkb[l, 0:1], gkb[l, 1:2], gla_norm_g[l].reshape(1, GLA_DV).astype(F32),
                   bsz, seq)
        hv, hx1, hx2, uf, y_d = _short(p, hy_conv_w[l].astype(F32), sc_conv_w[l].astype(F32),
                                       grp_norm_g[l, 2:3].astype(F32), bsz, seq)
        y_b = _fnet(uf, fnet_tab, cc, sc, grp_norm_g[l, 0:1].astype(F32), bsz, seq)
        gn_c = grp_norm_g[l, 1:2].astype(F32)
        pcs = _conv_spectrum(hy_tab, hv, h_cos, h_sin, l, 0, bsz, seq)
        z = _inverse(hy_tab_t, pcs, hx1, hv, hy_skip[l, 0:1].astype(F32), gn_c, bsz, seq, final=False)
        pcs = _conv_spectrum(hy_tab, z, h_cos, h_sin, l, 1, bsz, seq)
        y_c = _inverse(hy_tab_t, pcs, hx2, z, hy_skip[l, 1:2].astype(F32), gn_c, bsz, seq, final=True)
        xf = _out_proj((y_a, y_b, y_c, y_d), w_out_b[l], xf)
        q = _norm_matmul(xf, norm_g[l, 1], w_xq_b[l], 1024, D_MODEL)
        kv = _norm_matmul(mem2, mem_norm_g, w_xkv_b[l], 1024, 1024)
        xf = _attn(q, kv, w_xo_b[l], xf, seq)
        xf = _swiglu(xf, norm_g[l, 2], w_gu_b[l], w_d_b[l], final_norm_g, final=(l == depth - 1))
    return xf.reshape(bsz, seq, d)
```

```python
import functools
import math

import jax
import jax.numpy as jnp
from jax import lax
from jax.experimental import pallas as pl
from jax.experimental.pallas import tpu as pltpu

F32 = jnp.float32
BF16 = jnp.bfloat16
EPS = 1e-6

D_MODEL = 2048
GROUP_W = 512
GLA_HEADS = 4
GLA_DK = 64
GLA_DV = 128
GLA_RANK = 16
GLA_GATE_NORM = 16.0
GLA_CHUNK = 64
FNET_CH = 128
HY_BANDS = 16
HY_EMB = 2 * HY_BANDS + 1
HY_FFN = 64
XA_HEADS = 4
XA_HD = D_MODEL // XA_HEADS
N_MEM = 256
D_FF = 5632
RADIX = 4
LANE = 128
BF16_ROWS = 16
VMEM_LIMIT = 56 * 1024 * 1024

GLA_QK = GLA_HEADS * GLA_DK
P_UH, P_US, P_Q, P_K, P_V, P_G, P_UF = 0, 1536, 3072, 3328, 3584, 4096, 4608
P_W = 5120

NT = (((1,), (1,)), ((), ()))
TN = (((0,), (0,)), ((), ()))


def _cp(*sem):
    return pltpu.CompilerParams(dimension_semantics=sem, vmem_limit_bytes=VMEM_LIMIT)


def _resident(shape, index_map):
    return pl.BlockSpec(shape, index_map, pipeline_mode=pl.Buffered(1))


def _rms(x, g):
    ms = jnp.mean(x * x, axis=-1, keepdims=True)
    return x * lax.rsqrt(ms + EPS) * g


def _norm_rows_to(x_ref, g_ref, hn_ref):
    rows = min(256, x_ref.shape[0])
    n = x_ref.shape[0] // rows

    def body(i, c):
        r = pl.ds(pl.multiple_of(i * rows, rows), rows)
        hn_ref[r, :] = _rms(x_ref[r, :], g_ref[...]).astype(hn_ref.dtype)
        return c

    lax.fori_loop(0, n, body, 0)


def _norm_matmul_kernel(x_ref, g_ref, w_ref, *rest, with_side, row_tiles, col_tiles):
    if with_side:
        w2_ref, o_ref, o2_ref, hn_ref = rest
    else:
        o_ref, hn_ref = rest
    i, j = pl.program_id(0), pl.program_id(1)
    part = x_ref.shape[0] // col_tiles

    def norm_part():
        g = g_ref[...]
        for r in range(0, part, BF16_ROWS):
            rows = pl.ds(pl.multiple_of(j * part + r, BF16_ROWS), BF16_ROWS)
            hn_ref[i % 2, rows, :] = _rms(x_ref[rows, :], g).astype(BF16)

    def matmul():
        o_ref[...] = jnp.dot(hn_ref[(i + 1) % 2], w_ref[...], preferred_element_type=F32).astype(o_ref.dtype)

    @pl.when(i == 0)
    def _():
        norm_part()

    @pl.when((i > 0) & (i < row_tiles))
    def _():
        matmul()
        norm_part()

    @pl.when(i == row_tiles)
    def _():
        matmul()

    if with_side:
        @pl.when((i > 0) & (j == 0))
        def _():
            o2_ref[...] = jnp.dot(hn_ref[(i + 1) % 2], w2_ref[...], preferred_element_type=F32)


def _norm_matmul(x, g, w, tm, tn, w_side=None):
    m, k = x.shape
    n = w.shape[1]
    tm, tn = min(tm, m), min(tn, n)
    ni, nj = m // tm, n // tn
    assert (tm // nj) % BF16_ROWS == 0
    prev = lambda i: jnp.maximum(i - 1, 0)
    in_specs = [
        pl.BlockSpec((tm, k), lambda i, j: (jnp.minimum(i, ni - 1), 0)),
        pl.BlockSpec((1, k), lambda i, j: (0, 0)),
        _resident((k, tn), lambda i, j: (0, j)) if tn == n else pl.BlockSpec((k, tn), lambda i, j: (0, j)),
    ]
    out_shape = [jax.ShapeDtypeStruct((m, n), BF16)]
    out_specs = [pl.BlockSpec((tm, tn), lambda i, j: (prev(i), jnp.where(i == 0, 0, j)))]
    args = [x, g.reshape(1, k), w]
    if w_side is not None:
        n2 = w_side.shape[1]
        in_specs.append(pl.BlockSpec((k, n2), lambda i, j: (0, 0)))
        out_shape.append(jax.ShapeDtypeStruct((m, n2), F32))
        out_specs.append(pl.BlockSpec((tm, n2), lambda i, j: (prev(i), 0)))
        args.append(w_side)
    res = pl.pallas_call(
        functools.partial(_norm_matmul_kernel, with_side=w_side is not None, row_tiles=ni, col_tiles=nj),
        grid=(ni + 1, nj),
        in_specs=in_specs,
        out_specs=out_specs,
        out_shape=out_shape,
        scratch_shapes=[pltpu.VMEM((2, tm, k), BF16)],
        compiler_params=_cp("arbitrary", "arbitrary"),
        name="norm_matmul",
    )(*args)
    return res if w_side is not None else res[0]


def _gla_pass(fwd, blk, q_ref, k_ref, v_ref, lr_ref, g_ref, wg_ref, bg_ref, gn_ref, y_ref, st_ref, ob_ref):
    ns, ts = q_ref.shape[0], q_ref.shape[1]
    c_sz = GLA_CHUNK
    nc = ts // c_sz
    ii = lax.broadcasted_iota(jnp.int32, (c_sz, c_sz), 0)
    jj = lax.broadcasted_iota(jnp.int32, (c_sz, c_sz), 1)
    incl = (jj <= ii) if fwd else (jj >= ii)
    mask = incl if fwd else (jj > ii)
    tri = jnp.where(incl, 1.0, 0.0).astype(BF16)
    gn = gn_ref[...]
    chunk = lambda c: slice(c * c_sz, (c + 1) * c_sz)
    order = list(range(nc)) if fwd else list(reversed(range(nc)))
    problems = [(s, h) for s in range(ns) for h in range(GLA_HEADS)]
    pair = lambda h: slice((h // 2) * LANE, (h // 2 + 1) * LANE)
    val = lambda h: slice(h * GLA_DV, (h + 1) * GLA_DV)
    first = lax.broadcasted_iota(jnp.int32, (c_sz, LANE), 1) < GLA_DK
    own = lambda h: first if h % 2 == 0 else jnp.logical_not(first)

    qe, ke, qi, ku, dec = {}, {}, {}, {}, {}
    for s in range(ns):
        x = jnp.dot(lr_ref[s].astype(BF16), wg_ref[...], preferred_element_type=F32) + bg_ref[...]
        gk = (jnp.minimum(x, 0.0) - jnp.log1p(jnp.exp(-jnp.abs(x)))) * (1.0 / GLA_GATE_NORM)
        hi = gk.astype(BF16)
        lo = (gk - hi.astype(F32)).astype(BF16)
        for c in range(nc):
            r = chunk(c)
            b = jnp.dot(tri, hi[r], preferred_element_type=F32) + jnp.dot(tri, lo[r], preferred_element_type=F32)
            bref = b[c_sz // 2:c_sz // 2 + 1] if fwd else b[c_sz // 2 - 1:c_sz // 2]
            btot = b[c_sz - 1:c_sz] if fwd else b[0:1]
            qe_f = q_ref[s, r, :].astype(F32) * (GLA_DK ** -0.5) * jnp.exp(b - bref)
            ke_f = k_ref[s, r, :].astype(F32) * jnp.exp(bref - b)
            qe[s, c] = qe_f
            ke[s, c] = ke_f.astype(BF16)
            qi[s, c] = (qe_f * jnp.exp(bref)).astype(BF16)
            ku[s, c] = ke_f * jnp.exp(btot - bref)
            dec[s, c] = jnp.exp(btot)

    kv = {}
    for c in order:
        for s, h in problems:
            kv[s, h, c] = lax.dot_general(v_ref[s, chunk(c), val(h)],
                                          jnp.where(own(h), ku[s, c][:, pair(h)], 0.0).astype(BF16), TN,
                                          preferred_element_type=F32)
    st = {(s, h): st_ref[s, h] for s, h in problems}
    st_in = {}
    for c in order:
        for s, h in problems:
            st_in[s, h, c] = st[s, h].astype(BF16)
            st[s, h] = st[s, h] * dec[s, c][:, pair(h)] + kv[s, h, c]
    for s, h in problems:
        st_ref[s, h] = st[s, h]
    for c in order:
        r = chunk(c)
        rows = pl.ds(pl.multiple_of(blk * ts + c * c_sz, c_sz), c_sz)
        scores = {}
        for s, h in problems:
            sc = lax.dot_general(jnp.where(own(h), qe[s, c][:, pair(h)], 0.0).astype(BF16), ke[s, c][:, pair(h)], NT,
                                 preferred_element_type=F32)
            scores[s, h] = jnp.where(mask, sc, 0.0).astype(BF16)
        for s, h in problems:
            o = jnp.dot(scores[s, h], v_ref[s, r, val(h)], preferred_element_type=F32)
            o = o + lax.dot_general(qi[s, c][:, pair(h)], st_in[s, h, c], NT, preferred_element_type=F32)
            if fwd:
                tot = o + ob_ref[s, rows, val(h)]
                gate = g_ref[s, r, val(h)].astype(F32)
                y_ref[s, r, val(h)] = (_rms(tot, gn) * (gate * jax.nn.sigmoid(gate))).astype(y_ref.dtype)
            else:
                ob_ref[s, rows, val(h)] = o


def _gla_kernel(q_ref, k_ref, v_ref, lr_ref, g_ref, wgf_ref, wgb_ref, bgf_ref, bgb_ref, gn_ref,
                y_ref, st_ref, ob_ref, *, nb):
    n = pl.program_id(1)

    @pl.when((n == 0) | (n == nb))
    def _():
        st_ref[...] = jnp.zeros_like(st_ref)

    @pl.when(n < nb)
    def _():
        _gla_pass(False, nb - 1 - n, q_ref, k_ref, v_ref, lr_ref, g_ref, wgb_ref, bgb_ref, gn_ref, y_ref, st_ref, ob_ref)

    @pl.when(n >= nb)
    def _():
        _gla_pass(True, n - nb, q_ref, k_ref, v_ref, lr_ref, g_ref, wgf_ref, bgf_ref, gn_ref, y_ref, st_ref, ob_ref)


def _gla(p, lr, wgf, wgb, bgf, bgb, gn, bsz, seq):
    ts = min(512, seq)
    nb = seq // ts
    ns = 2 if bsz % 2 == 0 else 1
    blk = lambda n: jnp.where(n < nb, nb - 1 - n, n - nb)
    oblk = lambda n: jnp.where(n < nb, 0, n - nb)
    in_specs = [
        pl.BlockSpec((ns, ts, GLA_QK), lambda b, n: (b, blk(n), P_Q // GLA_QK)),
        pl.BlockSpec((ns, ts, GLA_QK), lambda b, n: (b, blk(n), P_K // GLA_QK)),
        pl.BlockSpec((ns, ts, GROUP_W), lambda b, n: (b, blk(n), P_V // GROUP_W)),
        pl.BlockSpec((ns, ts, LANE), lambda b, n: (b, blk(n), 0)),
        pl.BlockSpec((ns, ts, GROUP_W), lambda b, n: (b, oblk(n), P_G // GROUP_W)),
        pl.BlockSpec((LANE, GLA_QK), lambda b, n: (0, 0)),
        pl.BlockSpec((LANE, GLA_QK), lambda b, n: (0, 0)),
        pl.BlockSpec((1, GLA_QK), lambda b, n: (0, 0)),
        pl.BlockSpec((1, GLA_QK), lambda b, n: (0, 0)),
        pl.BlockSpec((1, GLA_DV), lambda b, n: (0, 0)),
    ]
    p3 = p.reshape(bsz, seq, P_W)
    y = pl.pallas_call(
        functools.partial(_gla_kernel, nb=nb),
        grid=(bsz // ns, 2 * nb),
        in_specs=in_specs,
        out_specs=pl.BlockSpec((ns, ts, GROUP_W), lambda b, n: (b, oblk(n), 0)),
        out_shape=jax.ShapeDtypeStruct((bsz, seq, GROUP_W), BF16),
        scratch_shapes=[pltpu.VMEM((ns, GLA_HEADS, GLA_DV, LANE), F32), pltpu.VMEM((ns, seq, GROUP_W), F32)],
        compiler_params=_cp("parallel", "arbitrary"),
        name="gla",
    )(p3, p3, p3, lr.reshape(bsz, seq, LANE), p3, wgf, wgb, bgf, bgb, gn)
    return y.reshape(bsz * seq, GROUP_W)


def _residue_products(tab_ref, z_refs):
    a = [jnp.dot(tab_ref[r], z_refs[r][...], preferred_element_type=F32) for r in range(RADIX)]
    b = [jnp.dot(tab_ref[RADIX + r], z_refs[r][...], preferred_element_type=F32) for r in range(RADIX)]
    return a, b


def _butterflies(a, b):
    a02p, a02m, a13p, a13m = a[0] + a[2], a[0] - a[2], a[1] + a[3], a[1] - a[3]
    b02p, b02m, b13p, b13m = b[0] + b[2], b[0] - b[2], b[1] + b[3], b[1] - b[3]
    re = [a02p + a13p, a02m - b13m, a02p - a13p, a02m + b13m]
    nim = [b02p + b13p, b02m + a13m, b02p - b13p, b02m - a13m]
    return re, nim


class _Lagged:
    def __init__(self, n_outer, n_inner):
        self.total = n_outer * n_inner
        self.n_inner = n_inner

    def cur(self, t):
        t = jnp.minimum(t, self.total - 1)
        return t // self.n_inner, t % self.n_inner

    def prev(self, t):
        t = jnp.maximum(t - 1, 0)
        return t // self.n_inner, t % self.n_inner

    def run(self, produce, finish):
        t = pl.program_id(0)

        def emit(parts):
            for part in parts:
                part()

        @pl.when(t == 0)
        def _():
            emit(produce(0))

        for parity in (0, 1):
            @pl.when((t > 0) & (t < self.total) & (t % 2 == parity))
            def _():
                heavy, light = produce(parity), finish(1 - parity)
                for k, part in enumerate(heavy):
                    part()
                    emit(light[k * len(light) // len(heavy):(k + 1) * len(light) // len(heavy)])

        @pl.when(t == self.total)
        def _():
            emit(finish((self.total + 1) % 2))


def _product_parts(tab_ref, z_refs, ab_ref, slot):
    def part(k):
        def run():
            ab_ref[slot, k] = jnp.dot(tab_ref[k], z_refs[k % RADIX][...], preferred_element_type=F32)
        return run
    return [part(k) for k in range(2 * RADIX)]


def _load_products(ab_ref, slot, rows=slice(None), cols=slice(None)):
    return ([ab_ref[slot, r, rows, cols] for r in range(RADIX)],
            [ab_ref[slot, RADIX + r, rows, cols] for r in range(RADIX)])


def _groups(n_rows, n_cols, cols_per_group=2 * LANE):
    return [(slice(r, r + BF16_ROWS), slice(c, c + cols_per_group))
            for r in range(0, n_rows, BF16_ROWS) for c in range(0, n_cols, cols_per_group)]


def _fnet_kernel(tab_ref, x0, x1, x2, x3, cc_ref, sc_ref, gn_ref, y_ref, *, scale):
    a, b = _residue_products(tab_ref, (x0, x1, x2, x3))
    re, nim = _butterflies(a, b)
    for m in range(RADIX):
        y = jnp.dot(re[m].astype(BF16), cc_ref[...], preferred_element_type=F32)
        y = y - jnp.dot(nim[m].astype(BF16), sc_ref[...], preferred_element_type=F32)
        y_ref[m] = _rms(y * scale, gn_ref[...]).astype(y_ref.dtype)


def _fnet(uf, tab, cc, sc, gn, bsz, seq):
    q = seq // RADIX
    tq = min(256, q)
    nq = q // tq
    res = lambda r: pl.BlockSpec((None, None, q, GROUP_W), lambda i, b: (b, r, 0, 0))
    sq = pl.BlockSpec((GROUP_W, GROUP_W), lambda i, b: (0, 0))
    out = pl.pallas_call(
        functools.partial(_fnet_kernel, scale=1.0 / math.sqrt(seq * FNET_CH)),
        grid=(nq, bsz),
        in_specs=[pl.BlockSpec((2 * RADIX, tq, q), lambda i, b: (0, i, 0)),
                  res(0), res(1), res(2), res(3), sq, sq,
                  pl.BlockSpec((1, GROUP_W), lambda i, b: (0, 0))],
        out_specs=pl.BlockSpec((None, RADIX, tq, GROUP_W), lambda i, b: (b, 0, i, 0)),
        out_shape=jax.ShapeDtypeStruct((bsz, RADIX, q, GROUP_W), BF16),
        compiler_params=_cp("parallel", "arbitrary"),
        name="fnet",
    )(tab, uf, uf, uf, uf, cc, sc, gn)
    return out.reshape(bsz * seq, GROUP_W)


def _conv3(main, prev_row, next_row, w):
    ts = main.shape[0]
    rid = lax.broadcasted_iota(jnp.int32, main.shape, 0)
    up = jnp.where(rid == 0, prev_row, pltpu.roll(main, 1, 0))
    dn = jnp.where(rid == ts - 1, next_row, pltpu.roll(main, ts - 1, 0))
    return up * w[0:1] + main * w[1:2] + dn * w[2:3]


def _short_kernel(hm_ref, hp_ref, hn_ref, sm_ref, sp_ref, sn_ref, uf_ref, hw_ref, sw_ref, gn_ref,
                  v_ref, x1_ref, x2_ref, ufr_ref, yd_ref, nat_ref, *, nblk):
    i = pl.program_id(0)
    keep_prev = jnp.where(i % nblk == 0, 0.0, 1.0)
    keep_next = jnp.where(i % nblk == nblk - 1, 0.0, 1.0)
    last = BF16_ROWS - 1
    w = GROUP_W
    ts = hm_ref.shape[0]
    ch = _conv3(hm_ref[...].astype(F32), hp_ref[last:last + 1, :].astype(F32) * keep_prev,
                hn_ref[0:1, :].astype(F32) * keep_next, hw_ref[...])
    nat = jnp.concatenate([ch, uf_ref[...].astype(F32)], axis=1)
    per = w // LANE
    for c in range(4 * per):
        nat_ref[c] = nat[:, c * LANE:(c + 1) * LANE]
    for g, dst in enumerate((v_ref, x1_ref, x2_ref, ufr_ref)):
        for r in range(RADIX):
            for c in range(per):
                piece = nat_ref[g * per + c, pl.ds(r, ts // RADIX, stride=RADIX), :]
                dst[r, :, c * LANE:(c + 1) * LANE] = piece.astype(dst.dtype)
    us = sm_ref[...].astype(F32)
    sp = sp_ref[last:last + 1, :].astype(F32)
    sn = sn_ref[0:1, :].astype(F32)
    prod = us[:, w:2 * w] * us[:, 2 * w:]
    conv = _conv3(prod, sp[:, w:2 * w] * sp[:, 2 * w:] * keep_prev, sn[:, w:2 * w] * sn[:, 2 * w:] * keep_next,
                  sw_ref[...])
    yd_ref[...] = _rms(us[:, :w] * conv, gn_ref[...]).astype(yd_ref.dtype)


def _short(p, hy_w, sc_w, gn, bsz, seq):
    m = bsz * seq
    ts = min(256, seq)
    nblk = seq // ts
    hb = ts // BF16_ROWS
    nh = m // BF16_ROWS
    wide = 3 * GROUP_W
    main = lambda cb: pl.BlockSpec((ts, wide), lambda i: (i, cb))
    prev = lambda cb: pl.BlockSpec((BF16_ROWS, wide), lambda i: (jnp.maximum(i * hb - 1, 0), cb))
    nxt = lambda cb: pl.BlockSpec((BF16_ROWS, wide), lambda i: (jnp.minimum((i + 1) * hb, nh - 1), cb))
    nat = pl.BlockSpec((ts, GROUP_W), lambda i: (i, 0))
    res = pl.BlockSpec((None, RADIX, ts // RADIX, GROUP_W), lambda i: (i // nblk, 0, i % nblk, 0))
    res_shape = jax.ShapeDtypeStruct((bsz, RADIX, seq // RADIX, GROUP_W), BF16)
    return pl.pallas_call(
        functools.partial(_short_kernel, nblk=nblk),
        grid=(m // ts,),
        in_specs=[main(0), prev(0), nxt(0), main(1), prev(1), nxt(1),
                  pl.BlockSpec((ts, GROUP_W), lambda i: (i, P_UF // GROUP_W)),
                  pl.BlockSpec((3, wide), lambda i: (0, 0)),
                  pl.BlockSpec((3, GROUP_W), lambda i: (0, 0)),
                  pl.BlockSpec((1, GROUP_W), lambda i: (0, 0))],
        out_specs=[res, res, res, res, nat],
        out_shape=[res_shape] * 4 + [jax.ShapeDtypeStruct((m, GROUP_W), BF16)],
        scratch_shapes=[pltpu.VMEM((4 * GROUP_W // LANE, ts, LANE), F32)],
        compiler_params=_cp("parallel"),
        name="short_conv",
    )(p, p, p, p, p, p, p, hy_w, sc_w, gn)


def _filter_kernel(feat_ref, w1_ref, b1_ref, w2_ref, b2_ref, w3_ref, fr_ref, dc_ref, a_ref, b_ref):
    hp = lax.Precision.HIGHEST
    feats = feat_ref[...]
    fr = fr_ref[...]
    h = jnp.sin(fr * (jnp.dot(feats, w1_ref[...], precision=hp, preferred_element_type=F32) + b1_ref[...]))
    h = jnp.sin(fr * (jnp.dot(h, w2_ref[...], precision=hp, preferred_element_type=F32) + b2_ref[...]))
    h = jnp.dot(h, w3_ref[...], precision=hp, preferred_element_type=F32)
    h = h * jnp.exp(-feats[:, 0:1] * jnp.abs(dc_ref[...]))
    tl = h.shape[0]
    pos = lax.broadcasted_iota(jnp.int32, (tl, GROUP_W), 0) + pl.program_id(1) * tl
    for o in range(2):
        hf = h[:, (2 * o) * GROUP_W:(2 * o + 1) * GROUP_W]
        hb = jnp.where(pos == 0, 0.0, h[:, (2 * o + 1) * GROUP_W:(2 * o + 2) * GROUP_W])
        a_ref[:, o * GROUP_W:(o + 1) * GROUP_W] = (hf + hb).astype(a_ref.dtype)
        b_ref[:, o * GROUP_W:(o + 1) * GROUP_W] = (hb - hf).astype(b_ref.dtype)


def _filters(feats, w1, b1, w2, b2, w3, fr, dc, seq):
    depth = w1.shape[0]
    tl = min(512, seq)
    lay = lambda r, c: pl.BlockSpec((None, r, c), lambda l, i: (l, 0, 0))
    out = pl.BlockSpec((None, tl, 2 * GROUP_W), lambda l, i: (l, i, 0))
    return pl.pallas_call(
        _filter_kernel,
        grid=(depth, seq // tl),
        in_specs=[pl.BlockSpec((tl, LANE), lambda l, i: (i, 0)),
                  lay(LANE, HY_FFN), lay(1, HY_FFN), lay(HY_FFN, HY_FFN), lay(1, HY_FFN),
                  lay(HY_FFN, 4 * GROUP_W), lay(1, HY_FFN), lay(1, 4 * GROUP_W)],
        out_specs=[out, out],
        out_shape=[jax.ShapeDtypeStruct((depth, seq, 2 * GROUP_W), BF16)] * 2,
        compiler_params=_cp("parallel", "parallel"),
        name="hyena_filter",
    )(feats, w1, b1, w2, b2, w3, fr, dc)


def _hyena_classes(a, b):
    re, nim = _butterflies(a, b)
    return re, [nim[0], nim[1], -nim[2], -nim[3]]


def _spectrum_kernel(tab_ref, z0, z1, z2, z3, o_ref):
    a, b = _residue_products(tab_ref, (z0, z1, z2, z3))
    zr, zs = _hyena_classes(a, b)
    for m in range(RADIX):
        o_ref[m] = zr[m]
        o_ref[RADIX + m] = zs[m]


def _filter_spectrum(tab, filt, seq):
    depth, _, _, cols = filt.shape
    q = seq // RADIX
    tq = min(256, q)
    ncb = cols // GROUP_W
    res = lambda r: pl.BlockSpec((None, None, q, GROUP_W), lambda i, n: (n // ncb, r, 0, n % ncb))
    return pl.pallas_call(
        _spectrum_kernel,
        grid=(q // tq, depth * ncb),
        in_specs=[pl.BlockSpec((2 * RADIX, tq, q), lambda i, n: (0, i, 0)), res(0), res(1), res(2), res(3)],
        out_specs=pl.BlockSpec((None, 2 * RADIX, tq, GROUP_W), lambda i, n: (n // ncb, 0, i, n % ncb)),
        out_shape=jax.ShapeDtypeStruct((depth, 2 * RADIX, q, cols), F32),
        compiler_params=_cp("parallel", "arbitrary"),
        name="filter_spectrum",
    )(tab, filt, filt, filt, filt)


def _conv_spectrum_kernel(tab_ref, z0, z1, z2, z3, hr_ref, hi_ref, p_ref, ab_ref, *, lag):
    def finish(slot):
        def group(rows, cols):
            def run():
                zr, zs = _hyena_classes(*_load_products(ab_ref, slot, rows, cols))
                gr, gi = [], []
                for m in range(RADIX):
                    hr, hi = hr_ref[m, rows, cols], hi_ref[m, rows, cols]
                    gr.append(zr[m] * hr + zs[m] * hi)
                    gi.append(zr[m] * hi - zs[m] * hr)
                s02, d02, s13, d13 = gr[0] + gr[2], gr[0] - gr[2], gr[1] + gr[3], gr[1] - gr[3]
                t02, e02, t13, e13 = gi[0] + gi[2], gi[0] - gi[2], gi[1] + gi[3], gi[1] - gi[3]
                pc = [s02 + s13, d02 - t13, s02 - s13, d02 + t13]
                ps = [-e02 - e13, -t02 - d13, e13 - e02, d13 - t02]
                for r in range(RADIX):
                    p_ref[r, rows, cols] = pc[r].astype(p_ref.dtype)
                    p_ref[RADIX + r, rows, cols] = ps[r].astype(p_ref.dtype)
            return run

        return [group(rows, cols) for rows, cols in _groups(ab_ref.shape[2], ab_ref.shape[3])]

    lag.run(lambda slot: _product_parts(tab_ref, (z0, z1, z2, z3), ab_ref, slot), finish)


def _conv_spectrum(tab, z, h_cos, h_sin, layer, order, bsz, seq):
    q = seq // RADIX
    tq = min(256, q)
    lag = _Lagged(q // tq, bsz)
    res = lambda r: pl.BlockSpec((None, None, q, GROUP_W), lambda t: (lag.cur(t)[1], r, 0, 0))
    filt = lambda half: pl.BlockSpec((None, RADIX, tq, GROUP_W), lambda t: (layer, half, lag.prev(t)[0], order))
    return pl.pallas_call(
        functools.partial(_conv_spectrum_kernel, lag=lag),
        grid=(lag.total + 1,),
        in_specs=[pl.BlockSpec((2 * RADIX, tq, q), lambda t: (0, lag.cur(t)[0], 0)), res(0), res(1), res(2), res(3),
                  filt(0), filt(1)],
        out_specs=pl.BlockSpec((None, 2 * RADIX, tq, GROUP_W), lambda t: (lag.prev(t)[1], 0, lag.prev(t)[0], 0)),
        out_shape=jax.ShapeDtypeStruct((bsz, 2 * RADIX, q, GROUP_W), BF16),
        scratch_shapes=[pltpu.VMEM((2, 2 * RADIX, tq, GROUP_W), F32)],
        compiler_params=_cp("arbitrary"),
        name="hyena_spectrum",
    )(tab, z, z, z, z, h_cos, h_sin)


def _inverse_kernel(tab_ref, p_ref, xm_ref, w_ref, skip_ref, gn_ref, o_ref, *scratch, inv_len, final):
    tj = xm_ref.shape[1]
    for r in range(RADIX):
        y = jnp.dot(tab_ref[r], p_ref[r], preferred_element_type=F32)
        y = y + jnp.dot(tab_ref[RADIX + r], p_ref[RADIX + r], preferred_element_type=F32)
        out = xm_ref[r].astype(F32) * (y * inv_len + w_ref[r].astype(F32) * skip_ref[...])
        if final:
            out = _rms(out, gn_ref[...])
            for c in range(GROUP_W // LANE):
                scratch[0][c, pl.ds(r, tj, stride=RADIX), :] = out[:, c * LANE:(c + 1) * LANE]
        else:
            o_ref[r] = out.astype(o_ref.dtype)
    if final:
        for c in range(GROUP_W // LANE):
            o_ref[:, c * LANE:(c + 1) * LANE] = scratch[0][c].astype(o_ref.dtype)


def _inverse(tab_t, pcs, xm, w, skip, gn, bsz, seq, final):
    q = seq // RADIX
    tj = min(256, q)
    nj = q // tj
    tile = pl.BlockSpec((None, RADIX, tj, GROUP_W), lambda i, b: (b, 0, i, 0))
    vec = pl.BlockSpec((1, GROUP_W), lambda i, b: (0, 0))
    if final:
        out_spec = pl.BlockSpec((RADIX * tj, GROUP_W), lambda i, b: (b * nj + i, 0))
        out_shape = jax.ShapeDtypeStruct((bsz * seq, GROUP_W), BF16)
        scratch = [pltpu.VMEM((GROUP_W // LANE, RADIX * tj, LANE), F32)]
    else:
        out_spec, out_shape, scratch = tile, jax.ShapeDtypeStruct((bsz, RADIX, q, GROUP_W), BF16), []
    return pl.pallas_call(
        functools.partial(_inverse_kernel, inv_len=1.0 / seq, final=final),
        grid=(nj, bsz),
        in_specs=[pl.BlockSpec((2 * RADIX, tj, q), lambda i, b: (0, i, 0)),
                  pl.BlockSpec((None, 2 * RADIX, q, GROUP_W), lambda i, b: (b, 0, 0, 0)),
                  tile, tile, vec, vec],
        out_specs=out_spec,
        out_shape=out_shape,
        scratch_shapes=scratch,
        compiler_params=_cp("parallel", "arbitrary"),
        name="hyena_inverse",
    )(tab_t, pcs, xm, w, skip, gn)


def _out_kernel(ya_ref, yb_ref, yc_ref, yd_ref, w_ref, x_ref, o_ref):
    acc = x_ref[...]
    for g, y_ref in enumerate((ya_ref, yb_ref, yc_ref, yd_ref)):
        acc = acc + jnp.dot(y_ref[...], w_ref[g * GROUP_W:(g + 1) * GROUP_W, :], preferred_element_type=F32)
    o_ref[...] = acc


def _out_proj(ys, w, x):
    m, n = x.shape
    tm = min(512, m)
    piece = pl.BlockSpec((tm, GROUP_W), lambda i: (i, 0))
    xs = pl.BlockSpec((tm, n), lambda i: (i, 0))
    return pl.pallas_call(
        _out_kernel,
        grid=(m // tm,),
        in_specs=[piece, piece, piece, piece, _resident(w.shape, lambda i: (0, 0)), xs],
        out_specs=xs,
        out_shape=jax.ShapeDtypeStruct((m, n), F32),
        compiler_params=_cp("parallel"),
        name="out_proj",
    )(*ys, w, x)


def _attn_kernel(q_ref, k_ref, v_ref, wo_ref, x_ref, o_ref, att_ref):
    heads = [slice(h * XA_HD, (h + 1) * XA_HD) for h in range(XA_HEADS)]
    scores = [lax.dot_general(q_ref[:, hs], k_ref[:, hs], NT, preferred_element_type=F32) for hs in heads]
    for hs, s in zip(heads, scores):
        s = s * (XA_HD ** -0.5)
        e = jnp.exp(s - jnp.max(s, axis=-1, keepdims=True))
        p = e / jnp.sum(e, axis=-1, keepdims=True)
        att_ref[:, hs] = jnp.dot(p.astype(BF16), v_ref[:, hs], preferred_element_type=F32).astype(BF16)
    o_ref[...] = x_ref[...] + jnp.dot(att_ref[...], wo_ref[...], preferred_element_type=F32)


def _attn(q, kv, wo, x, seq):
    m, d = x.shape
    tm = min(512, seq)
    per = seq // tm
    xs = pl.BlockSpec((tm, d), lambda i: (i, 0))
    return pl.pallas_call(
        _attn_kernel,
        grid=(m // tm,),
        in_specs=[xs,
                  pl.BlockSpec((N_MEM, d), lambda i: (i // per, 0)),
                  pl.BlockSpec((N_MEM, d), lambda i: (i // per, 1)),
                  _resident((d, d), lambda i: (0, 0)),
                  xs],
        out_specs=xs,
        out_shape=jax.ShapeDtypeStruct((m, d), F32),
        scratch_shapes=[pltpu.VMEM((tm, d), BF16)],
        compiler_params=_cp("parallel"),
        name="cross_attn",
    )(q, kv, kv, wo, x)


def _swiglu_kernel(x_hbm, g_ref, wg_ref, wu_ref, wd_ref, fg_ref, o_ref, x_ref, hn_ref, sem, *, final, row_tiles):
    i, f = pl.program_id(0), pl.program_id(1)
    tm = x_ref.shape[0]

    def x_copy(tile):
        return pltpu.make_async_copy(x_hbm.at[pl.ds(pl.multiple_of(tile * tm, tm), tm), :], x_ref, sem)

    @pl.when((i == 0) & (f == 0))
    def _():
        x_copy(0).start()

    @pl.when(f == 0)
    def _():
        x_copy(i).wait()
        _norm_rows_to(x_ref, g_ref, hn_ref)
        o_ref[...] = x_ref[...]

    @pl.when((f == 1) & (i + 1 < row_tiles))
    def _():
        x_copy(i + 1).start()

    hn = hn_ref[...]
    gate = jnp.dot(hn, wg_ref[...], preferred_element_type=F32)
    up = jnp.dot(hn, wu_ref[...], preferred_element_type=F32)
    act = (gate * jax.nn.sigmoid(gate) * up).astype(BF16)
    o_ref[...] += jnp.dot(act, wd_ref[...], preferred_element_type=F32)

    if final:
        @pl.when(f == pl.num_programs(1) - 1)
        def _():
            _norm_rows_to(o_ref, fg_ref, o_ref)


def _swiglu(x, g, w_gu, w_d, final_g, final):
    m, d = x.shape
    dff = w_d.shape[0]
    tm, tf = min(1024, m), 512
    nf = dff // tf
    assert nf >= 2
    vec = pl.BlockSpec((1, d), lambda i, f: (0, 0))
    return pl.pallas_call(
        functools.partial(_swiglu_kernel, final=final, row_tiles=m // tm),
        grid=(m // tm, nf),
        in_specs=[pl.BlockSpec(memory_space=pl.ANY), vec,
                  pl.BlockSpec((d, tf), lambda i, f: (0, f)),
                  pl.BlockSpec((d, tf), lambda i, f: (0, nf + f)),
                  pl.BlockSpec((tf, d), lambda i, f: (f, 0)),
                  vec],
        out_specs=pl.BlockSpec((tm, d), lambda i, f: (i, 0)),
        out_shape=jax.ShapeDtypeStruct((m, d), F32),
        scratch_shapes=[pltpu.VMEM((tm, d), F32), pltpu.VMEM((tm, d), BF16), pltpu.SemaphoreType.DMA(())],
        compiler_params=_cp("arbitrary", "arbitrary"),
        name="swiglu",
    )(x, g.reshape(1, d), w_gu, w_gu, w_d, final_g.reshape(1, d))


def _angles(rows, cols, period):
    m = (rows[..., :, None] * cols[..., None, :]) % period
    return m.astype(F32) * (2.0 * math.pi / period)


def _cos_sin(ang, sin_sign=1.0):
    return jnp.concatenate([jnp.cos(ang), sin_sign * jnp.sin(ang)], axis=0).astype(BF16)


def _tables(seq):
    q = seq // RADIX
    base = jnp.arange(q, dtype=jnp.int32)
    times = RADIX * base[None, :] + jnp.arange(RADIX, dtype=jnp.int32)[:, None]
    rep = jnp.broadcast_to(base, (RADIX, q))
    fnet_tab = _cos_sin(_angles(rep, times, seq))
    ch = jnp.arange(GROUP_W, dtype=jnp.int32)
    same = (ch[:, None] // FNET_CH) == (ch[None, :] // FNET_CH)
    angc = _angles(ch % FNET_CH, ch % FNET_CH, FNET_CH)
    cc = jnp.where(same, jnp.cos(angc), 0.0).astype(BF16)
    sc = jnp.where(same, jnp.sin(angc), 0.0).astype(BF16)
    hy_tab = _cos_sin(_angles(2 * rep + 1, times, 4 * seq))
    hy_tab_t = jnp.swapaxes(hy_tab, 1, 2)
    return fnet_tab, cc, sc, hy_tab, hy_tab_t


def _position_features(seq):
    pos = jnp.arange(seq, dtype=F32)
    t = pos / seq
    f = jnp.linspace(1e-4, HY_BANDS - 1, HY_BANDS, dtype=F32)
    ang = (2.0 * math.pi * t)[:, None] * f[None, :]
    feats = jnp.concatenate([t[:, None], jnp.cos(ang), -jnp.sin(ang)], axis=-1)
    return jnp.pad(feats, ((0, 0), (0, LANE - HY_EMB)))


def kernel(x, mem, norm_g, w_in, gla_gk_w, gla_gk_b, gla_norm_g, hy_conv_w, hy_ffn_w1, hy_ffn_b1, hy_ffn_w2, hy_ffn_b2, hy_ffn_w3, hy_sin_freq, hy_decay, hy_skip, sc_conv_w, grp_norm_g, w_out, mem_norm_g, w_xq, w_xkv, w_xo, w_gate_up, w_down, final_norm_g):
    bsz, seq, d = x.shape
    depth = norm_g.shape[0]
    m = bsz * seq
    assert d == D_MODEL and seq % (RADIX * GLA_CHUNK) == 0 and mem.shape[1] == N_MEM

    q_w, k_w, v_w, g_w, lr_w, uf_w, uh_w, us_w = jnp.split(
        w_in.astype(BF16), [256, 512, 1024, 1536, 1568, 2080, 3616], axis=-1)
    w_p = jnp.concatenate([uh_w, us_w, q_w, k_w, v_w, g_w, uf_w], axis=-1)
    w_lr = jnp.pad(lr_w, ((0, 0), (0, 0), (0, LANE - 2 * GLA_RANK)))
    wgf = jnp.pad(gla_gk_w[:, 0], ((0, 0), (0, LANE - GLA_RANK), (0, 0))).astype(BF16)
    wgb = jnp.pad(gla_gk_w[:, 1], ((0, 0), (GLA_RANK, LANE - 2 * GLA_RANK), (0, 0))).astype(BF16)
    gkb = gla_gk_b.astype(F32)
    w_out_b, w_xq_b, w_xkv_b, w_xo_b = (t.astype(BF16) for t in (w_out, w_xq, w_xkv, w_xo))
    w_gu_b, w_d_b = w_gate_up.astype(BF16), w_down.astype(BF16)
    fnet_tab, cc, sc, hy_tab, hy_tab_t = _tables(seq)
    qlen = seq // RADIX
    feats = _position_features(seq).reshape(qlen, RADIX, LANE).transpose(1, 0, 2).reshape(seq, LANE)

    w1 = jnp.pad(hy_ffn_w1.astype(F32), ((0, 0), (0, LANE - HY_EMB), (0, 0)))
    fa, fb = _filters(feats, w1, hy_ffn_b1.reshape(depth, 1, HY_FFN).astype(F32), hy_ffn_w2.astype(F32),
                      hy_ffn_b2.reshape(depth, 1, HY_FFN).astype(F32), hy_ffn_w3.astype(F32),
                      hy_sin_freq.reshape(depth, 1, HY_FFN).astype(F32),
                      hy_decay.reshape(depth, 1, 4 * GROUP_W).astype(F32), seq)
    res_major = lambda t: t.reshape(depth, RADIX, qlen, t.shape[-1])
    h_cos = _filter_spectrum(hy_tab, res_major(fa), seq)
    h_sin = _filter_spectrum(hy_tab, res_major(fb), seq)

    mem2 = mem.reshape(bsz * N_MEM, d)
    xf = x.reshape(m, d)
    for l in range(depth):
        p, lr = _norm_matmul(xf, norm_g[l, 0], w_p[l], 1024, 1280, w_side=w_lr[l])
        y_a = _gla(p, lr, wgf[l], wgb[l], gkb[l, 0:1], gkb[l, 1:2], gla_norm_g[l].reshape(1, GLA_DV).astype(F32),
                   bsz, seq)
        hv, hx1, hx2, uf, y_d = _short(p, hy_conv_w[l].astype(F32), sc_conv_w[l].astype(F32),
                                       grp_norm_g[l, 2:3].astype(F32), bsz, seq)
        y_b = _fnet(uf, fnet_tab, cc, sc, grp_norm_g[l, 0:1].astype(F32), bsz, seq)
        gn_c = grp_norm_g[l, 1:2].astype(F32)
        pcs = _conv_spectrum(hy_tab, hv, h_cos, h_sin, l, 0, bsz, seq)
        z = _inverse(hy_tab_t, pcs, hx1, hv, hy_skip[l, 0:1].astype(F32), gn_c, bsz, seq, final=False)
        pcs = _conv_spectrum(hy_tab, z, h_cos, h_sin, l, 1, bsz, seq)
        y_c = _inverse(hy_tab_t, pcs, hx2, z, hy_skip[l, 1:2].astype(F32), gn_c, bsz, seq, final=True)
        xf = _out_proj((y_a, y_b, y_c, y_d), w_out_b[l], xf)
        q = _norm_matmul(xf, norm_g[l, 1], w_xq_b[l], 1024, D_MODEL)
        kv = _norm_matmul(mem2, mem_norm_g, w_xkv_b[l], 1024, 1024)
        xf = _attn(q, kv, w_xo_b[l], xf, seq)
        xf = _swiglu(xf, norm_g[l, 2], w_gu_b[l], w_d_b[l], final_norm_g, final=(l == depth - 1))
    return xf.reshape(bsz, seq, d)
```

```python
import functools
import math

import jax
import jax.numpy as jnp
from jax import lax
from jax.experimental import pallas as pl
from jax.experimental.pallas import tpu as pltpu

F32 = jnp.float32
BF16 = jnp.bfloat16
EPS = 1e-6

D_MODEL = 2048
GROUP_W = 512
GLA_HEADS = 4
GLA_DK = 64
GLA_DV = 128
GLA_RANK = 16
GLA_GATE_NORM = 16.0
GLA_CHUNK = 64
FNET_CH = 128
HY_BANDS = 16
HY_EMB = 2 * HY_BANDS + 1
HY_FFN = 64
XA_HEADS = 4
XA_HD = D_MODEL // XA_HEADS
N_MEM = 256
D_FF = 5632
RADIX = 4
LANE = 128
BF16_ROWS = 16
VMEM_LIMIT = 56 * 1024 * 1024

GLA_QK = GLA_HEADS * GLA_DK
P_UH, P_US, P_Q, P_K, P_V, P_G, P_UF = 0, 1536, 3072, 3328, 3584, 4096, 4608
P_W = 5120

NT = (((1,), (1,)), ((), ()))
TN = (((0,), (0,)), ((), ()))


def _cp(*sem):
    return pltpu.CompilerParams(dimension_semantics=sem, vmem_limit_bytes=VMEM_LIMIT)


def _resident(shape, index_map):
    return pl.BlockSpec(shape, index_map, pipeline_mode=pl.Buffered(1))


def _rms(x, g):
    ms = jnp.mean(x * x, axis=-1, keepdims=True)
    return x * lax.rsqrt(ms + EPS) * g


def _norm_rows_to(x_ref, g_ref, hn_ref):
    rows = min(256, x_ref.shape[0])
    n = x_ref.shape[0] // rows

    def body(i, c):
        r = pl.ds(pl.multiple_of(i * rows, rows), rows)
        hn_ref[r, :] = _rms(x_ref[r, :], g_ref[...]).astype(hn_ref.dtype)
        return c

    lax.fori_loop(0, n, body, 0)


def _norm_matmul_kernel(x_ref, g_ref, w_ref, *rest, with_side, row_tiles, col_tiles):
    if with_side:
        w2_ref, o_ref, o2_ref, hn_ref = rest
    else:
        o_ref, hn_ref = rest
    i, j = pl.program_id(0), pl.program_id(1)
    part = x_ref.shape[0] // col_tiles

    def norm_part():
        g = g_ref[...]
        for r in range(0, part, BF16_ROWS):
            rows = pl.ds(pl.multiple_of(j * part + r, BF16_ROWS), BF16_ROWS)
            hn_ref[i % 2, rows, :] = _rms(x_ref[rows, :], g).astype(BF16)

    def matmul():
        o_ref[...] = jnp.dot(hn_ref[(i + 1) % 2], w_ref[...], preferred_element_type=F32).astype(o_ref.dtype)

    @pl.when(i == 0)
    def _():
        norm_part()

    @pl.when((i > 0) & (i < row_tiles))
    def _():
        matmul()
        norm_part()

    @pl.when(i == row_tiles)
    def _():
        matmul()

    if with_side:
        @pl.when((i > 0) & (j == 0))
        def _():
            o2_ref[...] = jnp.dot(hn_ref[(i + 1) % 2], w2_ref[...], preferred_element_type=F32)


def _norm_matmul(x, g, w, tm, tn, w_side=None):
    m, k = x.shape
    n = w.shape[1]
    tm, tn = min(tm, m), min(tn, n)
    ni, nj = m // tm, n // tn
    assert (tm // nj) % BF16_ROWS == 0
    prev = lambda i: jnp.maximum(i - 1, 0)
    in_specs = [
        pl.BlockSpec((tm, k), lambda i, j: (jnp.minimum(i, ni - 1), 0)),
        pl.BlockSpec((1, k), lambda i, j: (0, 0)),
        _resident((k, tn), lambda i, j: (0, j)) if tn == n else pl.BlockSpec((k, tn), lambda i, j: (0, j)),
    ]
    out_shape = [jax.ShapeDtypeStruct((m, n), BF16)]
    out_specs = [pl.BlockSpec((tm, tn), lambda i, j: (prev(i), jnp.where(i == 0, 0, j)))]
    args = [x, g.reshape(1, k), w]
    if w_side is not None:
        n2 = w_side.shape[1]
        in_specs.append(pl.BlockSpec((k, n2), lambda i, j: (0, 0)))
        out_shape.append(jax.ShapeDtypeStruct((m, n2), F32))
        out_specs.append(pl.BlockSpec((tm, n2), lambda i, j: (prev(i), 0)))
        args.append(w_side)
    res = pl.pallas_call(
        functools.partial(_norm_matmul_kernel, with_side=w_side is not None, row_tiles=ni, col_tiles=nj),
        grid=(ni + 1, nj),
        in_specs=in_specs,
        out_specs=out_specs,
        out_shape=out_shape,
        scratch_shapes=[pltpu.VMEM((2, tm, k), BF16)],
        compiler_params=_cp("arbitrary", "arbitrary"),
        name="norm_matmul",
    )(*args)
    return res if w_side is not None else res[0]


def _gla_pass(fwd, blk, q_ref, k_ref, v_ref, lr_ref, g_ref, wg_ref, bg_ref, gn_ref, y_ref, st_ref, ob_ref):
    ns, ts = q_ref.shape[0], q_ref.shape[1]
    c_sz = GLA_CHUNK
    nc = ts // c_sz
    ii = lax.broadcasted_iota(jnp.int32, (c_sz, c_sz), 0)
    jj = lax.broadcasted_iota(jnp.int32, (c_sz, c_sz), 1)
    incl = (jj <= ii) if fwd else (jj >= ii)
    mask = incl if fwd else (jj > ii)
    tri = jnp.where(incl, 1.0, 0.0).astype(BF16)
    gn = gn_ref[...]
    chunk = lambda c: slice(c * c_sz, (c + 1) * c_sz)
    order = list(range(nc)) if fwd else list(reversed(range(nc)))
    problems = [(s, h) for s in range(ns) for h in range(GLA_HEADS)]
    pair = lambda h: slice((h // 2) * LANE, (h // 2 + 1) * LANE)
    val = lambda h: slice(h * GLA_DV, (h + 1) * GLA_DV)
    first = lax.broadcasted_iota(jnp.int32, (c_sz, LANE), 1) < GLA_DK
    own = lambda h: first if h % 2 == 0 else jnp.logical_not(first)

    qe, ke, qi, ku, dec = {}, {}, {}, {}, {}
    for s in range(ns):
        x = jnp.dot(lr_ref[s].astype(BF16), wg_ref[...], preferred_element_type=F32) + bg_ref[...]
        gk = (jnp.minimum(x, 0.0) - jnp.log1p(jnp.exp(-jnp.abs(x)))) * (1.0 / GLA_GATE_NORM)
        hi = gk.astype(BF16)
        lo = (gk - hi.astype(F32)).astype(BF16)
        for c in range(nc):
            r = chunk(c)
            b = jnp.dot(tri, hi[r], preferred_element_type=F32) + jnp.dot(tri, lo[r], preferred_element_type=F32)
            bref = b[c_sz // 2:c_sz // 2 + 1] if fwd else b[c_sz // 2 - 1:c_sz // 2]
            btot = b[c_sz - 1:c_sz] if fwd else b[0:1]
            qe_f = q_ref[s, r, :].astype(F32) * (GLA_DK ** -0.5) * jnp.exp(b - bref)
            ke_f = k_ref[s, r, :].astype(F32) * jnp.exp(bref - b)
            qe[s, c] = qe_f
            ke[s, c] = ke_f.astype(BF16)
            qi[s, c] = (qe_f * jnp.exp(bref)).astype(BF16)
            ku[s, c] = ke_f * jnp.exp(btot - bref)
            dec[s, c] = jnp.exp(btot)

    kv = {}
    for c in order:
        for s, h in problems:
            kv[s, h, c] = lax.dot_general(v_ref[s, chunk(c), val(h)],
                                          jnp.where(own(h), ku[s, c][:, pair(h)], 0.0).astype(BF16), TN,
                                          preferred_element_type=F32)
    st = {(s, h): st_ref[s, h] for s, h in problems}
    st_in = {}
    for c in order:
        for s, h in problems:
            st_in[s, h, c] = st[s, h].astype(BF16)
            st[s, h] = st[s, h] * dec[s, c][:, pair(h)] + kv[s, h, c]
    for s, h in problems:
        st_ref[s, h] = st[s, h]
    for c in order:
        r = chunk(c)
        rows = pl.ds(pl.multiple_of(blk * ts + c * c_sz, c_sz), c_sz)
        scores = {}
        for s, h in problems:
            sc = lax.dot_general(jnp.where(own(h), qe[s, c][:, pair(h)], 0.0).astype(BF16), ke[s, c][:, pair(h)], NT,
                                 preferred_element_type=F32)
            scores[s, h] = jnp.where(mask, sc, 0.0).astype(BF16)
        for s, h in problems:
            o = jnp.dot(scores[s, h], v_ref[s, r, val(h)], preferred_element_type=F32)
            o = o + lax.dot_general(qi[s, c][:, pair(h)], st_in[s, h, c], NT, preferred_element_type=F32)
            if fwd:
                tot = o + ob_ref[s, rows, val(h)]
                gate = g_ref[s, r, val(h)].astype(F32)
                y_ref[s, r, val(h)] = (_rms(tot, gn) * (gate * jax.nn.sigmoid(gate))).astype(y_ref.dtype)
            else:
                ob_ref[s, rows, val(h)] = o


def _gla_kernel(q_ref, k_ref, v_ref, lr_ref, g_ref, wgf_ref, wgb_ref, bgf_ref, bgb_ref, gn_ref,
                y_ref, st_ref, ob_ref, *, nb):
    n = pl.program_id(1)

    @pl.when((n == 0) | (n == nb))
    def _():
        st_ref[...] = jnp.zeros_like(st_ref)

    @pl.when(n < nb)
    def _():
        _gla_pass(False, nb - 1 - n, q_ref, k_ref, v_ref, lr_ref, g_ref, wgb_ref, bgb_ref, gn_ref, y_ref, st_ref, ob_ref)

    @pl.when(n >= nb)
    def _():
        _gla_pass(True, n - nb, q_ref, k_ref, v_ref, lr_ref, g_ref, wgf_ref, bgf_ref, gn_ref, y_ref, st_ref, ob_ref)


def _gla(p, lr, wgf, wgb, bgf, bgb, gn, bsz, seq):
    ts = min(512, seq)
    nb = seq // ts
    ns = 2 if bsz % 2 == 0 else 1
    blk = lambda n: jnp.where(n < nb, nb - 1 - n, n - nb)
    oblk = lambda n: jnp.where(n < nb, 0, n - nb)
    in_specs = [
        pl.BlockSpec((ns, ts, GLA_QK), lambda b, n: (b, blk(n), P_Q // GLA_QK)),
        pl.BlockSpec((ns, ts, GLA_QK), lambda b, n: (b, blk(n), P_K // GLA_QK)),
        pl.BlockSpec((ns, ts, GROUP_W), lambda b, n: (b, blk(n), P_V // GROUP_W)),
        pl.BlockSpec((ns, ts, LANE), lambda b, n: (b, blk(n), 0)),
        pl.BlockSpec((ns, ts, GROUP_W), lambda b, n: (b, oblk(n), P_G // GROUP_W)),
        pl.BlockSpec((LANE, GLA_QK), lambda b, n: (0, 0)),
        pl.BlockSpec((LANE, GLA_QK), lambda b, n: (0, 0)),
        pl.BlockSpec((1, GLA_QK), lambda b, n: (0, 0)),
        pl.BlockSpec((1, GLA_QK), lambda b, n: (0, 0)),
        pl.BlockSpec((1, GLA_DV), lambda b, n: (0, 0)),
    ]
    p3 = p.reshape(bsz, seq, P_W)
    y = pl.pallas_call(
        functools.partial(_gla_kernel, nb=nb),
        grid=(bsz // ns, 2 * nb),
        in_specs=in_specs,
        out_specs=pl.BlockSpec((ns, ts, GROUP_W), lambda b, n: (b, oblk(n), 0)),
        out_shape=jax.ShapeDtypeStruct((bsz, seq, GROUP_W), BF16),
        scratch_shapes=[pltpu.VMEM((ns, GLA_HEADS, GLA_DV, LANE), F32), pltpu.VMEM((ns, seq, GROUP_W), F32)],
        compiler_params=_cp("parallel", "arbitrary"),
        name="gla",
    )(p3, p3, p3, lr.reshape(bsz, seq, LANE), p3, wgf, wgb, bgf, bgb, gn)
    return y.reshape(bsz * seq, GROUP_W)


def _table_rows(tab_ref, tile, rows):
    window = pl.ds(pl.multiple_of(tile * rows, rows), rows)
    return lambda k: tab_ref[k, window, :]


def _residue_products(tab, z_refs):
    a = [jnp.dot(tab(r), z_refs[r][...], preferred_element_type=F32) for r in range(RADIX)]
    b = [jnp.dot(tab(RADIX + r), z_refs[r][...], preferred_element_type=F32) for r in range(RADIX)]
    return a, b


def _butterflies(a, b):
    a02p, a02m, a13p, a13m = a[0] + a[2], a[0] - a[2], a[1] + a[3], a[1] - a[3]
    b02p, b02m, b13p, b13m = b[0] + b[2], b[0] - b[2], b[1] + b[3], b[1] - b[3]
    re = [a02p + a13p, a02m - b13m, a02p - a13p, a02m + b13m]
    nim = [b02p + b13p, b02m + a13m, b02p - b13p, b02m - a13m]
    return re, nim


class _Lagged:
    def __init__(self, n_outer, n_inner):
        self.total = n_outer * n_inner
        self.n_inner = n_inner

    def cur(self, t):
        t = jnp.minimum(t, self.total - 1)
        return t // self.n_inner, t % self.n_inner

    def prev(self, t):
        t = jnp.maximum(t - 1, 0)
        return t // self.n_inner, t % self.n_inner

    def run(self, produce, finish):
        t = pl.program_id(0)

        def emit(parts):
            for part in parts:
                part()

        @pl.when(t == 0)
        def _():
            emit(produce(0))

        for parity in (0, 1):
            @pl.when((t > 0) & (t < self.total) & (t % 2 == parity))
            def _():
                heavy, light = produce(parity), finish(1 - parity)
                for k, part in enumerate(heavy):
                    part()
                    emit(light[k * len(light) // len(heavy):(k + 1) * len(light) // len(heavy)])

        @pl.when(t == self.total)
        def _():
            emit(finish((self.total + 1) % 2))


def _product_parts(tab, z_refs, ab_ref, slot):
    def part(k):
        def run():
            ab_ref[slot, k] = jnp.dot(tab(k), z_refs[k % RADIX][...], preferred_element_type=F32)
        return run
    return [part(k) for k in range(2 * RADIX)]


def _load_products(ab_ref, slot, rows=slice(None), cols=slice(None)):
    return ([ab_ref[slot, r, rows, cols] for r in range(RADIX)],
            [ab_ref[slot, RADIX + r, rows, cols] for r in range(RADIX)])


def _groups(n_rows, n_cols, cols_per_group=2 * LANE):
    return [(slice(r, r + BF16_ROWS), slice(c, c + cols_per_group))
            for r in range(0, n_rows, BF16_ROWS) for c in range(0, n_cols, cols_per_group)]


def _fnet_kernel(tab_ref, x0, x1, x2, x3, cc_ref, sc_ref, gn_ref, y_ref, *, scale):
    a, b = _residue_products(_table_rows(tab_ref, pl.program_id(1), y_ref.shape[1]), (x0, x1, x2, x3))
    re, nim = _butterflies(a, b)
    for m in range(RADIX):
        y = jnp.dot(re[m].astype(BF16), cc_ref[...], preferred_element_type=F32)
        y = y - jnp.dot(nim[m].astype(BF16), sc_ref[...], preferred_element_type=F32)
        y_ref[m] = _rms(y * scale, gn_ref[...]).astype(y_ref.dtype)


def _fnet(uf, tab, cc, sc, gn, bsz, seq):
    q = seq // RADIX
    tq = min(256, q)
    nq = q // tq
    res = lambda r: pl.BlockSpec((None, None, q, GROUP_W), lambda b, i: (b, r, 0, 0))
    sq = pl.BlockSpec((GROUP_W, GROUP_W), lambda b, i: (0, 0))
    out = pl.pallas_call(
        functools.partial(_fnet_kernel, scale=1.0 / math.sqrt(seq * FNET_CH)),
        grid=(bsz, nq),
        in_specs=[_resident((2 * RADIX, q, q), lambda b, i: (0, 0, 0)),
                  res(0), res(1), res(2), res(3), sq, sq,
                  pl.BlockSpec((1, GROUP_W), lambda b, i: (0, 0))],
        out_specs=pl.BlockSpec((None, RADIX, tq, GROUP_W), lambda b, i: (b, 0, i, 0)),
        out_shape=jax.ShapeDtypeStruct((bsz, RADIX, q, GROUP_W), BF16),
        compiler_params=_cp("parallel", "arbitrary"),
        name="fnet",
    )(tab, uf, uf, uf, uf, cc, sc, gn)
    return out.reshape(bsz * seq, GROUP_W)


def _conv3(main, prev_row, next_row, w):
    ts = main.shape[0]
    rid = lax.broadcasted_iota(jnp.int32, main.shape, 0)
    up = jnp.where(rid == 0, prev_row, pltpu.roll(main, 1, 0))
    dn = jnp.where(rid == ts - 1, next_row, pltpu.roll(main, ts - 1, 0))
    return up * w[0:1] + main * w[1:2] + dn * w[2:3]


def _short_kernel(hm_ref, hp_ref, hn_ref, sm_ref, sp_ref, sn_ref, uf_ref, hw_ref, sw_ref, gn_ref,
                  v_ref, x1_ref, x2_ref, ufr_ref, yd_ref, nat_ref, *, nblk):
    i = pl.program_id(0)
    keep_prev = jnp.where(i % nblk == 0, 0.0, 1.0)
    keep_next = jnp.where(i % nblk == nblk - 1, 0.0, 1.0)
    last = BF16_ROWS - 1
    w = GROUP_W
    ts = hm_ref.shape[0]
    per = w // LANE

    def neighbours(ref, halo_prev, halo_next, r0, rows, cols):
        above = (ref[r0 - 1:r0, cols].astype(F32) if r0 > 0
                 else halo_prev[last:last + 1, cols].astype(F32) * keep_prev)
        below = (ref[r0 + rows:r0 + rows + 1, cols].astype(F32) if r0 + rows < ts
                 else halo_next[0:1, cols].astype(F32) * keep_next)
        return above, below

    rows_h, lanes_h = 2 * BF16_ROWS, 2 * LANE
    for r0 in range(0, ts, rows_h):
        for c0 in range(0, 3 * w, lanes_h):
            cols = slice(c0, c0 + lanes_h)
            above, below = neighbours(hm_ref, hp_ref, hn_ref, r0, rows_h, cols)
            conv = _conv3(hm_ref[r0:r0 + rows_h, cols].astype(F32), above, below, hw_ref[:, cols])
            for k in range(lanes_h // LANE):
                nat_ref[c0 // LANE + k, r0:r0 + rows_h, :] = conv[:, k * LANE:(k + 1) * LANE]
    for c in range(per):
        nat_ref[3 * per + c] = uf_ref[:, c * LANE:(c + 1) * LANE].astype(F32)
    for g, dst in enumerate((v_ref, x1_ref, x2_ref, ufr_ref)):
        for r in range(RADIX):
            for c in range(per):
                piece = nat_ref[g * per + c, pl.ds(r, ts // RADIX, stride=RADIX), :]
                dst[r, :, c * LANE:(c + 1) * LANE] = piece.astype(dst.dtype)

    bcol, ccol, hcol = slice(0, w), slice(w, 2 * w), slice(2 * w, 3 * w)
    rows_s = BF16_ROWS
    for r0 in range(0, ts, rows_s):
        rs = slice(r0, r0 + rows_s)
        ca, cb = neighbours(sm_ref, sp_ref, sn_ref, r0, rows_s, ccol)
        ha, hb = neighbours(sm_ref, sp_ref, sn_ref, r0, rows_s, hcol)
        prod = sm_ref[rs, ccol].astype(F32) * sm_ref[rs, hcol].astype(F32)
        conv = _conv3(prod, ca * ha, cb * hb, sw_ref[...])
        yd_ref[rs, :] = _rms(sm_ref[rs, bcol].astype(F32) * conv, gn_ref[...]).astype(yd_ref.dtype)


def _short(p, hy_w, sc_w, gn, bsz, seq):
    m = bsz * seq
    ts = min(256, seq)
    nblk = seq // ts
    hb = ts // BF16_ROWS
    nh = m // BF16_ROWS
    wide = 3 * GROUP_W
    main = lambda cb: pl.BlockSpec((ts, wide), lambda i: (i, cb))
    prev = lambda cb: pl.BlockSpec((BF16_ROWS, wide), lambda i: (jnp.maximum(i * hb - 1, 0), cb))
    nxt = lambda cb: pl.BlockSpec((BF16_ROWS, wide), lambda i: (jnp.minimum((i + 1) * hb, nh - 1), cb))
    nat = pl.BlockSpec((ts, GROUP_W), lambda i: (i, 0))
    res = pl.BlockSpec((None, RADIX, ts // RADIX, GROUP_W), lambda i: (i // nblk, 0, i % nblk, 0))
    res_shape = jax.ShapeDtypeStruct((bsz, RADIX, seq // RADIX, GROUP_W), BF16)
    return pl.pallas_call(
        functools.partial(_short_kernel, nblk=nblk),
        grid=(m // ts,),
        in_specs=[main(0), prev(0), nxt(0), main(1), prev(1), nxt(1),
                  pl.BlockSpec((ts, GROUP_W), lambda i: (i, P_UF // GROUP_W)),
                  pl.BlockSpec((3, wide), lambda i: (0, 0)),
                  pl.BlockSpec((3, GROUP_W), lambda i: (0, 0)),
                  pl.BlockSpec((1, GROUP_W), lambda i: (0, 0))],
        out_specs=[res, res, res, res, nat],
        out_shape=[res_shape] * 4 + [jax.ShapeDtypeStruct((m, GROUP_W), BF16)],
        scratch_shapes=[pltpu.VMEM((4 * GROUP_W // LANE, ts, LANE), F32)],
        compiler_params=_cp("parallel"),
        name="short_conv",
    )(p, p, p, p, p, p, p, hy_w, sc_w, gn)


def _filter_kernel(feat_ref, w1_ref, b1_ref, w2_ref, b2_ref, w3_ref, fr_ref, dc_ref, a_ref, b_ref):
    hp = lax.Precision.HIGHEST
    feats = feat_ref[...]
    fr = fr_ref[...]
    h = jnp.sin(fr * (jnp.dot(feats, w1_ref[...], precision=hp, preferred_element_type=F32) + b1_ref[...]))
    h = jnp.sin(fr * (jnp.dot(h, w2_ref[...], precision=hp, preferred_element_type=F32) + b2_ref[...]))
    h = jnp.dot(h, w3_ref[...], precision=hp, preferred_element_type=F32)
    h = h * jnp.exp(-feats[:, 0:1] * jnp.abs(dc_ref[...]))
    tl = h.shape[0]
    pos = lax.broadcasted_iota(jnp.int32, (tl, GROUP_W), 0) + pl.program_id(1) * tl
    for o in range(2):
        hf = h[:, (2 * o) * GROUP_W:(2 * o + 1) * GROUP_W]
        hb = jnp.where(pos == 0, 0.0, h[:, (2 * o + 1) * GROUP_W:(2 * o + 2) * GROUP_W])
        a_ref[:, o * GROUP_W:(o + 1) * GROUP_W] = (hf + hb).astype(a_ref.dtype)
        b_ref[:, o * GROUP_W:(o + 1) * GROUP_W] = (hb - hf).astype(b_ref.dtype)


def _filters(feats, w1, b1, w2, b2, w3, fr, dc, seq):
    depth = w1.shape[0]
    tl = min(512, seq)
    lay = lambda r, c: pl.BlockSpec((None, r, c), lambda l, i: (l, 0, 0))
    out = pl.BlockSpec((None, tl, 2 * GROUP_W), lambda l, i: (l, i, 0))
    return pl.pallas_call(
        _filter_kernel,
        grid=(depth, seq // tl),
        in_specs=[pl.BlockSpec((tl, LANE), lambda l, i: (i, 0)),
                  lay(LANE, HY_FFN), lay(1, HY_FFN), lay(HY_FFN, HY_FFN), lay(1, HY_FFN),
                  lay(HY_FFN, 4 * GROUP_W), lay(1, HY_FFN), lay(1, 4 * GROUP_W)],
        out_specs=[out, out],
        out_shape=[jax.ShapeDtypeStruct((depth, seq, 2 * GROUP_W), BF16)] * 2,
        compiler_params=_cp("parallel", "parallel"),
        name="hyena_filter",
    )(feats, w1, b1, w2, b2, w3, fr, dc)


def _hyena_classes(a, b):
    re, nim = _butterflies(a, b)
    return re, [nim[0], nim[1], -nim[2], -nim[3]]


def _spectrum_kernel(tab_ref, z0, z1, z2, z3, o_ref):
    a, b = _residue_products(lambda k: tab_ref[k], (z0, z1, z2, z3))
    zr, zs = _hyena_classes(a, b)
    for m in range(RADIX):
        o_ref[m] = zr[m].astype(o_ref.dtype)
        o_ref[RADIX + m] = zs[m].astype(o_ref.dtype)


def _filter_spectrum(tab, filt, seq):
    depth, _, _, cols = filt.shape
    q = seq // RADIX
    tq = min(256, q)
    ncb = cols // GROUP_W
    res = lambda r: pl.BlockSpec((None, None, q, GROUP_W), lambda i, n: (n // ncb, r, 0, n % ncb))
    return pl.pallas_call(
        _spectrum_kernel,
        grid=(q // tq, depth * ncb),
        in_specs=[pl.BlockSpec((2 * RADIX, tq, q), lambda i, n: (0, i, 0)), res(0), res(1), res(2), res(3)],
        out_specs=pl.BlockSpec((None, 2 * RADIX, tq, GROUP_W), lambda i, n: (n // ncb, 0, i, n % ncb)),
        out_shape=jax.ShapeDtypeStruct((depth, 2 * RADIX, q, cols), BF16),
        compiler_params=_cp("parallel", "arbitrary"),
        name="filter_spectrum",
    )(tab, filt, filt, filt, filt)


def _conv_spectrum_kernel(tab_ref, z0, z1, z2, z3, hr_ref, hi_ref, p_ref, ab_ref, *, lag):
    def finish(slot):
        def group(rows, cols):
            def run():
                zr, zs = _hyena_classes(*_load_products(ab_ref, slot, rows, cols))
                gr, gi = [], []
                for m in range(RADIX):
                    hr, hi = hr_ref[m, rows, cols].astype(F32), hi_ref[m, rows, cols].astype(F32)
                    gr.append(zr[m] * hr + zs[m] * hi)
                    gi.append(zr[m] * hi - zs[m] * hr)
                s02, d02, s13, d13 = gr[0] + gr[2], gr[0] - gr[2], gr[1] + gr[3], gr[1] - gr[3]
                t02, e02, t13, e13 = gi[0] + gi[2], gi[0] - gi[2], gi[1] + gi[3], gi[1] - gi[3]
                pc = [s02 + s13, d02 - t13, s02 - s13, d02 + t13]
                ps = [-e02 - e13, -t02 - d13, e13 - e02, d13 - t02]
                for r in range(RADIX):
                    p_ref[r, rows, cols] = pc[r].astype(p_ref.dtype)
                    p_ref[RADIX + r, rows, cols] = ps[r].astype(p_ref.dtype)
            return run

        return [group(rows, cols) for rows, cols in _groups(ab_ref.shape[2], ab_ref.shape[3])]

    tab = _table_rows(tab_ref, lag.cur(pl.program_id(0))[1], ab_ref.shape[2])
    lag.run(lambda slot: _product_parts(tab, (z0, z1, z2, z3), ab_ref, slot), finish)


def _conv_spectrum(tab, z, h_cos, h_sin, layer, order, bsz, seq):
    q = seq // RADIX
    tq = min(256, q)
    lag = _Lagged(bsz, q // tq)
    res = lambda r: pl.BlockSpec((None, None, q, GROUP_W), lambda t: (lag.cur(t)[0], r, 0, 0))
    filt = lambda half: pl.BlockSpec((None, RADIX, tq, GROUP_W), lambda t: (layer, half, lag.prev(t)[1], order))
    return pl.pallas_call(
        functools.partial(_conv_spectrum_kernel, lag=lag),
        grid=(lag.total + 1,),
        in_specs=[_resident((2 * RADIX, q, q), lambda t: (0, 0, 0)), res(0), res(1), res(2), res(3),
                  filt(0), filt(1)],
        out_specs=pl.BlockSpec((None, 2 * RADIX, tq, GROUP_W), lambda t: (lag.prev(t)[0], 0, lag.prev(t)[1], 0)),
        out_shape=jax.ShapeDtypeStruct((bsz, 2 * RADIX, q, GROUP_W), BF16),
        scratch_shapes=[pltpu.VMEM((2, 2 * RADIX, tq, GROUP_W), F32)],
        compiler_params=_cp("arbitrary"),
        name="hyena_spectrum",
    )(tab, z, z, z, z, h_cos, h_sin)


def _inverse_kernel(tab_ref, p_ref, xm_ref, w_ref, skip_ref, gn_ref, o_ref, *scratch, inv_len, final):
    tj = xm_ref.shape[1]
    tab = _table_rows(tab_ref, pl.program_id(1), tj)
    for r in range(RADIX):
        y = jnp.dot(tab(r), p_ref[r], preferred_element_type=F32)
        y = y + jnp.dot(tab(RADIX + r), p_ref[RADIX + r], preferred_element_type=F32)
        out = xm_ref[r].astype(F32) * (y * inv_len + w_ref[r].astype(F32) * skip_ref[...])
        if final:
            out = _rms(out, gn_ref[...])
            for c in range(GROUP_W // LANE):
                scratch[0][c, pl.ds(r, tj, stride=RADIX), :] = out[:, c * LANE:(c + 1) * LANE]
        else:
            o_ref[r] = out.astype(o_ref.dtype)
    if final:
        for c in range(GROUP_W // LANE):
            o_ref[:, c * LANE:(c + 1) * LANE] = scratch[0][c].astype(o_ref.dtype)


def _inverse(tab_t, pcs, xm, w, skip, gn, bsz, seq, final):
    q = seq // RADIX
    tj = min(256, q)
    nj = q // tj
    tile = pl.BlockSpec((None, RADIX, tj, GROUP_W), lambda b, i: (b, 0, i, 0))
    vec = pl.BlockSpec((1, GROUP_W), lambda b, i: (0, 0))
    if final:
        out_spec = pl.BlockSpec((RADIX * tj, GROUP_W), lambda b, i: (b * nj + i, 0))
        out_shape = jax.ShapeDtypeStruct((bsz * seq, GROUP_W), BF16)
        scratch = [pltpu.VMEM((GROUP_W // LANE, RADIX * tj, LANE), F32)]
    else:
        out_spec, out_shape, scratch = tile, jax.ShapeDtypeStruct((bsz, RADIX, q, GROUP_W), BF16), []
    return pl.pallas_call(
        functools.partial(_inverse_kernel, inv_len=1.0 / seq, final=final),
        grid=(bsz, nj),
        in_specs=[_resident((2 * RADIX, q, q), lambda b, i: (0, 0, 0)),
                  pl.BlockSpec((None, 2 * RADIX, q, GROUP_W), lambda b, i: (b, 0, 0, 0)),
                  tile, tile, vec, vec],
        out_specs=out_spec,
        out_shape=out_shape,
        scratch_shapes=scratch,
        compiler_params=_cp("parallel", "arbitrary"),
        name="hyena_inverse",
    )(tab_t, pcs, xm, w, skip, gn)


def _out_kernel(ya_ref, yb_ref, yc_ref, yd_ref, w_ref, x_ref, o_ref):
    acc = x_ref[...]
    for g, y_ref in enumerate((ya_ref, yb_ref, yc_ref, yd_ref)):
        acc = acc + jnp.dot(y_ref[...], w_ref[g * GROUP_W:(g + 1) * GROUP_W, :], preferred_element_type=F32)
    o_ref[...] = acc


def _out_proj(ys, w, x):
    m, n = x.shape
    tm = min(512, m)
    piece = pl.BlockSpec((tm, GROUP_W), lambda i: (i, 0))
    xs = pl.BlockSpec((tm, n), lambda i: (i, 0))
    return pl.pallas_call(
        _out_kernel,
        grid=(m // tm,),
        in_specs=[piece, piece, piece, piece, _resident(w.shape, lambda i: (0, 0)), xs],
        out_specs=xs,
        out_shape=jax.ShapeDtypeStruct((m, n), F32),
        compiler_params=_cp("parallel"),
        name="out_proj",
    )(*ys, w, x)


def _attn_kernel(q_ref, k_ref, v_ref, wo_ref, x_ref, o_ref, att_ref):
    heads = [slice(h * XA_HD, (h + 1) * XA_HD) for h in range(XA_HEADS)]
    scores = [lax.dot_general(q_ref[:, hs], k_ref[:, hs], NT, preferred_element_type=F32) for hs in heads]
    for hs, s in zip(heads, scores):
        s = s * (XA_HD ** -0.5)
        e = jnp.exp(s - jnp.max(s, axis=-1, keepdims=True))
        p = e / jnp.sum(e, axis=-1, keepdims=True)
        att_ref[:, hs] = jnp.dot(p.astype(BF16), v_ref[:, hs], preferred_element_type=F32).astype(BF16)
    o_ref[...] = x_ref[...] + jnp.dot(att_ref[...], wo_ref[...], preferred_element_type=F32)


def _attn(q, kv, wo, x, seq):
    m, d = x.shape
    tm = min(512, seq)
    per = seq // tm
    xs = pl.BlockSpec((tm, d), lambda i: (i, 0))
    return pl.pallas_call(
        _attn_kernel,
        grid=(m // tm,),
        in_specs=[xs,
                  pl.BlockSpec((N_MEM, d), lambda i: (i // per, 0)),
                  pl.BlockSpec((N_MEM, d), lambda i: (i // per, 1)),
                  _resident((d, d), lambda i: (0, 0)),
                  xs],
        out_specs=xs,
        out_shape=jax.ShapeDtypeStruct((m, d), F32),
        scratch_shapes=[pltpu.VMEM((tm, d), BF16)],
        compiler_params=_cp("parallel"),
        name="cross_attn",
    )(q, kv, kv, wo, x)


def _swiglu_kernel(x_hbm, g_ref, wg_ref, wu_ref, wd_ref, fg_ref, o_ref, x_ref, hn_ref, sem, *, final, row_tiles):
    i, f = pl.program_id(0), pl.program_id(1)
    tm = x_ref.shape[0]

    def x_copy(tile):
        return pltpu.make_async_copy(x_hbm.at[pl.ds(pl.multiple_of(tile * tm, tm), tm), :], x_ref, sem)

    @pl.when((i == 0) & (f == 0))
    def _():
        x_copy(0).start()

    @pl.when(f == 0)
    def _():
        x_copy(i).wait()
        _norm_rows_to(x_ref, g_ref, hn_ref)
        o_ref[...] = x_ref[...]

    @pl.when((f == 1) & (i + 1 < row_tiles))
    def _():
        x_copy(i + 1).start()

    hn = hn_ref[...]
    gate = jnp.dot(hn, wg_ref[...], preferred_element_type=F32)
    up = jnp.dot(hn, wu_ref[...], preferred_element_type=F32)
    act = (gate * jax.nn.sigmoid(gate) * up).astype(BF16)
    o_ref[...] += jnp.dot(act, wd_ref[...], preferred_element_type=F32)

    if final:
        @pl.when(f == pl.num_programs(1) - 1)
        def _():
            _norm_rows_to(o_ref, fg_ref, o_ref)


def _swiglu(x, g, w_gu, w_d, final_g, final):
    m, d = x.shape
    dff = w_d.shape[0]
    tm, tf = min(1024, m), 512
    nf = dff // tf
    assert nf >= 2
    vec = pl.BlockSpec((1, d), lambda i, f: (0, 0))
    return pl.pallas_call(
        functools.partial(_swiglu_kernel, final=final, row_tiles=m // tm),
        grid=(m // tm, nf),
        in_specs=[pl.BlockSpec(memory_space=pl.ANY), vec,
                  pl.BlockSpec((d, tf), lambda i, f: (0, f)),
                  pl.BlockSpec((d, tf), lambda i, f: (0, nf + f)),
                  pl.BlockSpec((tf, d), lambda i, f: (f, 0)),
                  vec],
        out_specs=pl.BlockSpec((tm, d), lambda i, f: (i, 0)),
        out_shape=jax.ShapeDtypeStruct((m, d), F32),
        scratch_shapes=[pltpu.VMEM((tm, d), F32), pltpu.VMEM((tm, d), BF16), pltpu.SemaphoreType.DMA(())],
        compiler_params=_cp("arbitrary", "arbitrary"),
        name="swiglu",
    )(x, g.reshape(1, d), w_gu, w_gu, w_d, final_g.reshape(1, d))


def _angles(rows, cols, period):
    m = (rows[..., :, None] * cols[..., None, :]) % period
    return m.astype(F32) * (2.0 * math.pi / period)


def _cos_sin(ang, sin_sign=1.0):
    return jnp.concatenate([jnp.cos(ang), sin_sign * jnp.sin(ang)], axis=0).astype(BF16)


def _tables(seq):
    q = seq // RADIX
    base = jnp.arange(q, dtype=jnp.int32)
    times = RADIX * base[None, :] + jnp.arange(RADIX, dtype=jnp.int32)[:, None]
    rep = jnp.broadcast_to(base, (RADIX, q))
    fnet_tab = _cos_sin(_angles(rep, times, seq))
    ch = jnp.arange(GROUP_W, dtype=jnp.int32)
    same = (ch[:, None] // FNET_CH) == (ch[None, :] // FNET_CH)
    angc = _angles(ch % FNET_CH, ch % FNET_CH, FNET_CH)
    cc = jnp.where(same, jnp.cos(angc), 0.0).astype(BF16)
    sc = jnp.where(same, jnp.sin(angc), 0.0).astype(BF16)
    hy_tab = _cos_sin(_angles(2 * rep + 1, times, 4 * seq))
    hy_tab_t = jnp.swapaxes(hy_tab, 1, 2)
    return fnet_tab, cc, sc, hy_tab, hy_tab_t


def _position_features(seq):
    pos = jnp.arange(seq, dtype=F32)
    t = pos / seq
    f = jnp.linspace(1e-4, HY_BANDS - 1, HY_BANDS, dtype=F32)
    ang = (2.0 * math.pi * t)[:, None] * f[None, :]
    feats = jnp.concatenate([t[:, None], jnp.cos(ang), -jnp.sin(ang)], axis=-1)
    return jnp.pad(feats, ((0, 0), (0, LANE - HY_EMB)))


def kernel(x, mem, norm_g, w_in, gla_gk_w, gla_gk_b, gla_norm_g, hy_conv_w, hy_ffn_w1, hy_ffn_b1, hy_ffn_w2, hy_ffn_b2, hy_ffn_w3, hy_sin_freq, hy_decay, hy_skip, sc_conv_w, grp_norm_g, w_out, mem_norm_g, w_xq, w_xkv, w_xo, w_gate_up, w_down, final_norm_g):
    bsz, seq, d = x.shape
    depth = norm_g.shape[0]
    m = bsz * seq
    assert d == D_MODEL and seq % (RADIX * GLA_CHUNK) == 0 and mem.shape[1] == N_MEM

    q_w, k_w, v_w, g_w, lr_w, uf_w, uh_w, us_w = jnp.split(
        w_in.astype(BF16), [256, 512, 1024, 1536, 1568, 2080, 3616], axis=-1)
    w_p = jnp.concatenate([uh_w, us_w, q_w, k_w, v_w, g_w, uf_w], axis=-1)
    w_lr = jnp.pad(lr_w, ((0, 0), (0, 0), (0, LANE - 2 * GLA_RANK)))
    wgf = jnp.pad(gla_gk_w[:, 0], ((0, 0), (0, LANE - GLA_RANK), (0, 0))).astype(BF16)
    wgb = jnp.pad(gla_gk_w[:, 1], ((0, 0), (GLA_RANK, LANE - 2 * GLA_RANK), (0, 0))).astype(BF16)
    gkb = gla_gk_b.astype(F32)
    w_out_b, w_xq_b, w_xkv_b, w_xo_b = (t.astype(BF16) for t in (w_out, w_xq, w_xkv, w_xo))
    w_gu_b, w_d_b = w_gate_up.astype(BF16), w_down.astype(BF16)
    fnet_tab, cc, sc, hy_tab, hy_tab_t = _tables(seq)
    qlen = seq // RADIX
    feats = _position_features(seq).reshape(qlen, RADIX, LANE).transpose(1, 0, 2).reshape(seq, LANE)

    w1 = jnp.pad(hy_ffn_w1.astype(F32), ((0, 0), (0, LANE - HY_EMB), (0, 0)))
    fa, fb = _filters(feats, w1, hy_ffn_b1.reshape(depth, 1, HY_FFN).astype(F32), hy_ffn_w2.astype(F32),
                      hy_ffn_b2.reshape(depth, 1, HY_FFN).astype(F32), hy_ffn_w3.astype(F32),
                      hy_sin_freq.reshape(depth, 1, HY_FFN).astype(F32),
                      hy_decay.reshape(depth, 1, 4 * GROUP_W).astype(F32), seq)
    res_major = lambda t: t.reshape(depth, RADIX, qlen, t.shape[-1])
    h_cos = _filter_spectrum(hy_tab, res_major(fa), seq)
    h_sin = _filter_spectrum(hy_tab, res_major(fb), seq)

    mem2 = mem.reshape(bsz * N_MEM, d)
    xf = x.reshape(m, d)
    for l in range(depth):
        p, lr = _norm_matmul(xf, norm_g[l, 0], w_p[l], 1024, 1280, w_side=w_lr[l])
        y_a = _gla(p, lr, wgf[l], wgb[l], gkb[l, 0:1], gkb[l, 1:2], gla_norm_g[l].reshape(1, GLA_DV).astype(F32),
                   bsz, seq)
        hv, hx1, hx2, uf, y_d = _short(p, hy_conv_w[l].astype(F32), sc_conv_w[l].astype(F32),
                                       grp_norm_g[l, 2:3].astype(F32), bsz, seq)
        y_b = _fnet(uf, fnet_tab, cc, sc, grp_norm_g[l, 0:1].astype(F32), bsz, seq)
        gn_c = grp_norm_g[l, 1:2].astype(F32)
        pcs = _conv_spectrum(hy_tab, hv, h_cos, h_sin, l, 0, bsz, seq)
        z = _inverse(hy_tab_t, pcs, hx1, hv, hy_skip[l, 0:1].astype(F32), gn_c, bsz, seq, final=False)
        pcs = _conv_spectrum(hy_tab, z, h_cos, h_sin, l, 1, bsz, seq)
        y_c = _inverse(hy_tab_t, pcs, hx2, z, hy_skip[l, 1:2].astype(F32), gn_c, bsz, seq, final=True)
        xf = _out_proj((y_a, y_b, y_c, y_d), w_out_b[l], xf)
        q = _norm_matmul(xf, norm_g[l, 1], w_xq_b[l], 1024, D_MODEL)
        kv = _norm_matmul(mem2, mem_norm_g, w_xkv_b[l], 1024, 1024)
        xf = _attn(q, kv, w_xo_b[l], xf, seq)
        xf = _swiglu(xf, norm_g[l, 2], w_gu_b[l], w_d_b[l], final_norm_g, final=(l == depth - 1))
    return xf.reshape(bsz, seq, d)
```

```python
import functools
import math

import jax
import jax.numpy as jnp
from jax import lax
from jax.experimental import pallas as pl
from jax.experimental.pallas import tpu as pltpu

F32 = jnp.float32
BF16 = jnp.bfloat16
EPS = 1e-6

D_MODEL = 2048
GROUP_W = 512
GLA_HEADS = 4
GLA_DK = 64
GLA_DV = 128
GLA_RANK = 16
GLA_GATE_NORM = 16.0
GLA_CHUNK = 64
FNET_CH = 128
HY_BANDS = 16
HY_EMB = 2 * HY_BANDS + 1
HY_FFN = 64
XA_HEADS = 4
XA_HD = D_MODEL // XA_HEADS
N_MEM = 256
D_FF = 5632
RADIX = 4
DFT_ROWS = 512
LANE = 128
BF16_ROWS = 16
VMEM_LIMIT = 56 * 1024 * 1024

GLA_QK = GLA_HEADS * GLA_DK
P_UH, P_US, P_Q, P_K, P_V, P_G, P_UF = 0, 1536, 3072, 3328, 3584, 4096, 4608
P_W = 5120

NT = (((1,), (1,)), ((), ()))
TN = (((0,), (0,)), ((), ()))


def _cp(*sem):
    return pltpu.CompilerParams(dimension_semantics=sem, vmem_limit_bytes=VMEM_LIMIT)


def _resident(shape, index_map):
    return pl.BlockSpec(shape, index_map, pipeline_mode=pl.Buffered(1))


def _rms(x, g):
    ms = jnp.mean(x * x, axis=-1, keepdims=True)
    return x * lax.rsqrt(ms + EPS) * g


def _norm_rows_to(x_ref, g_ref, hn_ref):
    rows = min(256, x_ref.shape[0])
    n = x_ref.shape[0] // rows

    def body(i, c):
        r = pl.ds(pl.multiple_of(i * rows, rows), rows)
        hn_ref[r, :] = _rms(x_ref[r, :], g_ref[...]).astype(hn_ref.dtype)
        return c

    lax.fori_loop(0, n, body, 0)


def _norm_matmul_kernel(x_ref, g_ref, w_ref, *rest, with_side, row_tiles, col_tiles):
    if with_side:
        w2_ref, o_ref, o2_ref, hn_ref = rest
    else:
        o_ref, hn_ref = rest
    i, j = pl.program_id(0), pl.program_id(1)
    part = x_ref.shape[0] // col_tiles

    def norm_part():
        g = g_ref[...]
        for r in range(0, part, BF16_ROWS):
            rows = pl.ds(pl.multiple_of(j * part + r, BF16_ROWS), BF16_ROWS)
            hn_ref[i % 2, rows, :] = _rms(x_ref[rows, :], g).astype(BF16)

    def matmul():
        o_ref[...] = jnp.dot(hn_ref[(i + 1) % 2], w_ref[...], preferred_element_type=F32).astype(o_ref.dtype)

    @pl.when(i == 0)
    def _():
        norm_part()

    @pl.when((i > 0) & (i < row_tiles))
    def _():
        matmul()
        norm_part()

    @pl.when(i == row_tiles)
    def _():
        matmul()

    if with_side:
        @pl.when((i > 0) & (j == 0))
        def _():
            o2_ref[...] = jnp.dot(hn_ref[(i + 1) % 2], w2_ref[...], preferred_element_type=F32)


def _norm_matmul(x, g, w, tm, tn, w_side=None):
    m, k = x.shape
    n = w.shape[1]
    tm, tn = min(tm, m), min(tn, n)
    ni, nj = m // tm, n // tn
    assert (tm // nj) % BF16_ROWS == 0
    prev = lambda i: jnp.maximum(i - 1, 0)
    in_specs = [
        pl.BlockSpec((tm, k), lambda i, j: (jnp.minimum(i, ni - 1), 0)),
        pl.BlockSpec((1, k), lambda i, j: (0, 0)),
        _resident((k, tn), lambda i, j: (0, j)) if tn == n else pl.BlockSpec((k, tn), lambda i, j: (0, j)),
    ]
    out_shape = [jax.ShapeDtypeStruct((m, n), BF16)]
    out_specs = [pl.BlockSpec((tm, tn), lambda i, j: (prev(i), jnp.where(i == 0, 0, j)))]
    args = [x, g.reshape(1, k), w]
    if w_side is not None:
        n2 = w_side.shape[1]
        in_specs.append(pl.BlockSpec((k, n2), lambda i, j: (0, 0)))
        out_shape.append(jax.ShapeDtypeStruct((m, n2), F32))
        out_specs.append(pl.BlockSpec((tm, n2), lambda i, j: (prev(i), 0)))
        args.append(w_side)
    res = pl.pallas_call(
        functools.partial(_norm_matmul_kernel, with_side=w_side is not None, row_tiles=ni, col_tiles=nj),
        grid=(ni + 1, nj),
        in_specs=in_specs,
        out_specs=out_specs,
        out_shape=out_shape,
        scratch_shapes=[pltpu.VMEM((2, tm, k), BF16)],
        compiler_params=_cp("arbitrary", "arbitrary"),
        name="norm_matmul",
    )(*args)
    return res if w_side is not None else res[0]


def _gla_pass(fwd, blk, q_ref, k_ref, v_ref, lr_ref, g_ref, wg_ref, bg_ref, gn_ref, y_ref, st_ref, ob_ref):
    ns, ts = q_ref.shape[0], q_ref.shape[1]
    c_sz = GLA_CHUNK
    nc = ts // c_sz
    ii = lax.broadcasted_iota(jnp.int32, (c_sz, c_sz), 0)
    jj = lax.broadcasted_iota(jnp.int32, (c_sz, c_sz), 1)
    incl = (jj <= ii) if fwd else (jj >= ii)
    mask = incl if fwd else (jj > ii)
    tri = jnp.where(incl, 1.0, 0.0).astype(BF16)
    gn = gn_ref[...]
    chunk = lambda c: slice(c * c_sz, (c + 1) * c_sz)
    order = list(range(nc)) if fwd else list(reversed(range(nc)))
    problems = [(s, h) for s in range(ns) for h in range(GLA_HEADS)]
    pair = lambda h: slice((h // 2) * LANE, (h // 2 + 1) * LANE)
    val = lambda h: slice(h * GLA_DV, (h + 1) * GLA_DV)
    first = lax.broadcasted_iota(jnp.int32, (c_sz, LANE), 1) < GLA_DK
    own = lambda h: first if h % 2 == 0 else jnp.logical_not(first)

    qe, ke, qi, ku, dec = {}, {}, {}, {}, {}
    for s in range(ns):
        x = jnp.dot(lr_ref[s].astype(BF16), wg_ref[...], preferred_element_type=F32) + bg_ref[...]
        gk = (jnp.minimum(x, 0.0) - jnp.log1p(jnp.exp(-jnp.abs(x)))) * (1.0 / GLA_GATE_NORM)
        hi = gk.astype(BF16)
        lo = (gk - hi.astype(F32)).astype(BF16)
        for c in range(nc):
            r = chunk(c)
            b = jnp.dot(tri, hi[r], preferred_element_type=F32) + jnp.dot(tri, lo[r], preferred_element_type=F32)
            bref = b[c_sz // 2:c_sz // 2 + 1] if fwd else b[c_sz // 2 - 1:c_sz // 2]
            btot = b[c_sz - 1:c_sz] if fwd else b[0:1]
            qe_f = q_ref[s, r, :].astype(F32) * (GLA_DK ** -0.5) * jnp.exp(b - bref)
            ke_f = k_ref[s, r, :].astype(F32) * jnp.exp(bref - b)
            qe[s, c] = qe_f
            ke[s, c] = ke_f.astype(BF16)
            qi[s, c] = (qe_f * jnp.exp(bref)).astype(BF16)
            ku[s, c] = ke_f * jnp.exp(btot - bref)
            dec[s, c] = jnp.exp(btot)

    kv = {}
    for c in order:
        for s, h in problems:
            kv[s, h, c] = lax.dot_general(v_ref[s, chunk(c), val(h)],
                                          jnp.where(own(h), ku[s, c][:, pair(h)], 0.0).astype(BF16), TN,
                                          preferred_element_type=F32)
    st = {(s, h): st_ref[s, h] for s, h in problems}
    st_in = {}
    for c in order:
        for s, h in problems:
            st_in[s, h, c] = st[s, h].astype(BF16)
            st[s, h] = st[s, h] * dec[s, c][:, pair(h)] + kv[s, h, c]
    for s, h in problems:
        st_ref[s, h] = st[s, h]
    for c in order:
        r = chunk(c)
        rows = pl.ds(pl.multiple_of(blk * ts + c * c_sz, c_sz), c_sz)
        scores = {}
        for s, h in problems:
            sc = lax.dot_general(jnp.where(own(h), qe[s, c][:, pair(h)], 0.0).astype(BF16), ke[s, c][:, pair(h)], NT,
                                 preferred_element_type=F32)
            scores[s, h] = jnp.where(mask, sc, 0.0).astype(BF16)
        for s, h in problems:
            o = jnp.dot(scores[s, h], v_ref[s, r, val(h)], preferred_element_type=F32)
            o = o + lax.dot_general(qi[s, c][:, pair(h)], st_in[s, h, c], NT, preferred_element_type=F32)
            if fwd:
                tot = o + ob_ref[s, rows, val(h)]
                gate = g_ref[s, r, val(h)].astype(F32)
                y_ref[s, r, val(h)] = (_rms(tot, gn) * (gate * jax.nn.sigmoid(gate))).astype(y_ref.dtype)
            else:
                ob_ref[s, rows, val(h)] = o


def _gla_kernel(q_ref, k_ref, v_ref, lr_ref, g_ref, wgf_ref, wgb_ref, bgf_ref, bgb_ref, gn_ref,
                y_ref, st_ref, ob_ref, *, nb):
    n = pl.program_id(1)

    @pl.when((n == 0) | (n == nb))
    def _():
        st_ref[...] = jnp.zeros_like(st_ref)

    @pl.when(n < nb)
    def _():
        _gla_pass(False, nb - 1 - n, q_ref, k_ref, v_ref, lr_ref, g_ref, wgb_ref, bgb_ref, gn_ref, y_ref, st_ref, ob_ref)

    @pl.when(n >= nb)
    def _():
        _gla_pass(True, n - nb, q_ref, k_ref, v_ref, lr_ref, g_ref, wgf_ref, bgf_ref, gn_ref, y_ref, st_ref, ob_ref)


def _gla(p, lr, wgf, wgb, bgf, bgb, gn, bsz, seq):
    ts = min(512, seq)
    nb = seq // ts
    ns = 2 if bsz % 2 == 0 else 1
    blk = lambda n: jnp.where(n < nb, nb - 1 - n, n - nb)
    oblk = lambda n: jnp.where(n < nb, 0, n - nb)
    in_specs = [
        pl.BlockSpec((ns, ts, GLA_QK), lambda b, n: (b, blk(n), P_Q // GLA_QK)),
        pl.BlockSpec((ns, ts, GLA_QK), lambda b, n: (b, blk(n), P_K // GLA_QK)),
        pl.BlockSpec((ns, ts, GROUP_W), lambda b, n: (b, blk(n), P_V // GROUP_W)),
        pl.BlockSpec((ns, ts, LANE), lambda b, n: (b, blk(n), 0)),
        pl.BlockSpec((ns, ts, GROUP_W), lambda b, n: (b, oblk(n), P_G // GROUP_W)),
        pl.BlockSpec((LANE, GLA_QK), lambda b, n: (0, 0)),
        pl.BlockSpec((LANE, GLA_QK), lambda b, n: (0, 0)),
        pl.BlockSpec((1, GLA_QK), lambda b, n: (0, 0)),
        pl.BlockSpec((1, GLA_QK), lambda b, n: (0, 0)),
        pl.BlockSpec((1, GLA_DV), lambda b, n: (0, 0)),
    ]
    p3 = p.reshape(bsz, seq, P_W)
    y = pl.pallas_call(
        functools.partial(_gla_kernel, nb=nb),
        grid=(bsz // ns, 2 * nb),
        in_specs=in_specs,
        out_specs=pl.BlockSpec((ns, ts, GROUP_W), lambda b, n: (b, oblk(n), 0)),
        out_shape=jax.ShapeDtypeStruct((bsz, seq, GROUP_W), BF16),
        scratch_shapes=[pltpu.VMEM((ns, GLA_HEADS, GLA_DV, LANE), F32), pltpu.VMEM((ns, seq, GROUP_W), F32)],
        compiler_params=_cp("parallel", "arbitrary"),
        name="gla",
    )(p3, p3, p3, lr.reshape(bsz, seq, LANE), p3, wgf, wgb, bgf, bgb, gn)
    return y.reshape(bsz * seq, GROUP_W)


def _table_rows(tab_ref, tile, rows):
    window = pl.ds(pl.multiple_of(tile * rows, rows), rows)
    return lambda k: tab_ref[k, window, :]


def _residue_products(tab, z_refs):
    a = [jnp.dot(tab(r), z_refs[r][...], preferred_element_type=F32) for r in range(RADIX)]
    b = [jnp.dot(tab(RADIX + r), z_refs[r][...], preferred_element_type=F32) for r in range(RADIX)]
    return a, b


def _butterflies(a, b):
    a02p, a02m, a13p, a13m = a[0] + a[2], a[0] - a[2], a[1] + a[3], a[1] - a[3]
    b02p, b02m, b13p, b13m = b[0] + b[2], b[0] - b[2], b[1] + b[3], b[1] - b[3]
    re = [a02p + a13p, a02m - b13m, a02p - a13p, a02m + b13m]
    nim = [b02p + b13p, b02m + a13m, b02p - b13p, b02m - a13m]
    return re, nim


def _stage_products(tab, z_refs, ab_ref):
    for k in range(2 * RADIX):
        ab_ref[k] = jnp.dot(tab(k), z_refs[k % RADIX][...], preferred_element_type=F32)


def _load_products(ab_ref, rows, cols):
    return ([ab_ref[r, rows, cols] for r in range(RADIX)], [ab_ref[RADIX + r, rows, cols] for r in range(RADIX)])


def _groups(n_rows, n_cols, cols_per_group=2 * LANE):
    return [(slice(r, r + BF16_ROWS), slice(c, c + cols_per_group))
            for r in range(0, n_rows, BF16_ROWS) for c in range(0, n_cols, cols_per_group)]


def _fnet_kernel(tab_ref, x0, x1, x2, x3, cc_ref, sc_ref, gn_ref, y_ref, ab_ref, bf_ref, *, scale):
    tq = y_ref.shape[1]
    _stage_products(_table_rows(tab_ref, pl.program_id(1), tq), (x0, x1, x2, x3), ab_ref)
    for rows, cols in _groups(tq, GROUP_W):
        re, nim = _butterflies(*_load_products(ab_ref, rows, cols))
        for m in range(RADIX):
            bf_ref[m, rows, cols] = re[m].astype(BF16)
            bf_ref[RADIX + m, rows, cols] = nim[m].astype(BF16)
    for m in range(RADIX):
        y = jnp.dot(bf_ref[m], cc_ref[...], preferred_element_type=F32)
        y = y - jnp.dot(bf_ref[RADIX + m], sc_ref[...], preferred_element_type=F32)
        y_ref[m] = _rms(y * scale, gn_ref[...]).astype(y_ref.dtype)


def _fnet(uf, tab, cc, sc, gn, bsz, seq):
    q = seq // RADIX
    tq = min(DFT_ROWS, q)
    nq = q // tq
    res = lambda r: pl.BlockSpec((None, None, q, GROUP_W), lambda b, i: (b, r, 0, 0))
    sq = pl.BlockSpec((GROUP_W, GROUP_W), lambda b, i: (0, 0))
    out = pl.pallas_call(
        functools.partial(_fnet_kernel, scale=1.0 / math.sqrt(seq * FNET_CH)),
        grid=(bsz, nq),
        in_specs=[_resident((2 * RADIX, q, q), lambda b, i: (0, 0, 0)),
                  res(0), res(1), res(2), res(3), sq, sq,
                  pl.BlockSpec((1, GROUP_W), lambda b, i: (0, 0))],
        out_specs=pl.BlockSpec((None, RADIX, tq, GROUP_W), lambda b, i: (b, 0, i, 0)),
        out_shape=jax.ShapeDtypeStruct((bsz, RADIX, q, GROUP_W), BF16),
        scratch_shapes=[pltpu.VMEM((2 * RADIX, tq, GROUP_W), F32), pltpu.VMEM((2 * RADIX, tq, GROUP_W), BF16)],
        compiler_params=_cp("parallel", "arbitrary"),
        name="fnet",
    )(tab, uf, uf, uf, uf, cc, sc, gn)
    return out.reshape(bsz * seq, GROUP_W)


def _conv3(main, prev_row, next_row, w):
    ts = main.shape[0]
    rid = lax.broadcasted_iota(jnp.int32, main.shape, 0)
    up = jnp.where(rid == 0, prev_row, pltpu.roll(main, 1, 0))
    dn = jnp.where(rid == ts - 1, next_row, pltpu.roll(main, ts - 1, 0))
    return up * w[0:1] + main * w[1:2] + dn * w[2:3]


def _short_kernel(hm_ref, hp_ref, hn_ref, sm_ref, sp_ref, sn_ref, uf_ref, hw_ref, sw_ref, gn_ref,
                  v_ref, x1_ref, x2_ref, ufr_ref, yd_ref, nat_ref, *, nblk):
    i = pl.program_id(0)
    keep_prev = jnp.where(i % nblk == 0, 0.0, 1.0)
    keep_next = jnp.where(i % nblk == nblk - 1, 0.0, 1.0)
    last = BF16_ROWS - 1
    w = GROUP_W
    ts = hm_ref.shape[0]
    per = w // LANE

    def neighbours(ref, halo_prev, halo_next, r0, rows, cols):
        above = (ref[r0 - 1:r0, cols].astype(F32) if r0 > 0
                 else halo_prev[last:last + 1, cols].astype(F32) * keep_prev)
        below = (ref[r0 + rows:r0 + rows + 1, cols].astype(F32) if r0 + rows < ts
                 else halo_next[0:1, cols].astype(F32) * keep_next)
        return above, below

    rows_h, lanes_h = 2 * BF16_ROWS, 2 * LANE
    for r0 in range(0, ts, rows_h):
        for c0 in range(0, 3 * w, lanes_h):
            cols = slice(c0, c0 + lanes_h)
            above, below = neighbours(hm_ref, hp_ref, hn_ref, r0, rows_h, cols)
            conv = _conv3(hm_ref[r0:r0 + rows_h, cols].astype(F32), above, below, hw_ref[:, cols])
            for k in range(lanes_h // LANE):
                nat_ref[c0 // LANE + k, r0:r0 + rows_h, :] = conv[:, k * LANE:(k + 1) * LANE]
    for c in range(per):
        nat_ref[3 * per + c] = uf_ref[:, c * LANE:(c + 1) * LANE].astype(F32)
    for g, dst in enumerate((v_ref, x1_ref, x2_ref, ufr_ref)):
        for r in range(RADIX):
            for c in range(per):
                piece = nat_ref[g * per + c, pl.ds(r, ts // RADIX, stride=RADIX), :]
                dst[r, :, c * LANE:(c + 1) * LANE] = piece.astype(dst.dtype)

    bcol, ccol, hcol = slice(0, w), slice(w, 2 * w), slice(2 * w, 3 * w)
    rows_s = BF16_ROWS
    for r0 in range(0, ts, rows_s):
        rs = slice(r0, r0 + rows_s)
        ca, cb = neighbours(sm_ref, sp_ref, sn_ref, r0, rows_s, ccol)
        ha, hb = neighbours(sm_ref, sp_ref, sn_ref, r0, rows_s, hcol)
        prod = sm_ref[rs, ccol].astype(F32) * sm_ref[rs, hcol].astype(F32)
        conv = _conv3(prod, ca * ha, cb * hb, sw_ref[...])
        yd_ref[rs, :] = _rms(sm_ref[rs, bcol].astype(F32) * conv, gn_ref[...]).astype(yd_ref.dtype)


def _short(p, hy_w, sc_w, gn, bsz, seq):
    m = bsz * seq
    ts = min(256, seq)
    nblk = seq // ts
    hb = ts // BF16_ROWS
    nh = m // BF16_ROWS
    wide = 3 * GROUP_W
    main = lambda cb: pl.BlockSpec((ts, wide), lambda i: (i, cb))
    prev = lambda cb: pl.BlockSpec((BF16_ROWS, wide), lambda i: (jnp.maximum(i * hb - 1, 0), cb))
    nxt = lambda cb: pl.BlockSpec((BF16_ROWS, wide), lambda i: (jnp.minimum((i + 1) * hb, nh - 1), cb))
    nat = pl.BlockSpec((ts, GROUP_W), lambda i: (i, 0))
    res = pl.BlockSpec((None, RADIX, ts // RADIX, GROUP_W), lambda i: (i // nblk, 0, i % nblk, 0))
    res_shape = jax.ShapeDtypeStruct((bsz, RADIX, seq // RADIX, GROUP_W), BF16)
    return pl.pallas_call(
        functools.partial(_short_kernel, nblk=nblk),
        grid=(m // ts,),
        in_specs=[main(0), prev(0), nxt(0), main(1), prev(1), nxt(1),
                  pl.BlockSpec((ts, GROUP_W), lambda i: (i, P_UF // GROUP_W)),
                  pl.BlockSpec((3, wide), lambda i: (0, 0)),
                  pl.BlockSpec((3, GROUP_W), lambda i: (0, 0)),
                  pl.BlockSpec((1, GROUP_W), lambda i: (0, 0))],
        out_specs=[res, res, res, res, nat],
        out_shape=[res_shape] * 4 + [jax.ShapeDtypeStruct((m, GROUP_W), BF16)],
        scratch_shapes=[pltpu.VMEM((4 * GROUP_W // LANE, ts, LANE), F32)],
        compiler_params=_cp("parallel"),
        name="short_conv",
    )(p, p, p, p, p, p, p, hy_w, sc_w, gn)


def _filter_kernel(feat_ref, w1_ref, b1_ref, w2_ref, b2_ref, w3_ref, fr_ref, dc_ref, a_ref, b_ref):
    hp = lax.Precision.HIGHEST
    feats = feat_ref[...]
    fr = fr_ref[...]
    h = jnp.sin(fr * (jnp.dot(feats, w1_ref[...], precision=hp, preferred_element_type=F32) + b1_ref[...]))
    h = jnp.sin(fr * (jnp.dot(h, w2_ref[...], precision=hp, preferred_element_type=F32) + b2_ref[...]))
    h = jnp.dot(h, w3_ref[...], precision=hp, preferred_element_type=F32)
    h = h * jnp.exp(-feats[:, 0:1] * jnp.abs(dc_ref[...]))
    tl = h.shape[0]
    pos = lax.broadcasted_iota(jnp.int32, (tl, GROUP_W), 0) + pl.program_id(1) * tl
    for o in range(2):
        hf = h[:, (2 * o) * GROUP_W:(2 * o + 1) * GROUP_W]
        hb = jnp.where(pos == 0, 0.0, h[:, (2 * o + 1) * GROUP_W:(2 * o + 2) * GROUP_W])
        a_ref[:, o * GROUP_W:(o + 1) * GROUP_W] = (hf + hb).astype(a_ref.dtype)
        b_ref[:, o * GROUP_W:(o + 1) * GROUP_W] = (hb - hf).astype(b_ref.dtype)


def _filters(feats, w1, b1, w2, b2, w3, fr, dc, seq):
    depth = w1.shape[0]
    tl = min(512, seq)
    lay = lambda r, c: pl.BlockSpec((None, r, c), lambda l, i: (l, 0, 0))
    out = pl.BlockSpec((None, tl, 2 * GROUP_W), lambda l, i: (l, i, 0))
    return pl.pallas_call(
        _filter_kernel,
        grid=(depth, seq // tl),
        in_specs=[pl.BlockSpec((tl, LANE), lambda l, i: (i, 0)),
                  lay(LANE, HY_FFN), lay(1, HY_FFN), lay(HY_FFN, HY_FFN), lay(1, HY_FFN),
                  lay(HY_FFN, 4 * GROUP_W), lay(1, HY_FFN), lay(1, 4 * GROUP_W)],
        out_specs=[out, out],
        out_shape=[jax.ShapeDtypeStruct((depth, seq, 2 * GROUP_W), BF16)] * 2,
        compiler_params=_cp("parallel", "parallel"),
        name="hyena_filter",
    )(feats, w1, b1, w2, b2, w3, fr, dc)


def _hyena_classes(a, b):
    re, nim = _butterflies(a, b)
    return re, [nim[0], nim[1], -nim[2], -nim[3]]


def _spectrum_kernel(tab_ref, z0, z1, z2, z3, o_ref):
    a, b = _residue_products(lambda k: tab_ref[k], (z0, z1, z2, z3))
    zr, zs = _hyena_classes(a, b)
    for m in range(RADIX):
        o_ref[m] = zr[m].astype(o_ref.dtype)
        o_ref[RADIX + m] = zs[m].astype(o_ref.dtype)


def _filter_spectrum(tab, filt, seq):
    depth, _, _, cols = filt.shape
    q = seq // RADIX
    tq = min(256, q)
    ncb = cols // GROUP_W
    res = lambda r: pl.BlockSpec((None, None, q, GROUP_W), lambda i, n: (n // ncb, r, 0, n % ncb))
    return pl.pallas_call(
        _spectrum_kernel,
        grid=(q // tq, depth * ncb),
        in_specs=[pl.BlockSpec((2 * RADIX, tq, q), lambda i, n: (0, i, 0)), res(0), res(1), res(2), res(3)],
        out_specs=pl.BlockSpec((None, 2 * RADIX, tq, GROUP_W), lambda i, n: (n // ncb, 0, i, n % ncb)),
        out_shape=jax.ShapeDtypeStruct((depth, 2 * RADIX, q, cols), BF16),
        compiler_params=_cp("parallel", "arbitrary"),
        name="filter_spectrum",
    )(tab, filt, filt, filt, filt)


def _conv_spectrum_kernel(tab_ref, z0, z1, z2, z3, hr_ref, hi_ref, p_ref, ab_ref):
    tq = p_ref.shape[1]
    _stage_products(_table_rows(tab_ref, pl.program_id(1), tq), (z0, z1, z2, z3), ab_ref)
    for rows, cols in _groups(tq, GROUP_W):
        zr, zs = _hyena_classes(*_load_products(ab_ref, rows, cols))
        gr, gi = [], []
        for m in range(RADIX):
            hr, hi = hr_ref[m, rows, cols].astype(F32), hi_ref[m, rows, cols].astype(F32)
            gr.append(zr[m] * hr + zs[m] * hi)
            gi.append(zr[m] * hi - zs[m] * hr)
        s02, d02, s13, d13 = gr[0] + gr[2], gr[0] - gr[2], gr[1] + gr[3], gr[1] - gr[3]
        t02, e02, t13, e13 = gi[0] + gi[2], gi[0] - gi[2], gi[1] + gi[3], gi[1] - gi[3]
        pc = [s02 + s13, d02 - t13, s02 - s13, d02 + t13]
        ps = [-e02 - e13, -t02 - d13, e13 - e02, d13 - t02]
        for r in range(RADIX):
            p_ref[r, rows, cols] = pc[r].astype(p_ref.dtype)
            p_ref[RADIX + r, rows, cols] = ps[r].astype(p_ref.dtype)


def _conv_spectrum(tab, z, h_cos, h_sin, layer, order, bsz, seq):
    q = seq // RADIX
    tq = min(DFT_ROWS, q)
    res = lambda r: pl.BlockSpec((None, None, q, GROUP_W), lambda b, i: (b, r, 0, 0))
    filt = lambda half: pl.BlockSpec((None, RADIX, tq, GROUP_W), lambda b, i: (layer, half, i, order))
    return pl.pallas_call(
        _conv_spectrum_kernel,
        grid=(bsz, q // tq),
        in_specs=[_resident((2 * RADIX, q, q), lambda b, i: (0, 0, 0)), res(0), res(1), res(2), res(3),
                  filt(0), filt(1)],
        out_specs=pl.BlockSpec((None, 2 * RADIX, tq, GROUP_W), lambda b, i: (b, 0, i, 0)),
        out_shape=jax.ShapeDtypeStruct((bsz, 2 * RADIX, q, GROUP_W), BF16),
        scratch_shapes=[pltpu.VMEM((2 * RADIX, tq, GROUP_W), F32)],
        compiler_params=_cp("parallel", "arbitrary"),
        name="hyena_spectrum",
    )(tab, z, z, z, z, h_cos, h_sin)


def _inverse_kernel(tab_ref, p_ref, xm_ref, w_ref, skip_ref, gn_ref, o_ref, *scratch, inv_len, final):
    tj = xm_ref.shape[1]
    tab = _table_rows(tab_ref, pl.program_id(1), tj)
    for r in range(RADIX):
        y = jnp.dot(tab(r), p_ref[r], preferred_element_type=F32)
        y = y + jnp.dot(tab(RADIX + r), p_ref[RADIX + r], preferred_element_type=F32)
        out = xm_ref[r].astype(F32) * (y * inv_len + w_ref[r].astype(F32) * skip_ref[...])
        if final:
            out = _rms(out, gn_ref[...])
            for c in range(GROUP_W // LANE):
                scratch[0][c, pl.ds(r, tj, stride=RADIX), :] = out[:, c * LANE:(c + 1) * LANE]
        else:
            o_ref[r] = out.astype(o_ref.dtype)
    if final:
        for c in range(GROUP_W // LANE):
            o_ref[:, c * LANE:(c + 1) * LANE] = scratch[0][c].astype(o_ref.dtype)


def _inverse(tab_t, pcs, xm, w, skip, gn, bsz, seq, final):
    q = seq // RADIX
    tj = min(DFT_ROWS, q)
    nj = q // tj
    tile = pl.BlockSpec((None, RADIX, tj, GROUP_W), lambda b, i: (b, 0, i, 0))
    vec = pl.BlockSpec((1, GROUP_W), lambda b, i: (0, 0))
    if final:
        out_spec = pl.BlockSpec((RADIX * tj, GROUP_W), lambda b, i: (b * nj + i, 0))
        out_shape = jax.ShapeDtypeStruct((bsz * seq, GROUP_W), BF16)
        scratch = [pltpu.VMEM((GROUP_W // LANE, RADIX * tj, LANE), F32)]
    else:
        out_spec, out_shape, scratch = tile, jax.ShapeDtypeStruct((bsz, RADIX, q, GROUP_W), BF16), []
    return pl.pallas_call(
        functools.partial(_inverse_kernel, inv_len=1.0 / seq, final=final),
        grid=(bsz, nj),
        in_specs=[_resident((2 * RADIX, q, q), lambda b, i: (0, 0, 0)),
                  pl.BlockSpec((None, 2 * RADIX, q, GROUP_W), lambda b, i: (b, 0, 0, 0)),
                  tile, tile, vec, vec],
        out_specs=out_spec,
        out_shape=out_shape,
        scratch_shapes=scratch,
        compiler_params=_cp("parallel", "arbitrary"),
        name="hyena_inverse",
    )(tab_t, pcs, xm, w, skip, gn)


def _out_kernel(ya_ref, yb_ref, yc_ref, yd_ref, w_ref, x_ref, o_ref):
    acc = x_ref[...]
    for g, y_ref in enumerate((ya_ref, yb_ref, yc_ref, yd_ref)):
        acc = acc + jnp.dot(y_ref[...], w_ref[g * GROUP_W:(g + 1) * GROUP_W, :], preferred_element_type=F32)
    o_ref[...] = acc


def _out_proj(ys, w, x):
    m, n = x.shape
    tm = min(512, m)
    piece = pl.BlockSpec((tm, GROUP_W), lambda i: (i, 0))
    xs = pl.BlockSpec((tm, n), lambda i: (i, 0))
    return pl.pallas_call(
        _out_kernel,
        grid=(m // tm,),
        in_specs=[piece, piece, piece, piece, _resident(w.shape, lambda i: (0, 0)), xs],
        out_specs=xs,
        out_shape=jax.ShapeDtypeStruct((m, n), F32),
        compiler_params=_cp("parallel"),
        name="out_proj",
    )(*ys, w, x)


def _attn_kernel(q_ref, k_ref, v_ref, wo_ref, x_ref, o_ref, att_ref):
    heads = [slice(h * XA_HD, (h + 1) * XA_HD) for h in range(XA_HEADS)]
    scores = [lax.dot_general(q_ref[:, hs], k_ref[:, hs], NT, preferred_element_type=F32) for hs in heads]
    for hs, s in zip(heads, scores):
        s = s * (XA_HD ** -0.5)
        e = jnp.exp(s - jnp.max(s, axis=-1, keepdims=True))
        p = e / jnp.sum(e, axis=-1, keepdims=True)
        att_ref[:, hs] = jnp.dot(p.astype(BF16), v_ref[:, hs], preferred_element_type=F32).astype(BF16)
    o_ref[...] = x_ref[...] + jnp.dot(att_ref[...], wo_ref[...], preferred_element_type=F32)


def _attn(q, kv, wo, x, seq):
    m, d = x.shape
    tm = min(512, seq)
    per = seq // tm
    xs = pl.BlockSpec((tm, d), lambda i: (i, 0))
    return pl.pallas_call(
        _attn_kernel,
        grid=(m // tm,),
        in_specs=[xs,
                  pl.BlockSpec((N_MEM, d), lambda i: (i // per, 0)),
                  pl.BlockSpec((N_MEM, d), lambda i: (i // per, 1)),
                  _resident((d, d), lambda i: (0, 0)),
                  xs],
        out_specs=xs,
        out_shape=jax.ShapeDtypeStruct((m, d), F32),
        scratch_shapes=[pltpu.VMEM((tm, d), BF16)],
        compiler_params=_cp("parallel"),
        name="cross_attn",
    )(q, kv, kv, wo, x)


def _swiglu_kernel(x_hbm, g_ref, wg_ref, wu_ref, wd_ref, fg_ref, o_ref, x_ref, hn_ref, sem, *, final, row_tiles):
    i, f = pl.program_id(0), pl.program_id(1)
    tm = x_ref.shape[0]

    def x_copy(tile):
        return pltpu.make_async_copy(x_hbm.at[pl.ds(pl.multiple_of(tile * tm, tm), tm), :], x_ref, sem)

    @pl.when((i == 0) & (f == 0))
    def _():
        x_copy(0).start()

    @pl.when(f == 0)
    def _():
        x_copy(i).wait()
        _norm_rows_to(x_ref, g_ref, hn_ref)
        o_ref[...] = x_ref[...]

    @pl.when((f == 1) & (i + 1 < row_tiles))
    def _():
        x_copy(i + 1).start()

    hn = hn_ref[...]
    gate = jnp.dot(hn, wg_ref[...], preferred_element_type=F32)
    up = jnp.dot(hn, wu_ref[...], preferred_element_type=F32)
    act = (gate * jax.nn.sigmoid(gate) * up).astype(BF16)
    o_ref[...] += jnp.dot(act, wd_ref[...], preferred_element_type=F32)

    if final:
        @pl.when(f == pl.num_programs(1) - 1)
        def _():
            _norm_rows_to(o_ref, fg_ref, o_ref)


def _swiglu(x, g, w_gu, w_d, final_g, final):
    m, d = x.shape
    dff = w_d.shape[0]
    tm, tf = min(1024, m), 512
    nf = dff // tf
    assert nf >= 2
    vec = pl.BlockSpec((1, d), lambda i, f: (0, 0))
    return pl.pallas_call(
        functools.partial(_swiglu_kernel, final=final, row_tiles=m // tm),
        grid=(m // tm, nf),
        in_specs=[pl.BlockSpec(memory_space=pl.ANY), vec,
                  pl.BlockSpec((d, tf), lambda i, f: (0, f)),
                  pl.BlockSpec((d, tf), lambda i, f: (0, nf + f)),
                  pl.BlockSpec((tf, d), lambda i, f: (f, 0)),
                  vec],
        out_specs=pl.BlockSpec((tm, d), lambda i, f: (i, 0)),
        out_shape=jax.ShapeDtypeStruct((m, d), F32),
        scratch_shapes=[pltpu.VMEM((tm, d), F32), pltpu.VMEM((tm, d), BF16), pltpu.SemaphoreType.DMA(())],
        compiler_params=_cp("arbitrary", "arbitrary"),
        name="swiglu",
    )(x, g.reshape(1, d), w_gu, w_gu, w_d, final_g.reshape(1, d))


def _angles(rows, cols, period):
    m = (rows[..., :, None] * cols[..., None, :]) % period
    return m.astype(F32) * (2.0 * math.pi / period)


def _cos_sin(ang, sin_sign=1.0):
    return jnp.concatenate([jnp.cos(ang), sin_sign * jnp.sin(ang)], axis=0).astype(BF16)


def _tables(seq):
    q = seq // RADIX
    base = jnp.arange(q, dtype=jnp.int32)
    times = RADIX * base[None, :] + jnp.arange(RADIX, dtype=jnp.int32)[:, None]
    rep = jnp.broadcast_to(base, (RADIX, q))
    fnet_tab = _cos_sin(_angles(rep, times, seq))
    ch = jnp.arange(GROUP_W, dtype=jnp.int32)
    same = (ch[:, None] // FNET_CH) == (ch[None, :] // FNET_CH)
    angc = _angles(ch % FNET_CH, ch % FNET_CH, FNET_CH)
    cc = jnp.where(same, jnp.cos(angc), 0.0).astype(BF16)
    sc = jnp.where(same, jnp.sin(angc), 0.0).astype(BF16)
    hy_tab = _cos_sin(_angles(2 * rep + 1, times, 4 * seq))
    hy_tab_t = jnp.swapaxes(hy_tab, 1, 2)
    return fnet_tab, cc, sc, hy_tab, hy_tab_t


def _position_features(seq):
    pos = jnp.arange(seq, dtype=F32)
    t = pos / seq
    f = jnp.linspace(1e-4, HY_BANDS - 1, HY_BANDS, dtype=F32)
    ang = (2.0 * math.pi * t)[:, None] * f[None, :]
    feats = jnp.concatenate([t[:, None], jnp.cos(ang), -jnp.sin(ang)], axis=-1)
    return jnp.pad(feats, ((0, 0), (0, LANE - HY_EMB)))


def kernel(x, mem, norm_g, w_in, gla_gk_w, gla_gk_b, gla_norm_g, hy_conv_w, hy_ffn_w1, hy_ffn_b1, hy_ffn_w2, hy_ffn_b2, hy_ffn_w3, hy_sin_freq, hy_decay, hy_skip, sc_conv_w, grp_norm_g, w_out, mem_norm_g, w_xq, w_xkv, w_xo, w_gate_up, w_down, final_norm_g):
    bsz, seq, d = x.shape
    depth = norm_g.shape[0]
    m = bsz * seq
    assert d == D_MODEL and seq % (RADIX * GLA_CHUNK) == 0 and mem.shape[1] == N_MEM

    q_w, k_w, v_w, g_w, lr_w, uf_w, uh_w, us_w = jnp.split(
        w_in.astype(BF16), [256, 512, 1024, 1536, 1568, 2080, 3616], axis=-1)
    w_p = jnp.concatenate([uh_w, us_w, q_w, k_w, v_w, g_w, uf_w], axis=-1)
    w_lr = jnp.pad(lr_w, ((0, 0), (0, 0), (0, LANE - 2 * GLA_RANK)))
    wgf = jnp.pad(gla_gk_w[:, 0], ((0, 0), (0, LANE - GLA_RANK), (0, 0))).astype(BF16)
    wgb = jnp.pad(gla_gk_w[:, 1], ((0, 0), (GLA_RANK, LANE - 2 * GLA_RANK), (0, 0))).astype(BF16)
    gkb = gla_gk_b.astype(F32)
    w_out_b, w_xq_b, w_xkv_b, w_xo_b = (t.astype(BF16) for t in (w_out, w_xq, w_xkv, w_xo))
    w_gu_b, w_d_b = w_gate_up.astype(BF16), w_down.astype(BF16)
    fnet_tab, cc, sc, hy_tab, hy_tab_t = _tables(seq)
    qlen = seq // RADIX
    feats = _position_features(seq).reshape(qlen, RADIX, LANE).transpose(1, 0, 2).reshape(seq, LANE)

    w1 = jnp.pad(hy_ffn_w1.astype(F32), ((0, 0), (0, LANE - HY_EMB), (0, 0)))
    fa, fb = _filters(feats, w1, hy_ffn_b1.reshape(depth, 1, HY_FFN).astype(F32), hy_ffn_w2.astype(F32),
                      hy_ffn_b2.reshape(depth, 1, HY_FFN).astype(F32), hy_ffn_w3.astype(F32),
                      hy_sin_freq.reshape(depth, 1, HY_FFN).astype(F32),
                      hy_decay.reshape(depth, 1, 4 * GROUP_W).astype(F32), seq)
    res_major = lambda t: t.reshape(depth, RADIX, qlen, t.shape[-1])
    h_cos = _filter_spectrum(hy_tab, res_major(fa), seq)
    h_sin = _filter_spectrum(hy_tab, res_major(fb), seq)

    mem2 = mem.reshape(bsz * N_MEM, d)
    xf = x.reshape(m, d)
    for l in range(depth):
        p, lr = _norm_matmul(xf, norm_g[l, 0], w_p[l], 1024, 1280, w_side=w_lr[l])
        y_a = _gla(p, lr, wgf[l], wgb[l], gkb[l, 0:1], gkb[l, 1:2], gla_norm_g[l].reshape(1, GLA_DV).astype(F32),
                   bsz, seq)
        hv, hx1, hx2, uf, y_d = _short(p, hy_conv_w[l].astype(F32), sc_conv_w[l].astype(F32),
                                       grp_norm_g[l, 2:3].astype(F32), bsz, seq)
        y_b = _fnet(uf, fnet_tab, cc, sc, grp_norm_g[l, 0:1].astype(F32), bsz, seq)
        gn_c = grp_norm_g[l, 1:2].astype(F32)
        pcs = _conv_spectrum(hy_tab, hv, h_cos, h_sin, l, 0, bsz, seq)
        z = _inverse(hy_tab_t, pcs, hx1, hv, hy_skip[l, 0:1].astype(F32), gn_c, bsz, seq, final=False)
        pcs = _conv_spectrum(hy_tab, z, h_cos, h_sin, l, 1, bsz, seq)
        y_c = _inverse(hy_tab_t, pcs, hx2, z, hy_skip[l, 1:2].astype(F32), gn_c, bsz, seq, final=True)
        xf = _out_proj((y_a, y_b, y_c, y_d), w_out_b[l], xf)
        q = _norm_matmul(xf, norm_g[l, 1], w_xq_b[l], 1024, D_MODEL)
        kv = _norm_matmul(mem2, mem_norm_g, w_xkv_b[l], 1024, 1024)
        xf = _attn(q, kv, w_xo_b[l], xf, seq)
        xf = _swiglu(xf, norm_g[l, 2], w_gu_b[l], w_d_b[l], final_norm_g, final=(l == depth - 1))
    return xf.reshape(bsz, seq, d)
```

```python
import functools
import math

import jax
import jax.numpy as jnp
from jax import lax
from jax.experimental import pallas as pl
from jax.experimental.pallas import tpu as pltpu

F32 = jnp.float32
BF16 = jnp.bfloat16
EPS = 1e-6

D_MODEL = 2048
GROUP_W = 512
GLA_HEADS = 4
GLA_DK = 64
GLA_DV = 128
GLA_RANK = 16
GLA_GATE_NORM = 16.0
GLA_CHUNK = 64
FNET_CH = 128
HY_BANDS = 16
HY_EMB = 2 * HY_BANDS + 1
HY_FFN = 64
XA_HEADS = 4
XA_HD = D_MODEL // XA_HEADS
N_MEM = 256
D_FF = 5632
RADIX = 4
DFT_ROWS = 512
LANE = 128
BF16_ROWS = 16
VMEM_LIMIT = 56 * 1024 * 1024

GLA_QK = GLA_HEADS * GLA_DK
P_UH, P_US, P_Q, P_K, P_V, P_G, P_UF = 0, 1536, 3072, 3328, 3584, 4096, 4608
P_W = 5120

NT = (((1,), (1,)), ((), ()))
TN = (((0,), (0,)), ((), ()))


def _cp(*sem):
    return pltpu.CompilerParams(dimension_semantics=sem, vmem_limit_bytes=VMEM_LIMIT)


def _resident(shape, index_map):
    return pl.BlockSpec(shape, index_map, pipeline_mode=pl.Buffered(1))


def _rms(x, g):
    ms = jnp.mean(x * x, axis=-1, keepdims=True)
    return x * lax.rsqrt(ms + EPS) * g


def _norm_rows_to(x_ref, g_ref, hn_ref):
    rows = min(256, x_ref.shape[0])
    n = x_ref.shape[0] // rows

    def body(i, c):
        r = pl.ds(pl.multiple_of(i * rows, rows), rows)
        hn_ref[r, :] = _rms(x_ref[r, :], g_ref[...]).astype(hn_ref.dtype)
        return c

    lax.fori_loop(0, n, body, 0)


def _norm_matmul_kernel(x_ref, g_ref, w_ref, *rest, with_side, row_tiles, col_tiles):
    if with_side:
        w2_ref, o_ref, o2_ref, hn_ref = rest
    else:
        o_ref, hn_ref = rest
    i, j = pl.program_id(0), pl.program_id(1)
    part = x_ref.shape[0] // col_tiles

    def norm_part():
        g = g_ref[...]
        for r in range(0, part, BF16_ROWS):
            rows = pl.ds(pl.multiple_of(j * part + r, BF16_ROWS), BF16_ROWS)
            hn_ref[i % 2, rows, :] = _rms(x_ref[rows, :], g).astype(BF16)

    def matmul():
        o_ref[...] = jnp.dot(hn_ref[(i + 1) % 2], w_ref[...], preferred_element_type=F32).astype(o_ref.dtype)

    @pl.when(i == 0)
    def _():
        norm_part()

    @pl.when((i > 0) & (i < row_tiles))
    def _():
        matmul()
        norm_part()

    @pl.when(i == row_tiles)
    def _():
        matmul()

    if with_side:
        @pl.when((i > 0) & (j == 0))
        def _():
            o2_ref[...] = jnp.dot(hn_ref[(i + 1) % 2], w2_ref[...], preferred_element_type=F32)


def _norm_matmul(x, g, w, layer, tm, tn, w_side=None):
    m, k = x.shape
    n = w.shape[2]
    tm, tn = min(tm, m), min(tn, n)
    ni, nj = m // tm, n // tn
    assert (tm // nj) % BF16_ROWS == 0
    prev = lambda i: jnp.maximum(i - 1, 0)
    in_specs = [
        pl.BlockSpec((tm, k), lambda i, j: (jnp.minimum(i, ni - 1), 0)),
        pl.BlockSpec((1, k), lambda i, j: (0, 0)),
        (_resident if tn == n else pl.BlockSpec)((None, k, tn), lambda i, j: (layer, 0, j)),
    ]
    out_shape = [jax.ShapeDtypeStruct((m, n), BF16)]
    out_specs = [pl.BlockSpec((tm, tn), lambda i, j: (prev(i), jnp.where(i == 0, 0, j)))]
    args = [x, g.reshape(1, k), w]
    if w_side is not None:
        n2 = w_side.shape[2]
        in_specs.append(pl.BlockSpec((None, k, n2), lambda i, j: (layer, 0, 0)))
        out_shape.append(jax.ShapeDtypeStruct((m, n2), F32))
        out_specs.append(pl.BlockSpec((tm, n2), lambda i, j: (prev(i), 0)))
        args.append(w_side)
    res = pl.pallas_call(
        functools.partial(_norm_matmul_kernel, with_side=w_side is not None, row_tiles=ni, col_tiles=nj),
        grid=(ni + 1, nj),
        in_specs=in_specs,
        out_specs=out_specs,
        out_shape=out_shape,
        scratch_shapes=[pltpu.VMEM((2, tm, k), BF16)],
        compiler_params=_cp("arbitrary", "arbitrary"),
        name="norm_matmul",
    )(*args)
    return res if w_side is not None else res[0]


def _gla_pass(fwd, blk, q_ref, k_ref, v_ref, lr_ref, g_ref, wg_ref, bg_ref, gn_ref, y_ref, st_ref, ob_ref):
    ns, ts = q_ref.shape[0], q_ref.shape[1]
    c_sz = GLA_CHUNK
    nc = ts // c_sz
    ii = lax.broadcasted_iota(jnp.int32, (c_sz, c_sz), 0)
    jj = lax.broadcasted_iota(jnp.int32, (c_sz, c_sz), 1)
    incl = (jj <= ii) if fwd else (jj >= ii)
    mask = incl if fwd else (jj > ii)
    tri = jnp.where(incl, 1.0, 0.0).astype(BF16)
    gn = gn_ref[...]
    chunk = lambda c: slice(c * c_sz, (c + 1) * c_sz)
    order = list(range(nc)) if fwd else list(reversed(range(nc)))
    problems = [(s, h) for s in range(ns) for h in range(GLA_HEADS)]
    pair = lambda h: slice((h // 2) * LANE, (h // 2 + 1) * LANE)
    val = lambda h: slice(h * GLA_DV, (h + 1) * GLA_DV)
    first = lax.broadcasted_iota(jnp.int32, (c_sz, LANE), 1) < GLA_DK
    own = lambda h: first if h % 2 == 0 else jnp.logical_not(first)

    qe, ke, qi, ku, dec = {}, {}, {}, {}, {}
    for s in range(ns):
        x = jnp.dot(lr_ref[s].astype(BF16), wg_ref[...], preferred_element_type=F32) + bg_ref[...]
        gk = (jnp.minimum(x, 0.0) - jnp.log1p(jnp.exp(-jnp.abs(x)))) * (1.0 / GLA_GATE_NORM)
        hi = gk.astype(BF16)
        lo = (gk - hi.astype(F32)).astype(BF16)
        for c in range(nc):
            r = chunk(c)
            b = jnp.dot(tri, hi[r], preferred_element_type=F32) + jnp.dot(tri, lo[r], preferred_element_type=F32)
            bref = b[c_sz // 2:c_sz // 2 + 1] if fwd else b[c_sz // 2 - 1:c_sz // 2]
            btot = b[c_sz - 1:c_sz] if fwd else b[0:1]
            qe_f = q_ref[s, r, :].astype(F32) * (GLA_DK ** -0.5) * jnp.exp(b - bref)
            ke_f = k_ref[s, r, :].astype(F32) * jnp.exp(bref - b)
            qe[s, c] = qe_f
            ke[s, c] = ke_f.astype(BF16)
            qi[s, c] = (qe_f * jnp.exp(bref)).astype(BF16)
            ku[s, c] = ke_f * jnp.exp(btot - bref)
            dec[s, c] = jnp.exp(btot)

    kv = {}
    for c in order:
        for s, h in problems:
            kv[s, h, c] = lax.dot_general(v_ref[s, chunk(c), val(h)],
                                          jnp.where(own(h), ku[s, c][:, pair(h)], 0.0).astype(BF16), TN,
                                          preferred_element_type=F32)
    st = {(s, h): st_ref[s, h] for s, h in problems}
    st_in = {}
    for c in order:
        for s, h in problems:
            st_in[s, h, c] = st[s, h].astype(BF16)
            st[s, h] = st[s, h] * dec[s, c][:, pair(h)] + kv[s, h, c]
    for s, h in problems:
        st_ref[s, h] = st[s, h]
    for c in order:
        r = chunk(c)
        rows = pl.ds(pl.multiple_of(blk * ts + c * c_sz, c_sz), c_sz)
        scores = {}
        for s, h in problems:
            sc = lax.dot_general(jnp.where(own(h), qe[s, c][:, pair(h)], 0.0).astype(BF16), ke[s, c][:, pair(h)], NT,
                                 preferred_element_type=F32)
            scores[s, h] = jnp.where(mask, sc, 0.0).astype(BF16)
        for s, h in problems:
            o = jnp.dot(scores[s, h], v_ref[s, r, val(h)], preferred_element_type=F32)
            o = o + lax.dot_general(qi[s, c][:, pair(h)], st_in[s, h, c], NT, preferred_element_type=F32)
            if fwd:
                tot = o + ob_ref[s, rows, val(h)]
                gate = g_ref[s, r, val(h)].astype(F32)
                y_ref[s, r, val(h)] = (_rms(tot, gn) * (gate * jax.nn.sigmoid(gate))).astype(y_ref.dtype)
            else:
                ob_ref[s, rows, val(h)] = o


def _gla_kernel(q_ref, k_ref, v_ref, lr_ref, g_ref, wgf_ref, wgb_ref, bgf_ref, bgb_ref, gn_ref,
                y_ref, st_ref, ob_ref, *, nb):
    n = pl.program_id(1)

    @pl.when((n == 0) | (n == nb))
    def _():
        st_ref[...] = jnp.zeros_like(st_ref)

    @pl.when(n < nb)
    def _():
        _gla_pass(False, nb - 1 - n, q_ref, k_ref, v_ref, lr_ref, g_ref, wgb_ref, bgb_ref, gn_ref, y_ref, st_ref, ob_ref)

    @pl.when(n >= nb)
    def _():
        _gla_pass(True, n - nb, q_ref, k_ref, v_ref, lr_ref, g_ref, wgf_ref, bgf_ref, gn_ref, y_ref, st_ref, ob_ref)


def _gla(p, lr, wgf, wgb, bgf, bgb, gn, bsz, seq):
    ts = min(512, seq)
    nb = seq // ts
    ns = 2 if bsz % 2 == 0 else 1
    blk = lambda n: jnp.where(n < nb, nb - 1 - n, n - nb)
    oblk = lambda n: jnp.where(n < nb, 0, n - nb)
    in_specs = [
        pl.BlockSpec((ns, ts, GLA_QK), lambda b, n: (b, blk(n), P_Q // GLA_QK)),
        pl.BlockSpec((ns, ts, GLA_QK), lambda b, n: (b, blk(n), P_K // GLA_QK)),
        pl.BlockSpec((ns, ts, GROUP_W), lambda b, n: (b, blk(n), P_V // GROUP_W)),
        pl.BlockSpec((ns, ts, LANE), lambda b, n: (b, blk(n), 0)),
        pl.BlockSpec((ns, ts, GROUP_W), lambda b, n: (b, oblk(n), P_G // GROUP_W)),
        pl.BlockSpec((LANE, GLA_QK), lambda b, n: (0, 0)),
        pl.BlockSpec((LANE, GLA_QK), lambda b, n: (0, 0)),
        pl.BlockSpec((1, GLA_QK), lambda b, n: (0, 0)),
        pl.BlockSpec((1, GLA_QK), lambda b, n: (0, 0)),
        pl.BlockSpec((1, GLA_DV), lambda b, n: (0, 0)),
    ]
    p3 = p.reshape(bsz, seq, P_W)
    y = pl.pallas_call(
        functools.partial(_gla_kernel, nb=nb),
        grid=(bsz // ns, 2 * nb),
        in_specs=in_specs,
        out_specs=pl.BlockSpec((ns, ts, GROUP_W), lambda b, n: (b, oblk(n), 0)),
        out_shape=jax.ShapeDtypeStruct((bsz, seq, GROUP_W), BF16),
        scratch_shapes=[pltpu.VMEM((ns, GLA_HEADS, GLA_DV, LANE), F32), pltpu.VMEM((ns, seq, GROUP_W), F32)],
        compiler_params=_cp("parallel", "arbitrary"),
        name="gla",
    )(p3, p3, p3, lr.reshape(bsz, seq, LANE), p3, wgf, wgb, bgf, bgb, gn)
    return y.reshape(bsz * seq, GROUP_W)


def _table_rows(tab_ref, tile, rows):
    window = pl.ds(pl.multiple_of(tile * rows, rows), rows)
    return lambda k: tab_ref[k, window, :]


def _butterflies(a, b):
    a02p, a02m, a13p, a13m = a[0] + a[2], a[0] - a[2], a[1] + a[3], a[1] - a[3]
    b02p, b02m, b13p, b13m = b[0] + b[2], b[0] - b[2], b[1] + b[3], b[1] - b[3]
    re = [a02p + a13p, a02m - b13m, a02p - a13p, a02m + b13m]
    nim = [b02p + b13p, b02m + a13m, b02p - b13p, b02m - a13m]
    return re, nim


def _stage_products(tab, z_refs, ab_ref):
    for k in range(2 * RADIX):
        ab_ref[k] = jnp.dot(tab(k), z_refs[k % RADIX][...], preferred_element_type=F32)


def _load_products(ab_ref, rows, cols):
    return ([ab_ref[r, rows, cols] for r in range(RADIX)], [ab_ref[RADIX + r, rows, cols] for r in range(RADIX)])


def _groups(n_rows, n_cols, cols_per_group=2 * LANE):
    return [(slice(r, r + BF16_ROWS), slice(c, c + cols_per_group))
            for r in range(0, n_rows, BF16_ROWS) for c in range(0, n_cols, cols_per_group)]


def _fnet_kernel(tab_ref, x0, x1, x2, x3, cc_ref, sc_ref, gn_ref, y_ref, ab_ref, bf_ref, *, scale):
    tq = y_ref.shape[1]
    _stage_products(_table_rows(tab_ref, pl.program_id(1), tq), (x0, x1, x2, x3), ab_ref)
    for rows, cols in _groups(tq, GROUP_W):
        re, nim = _butterflies(*_load_products(ab_ref, rows, cols))
        for m in range(RADIX):
            bf_ref[m, rows, cols] = re[m].astype(BF16)
            bf_ref[RADIX + m, rows, cols] = nim[m].astype(BF16)
    for m in range(RADIX):
        y = jnp.dot(bf_ref[m], cc_ref[...], preferred_element_type=F32)
        y = y - jnp.dot(bf_ref[RADIX + m], sc_ref[...], preferred_element_type=F32)
        y_ref[m] = _rms(y * scale, gn_ref[...]).astype(y_ref.dtype)


def _fnet(uf, tab, cc, sc, gn, bsz, seq):
    q = seq // RADIX
    tq = min(DFT_ROWS, q)
    nq = q // tq
    res = lambda r: pl.BlockSpec((None, None, q, GROUP_W), lambda b, i: (b, r, 0, 0))
    sq = pl.BlockSpec((GROUP_W, GROUP_W), lambda b, i: (0, 0))
    out = pl.pallas_call(
        functools.partial(_fnet_kernel, scale=1.0 / math.sqrt(seq * FNET_CH)),
        grid=(bsz, nq),
        in_specs=[_resident((2 * RADIX, q, q), lambda b, i: (0, 0, 0)),
                  res(0), res(1), res(2), res(3), sq, sq,
                  pl.BlockSpec((1, GROUP_W), lambda b, i: (0, 0))],
        out_specs=pl.BlockSpec((None, RADIX, tq, GROUP_W), lambda b, i: (b, 0, i, 0)),
        out_shape=jax.ShapeDtypeStruct((bsz, RADIX, q, GROUP_W), BF16),
        scratch_shapes=[pltpu.VMEM((2 * RADIX, tq, GROUP_W), F32), pltpu.VMEM((2 * RADIX, tq, GROUP_W), BF16)],
        compiler_params=_cp("parallel", "arbitrary"),
        name="fnet",
    )(tab, uf, uf, uf, uf, cc, sc, gn)
    return out.reshape(bsz * seq, GROUP_W)


def _conv3(main, prev_row, next_row, w):
    ts = main.shape[0]
    rid = lax.broadcasted_iota(jnp.int32, main.shape, 0)
    up = jnp.where(rid == 0, prev_row, pltpu.roll(main, 1, 0))
    dn = jnp.where(rid == ts - 1, next_row, pltpu.roll(main, ts - 1, 0))
    return up * w[0:1] + main * w[1:2] + dn * w[2:3]


def _short_kernel(hm_ref, hp_ref, hn_ref, sm_ref, sp_ref, sn_ref, uf_ref, hw_ref, sw_ref, gn_ref,
                  v_ref, x1_ref, x2_ref, ufr_ref, yd_ref, nat_ref, *, nblk):
    i = pl.program_id(0)
    keep_prev = jnp.where(i % nblk == 0, 0.0, 1.0)
    keep_next = jnp.where(i % nblk == nblk - 1, 0.0, 1.0)
    last = BF16_ROWS - 1
    w = GROUP_W
    ts = hm_ref.shape[0]
    per = w // LANE

    def neighbours(ref, halo_prev, halo_next, r0, rows, cols):
        above = (ref[r0 - 1:r0, cols].astype(F32) if r0 > 0
                 else halo_prev[last:last + 1, cols].astype(F32) * keep_prev)
        below = (ref[r0 + rows:r0 + rows + 1, cols].astype(F32) if r0 + rows < ts
                 else halo_next[0:1, cols].astype(F32) * keep_next)
        return above, below

    rows_h, lanes_h = 2 * BF16_ROWS, 2 * LANE
    for r0 in range(0, ts, rows_h):
        for c0 in range(0, 3 * w, lanes_h):
            cols = slice(c0, c0 + lanes_h)
            above, below = neighbours(hm_ref, hp_ref, hn_ref, r0, rows_h, cols)
            conv = _conv3(hm_ref[r0:r0 + rows_h, cols].astype(F32), above, below, hw_ref[:, cols])
            for k in range(lanes_h // LANE):
                nat_ref[c0 // LANE + k, r0:r0 + rows_h, :] = conv[:, k * LANE:(k + 1) * LANE]
    for c in range(per):
        nat_ref[3 * per + c] = uf_ref[:, c * LANE:(c + 1) * LANE].astype(F32)
    for g, dst in enumerate((v_ref, x1_ref, x2_ref, ufr_ref)):
        for r in range(RADIX):
            for c in range(per):
                piece = nat_ref[g * per + c, pl.ds(r, ts // RADIX, stride=RADIX), :]
                dst[r, :, c * LANE:(c + 1) * LANE] = piece.astype(dst.dtype)

    bcol, ccol, hcol = slice(0, w), slice(w, 2 * w), slice(2 * w, 3 * w)
    rows_s = BF16_ROWS
    for r0 in range(0, ts, rows_s):
        rs = slice(r0, r0 + rows_s)
        ca, cb = neighbours(sm_ref, sp_ref, sn_ref, r0, rows_s, ccol)
        ha, hb = neighbours(sm_ref, sp_ref, sn_ref, r0, rows_s, hcol)
        prod = sm_ref[rs, ccol].astype(F32) * sm_ref[rs, hcol].astype(F32)
        conv = _conv3(prod, ca * ha, cb * hb, sw_ref[...])
        yd_ref[rs, :] = _rms(sm_ref[rs, bcol].astype(F32) * conv, gn_ref[...]).astype(yd_ref.dtype)


def _short(p, hy_w, sc_w, gn, bsz, seq):
    m = bsz * seq
    ts = min(256, seq)
    nblk = seq // ts
    hb = ts // BF16_ROWS
    nh = m // BF16_ROWS
    wide = 3 * GROUP_W
    main = lambda cb: pl.BlockSpec((ts, wide), lambda i: (i, cb))
    prev = lambda cb: pl.BlockSpec((BF16_ROWS, wide), lambda i: (jnp.maximum(i * hb - 1, 0), cb))
    nxt = lambda cb: pl.BlockSpec((BF16_ROWS, wide), lambda i: (jnp.minimum((i + 1) * hb, nh - 1), cb))
    nat = pl.BlockSpec((ts, GROUP_W), lambda i: (i, 0))
    res = pl.BlockSpec((None, RADIX, ts // RADIX, GROUP_W), lambda i: (i // nblk, 0, i % nblk, 0))
    res_shape = jax.ShapeDtypeStruct((bsz, RADIX, seq // RADIX, GROUP_W), BF16)
    return pl.pallas_call(
        functools.partial(_short_kernel, nblk=nblk),
        grid=(m // ts,),
        in_specs=[main(0), prev(0), nxt(0), main(1), prev(1), nxt(1),
                  pl.BlockSpec((ts, GROUP_W), lambda i: (i, P_UF // GROUP_W)),
                  pl.BlockSpec((3, wide), lambda i: (0, 0)),
                  pl.BlockSpec((3, GROUP_W), lambda i: (0, 0)),
                  pl.BlockSpec((1, GROUP_W), lambda i: (0, 0))],
        out_specs=[res, res, res, res, nat],
        out_shape=[res_shape] * 4 + [jax.ShapeDtypeStruct((m, GROUP_W), BF16)],
        scratch_shapes=[pltpu.VMEM((4 * GROUP_W // LANE, ts, LANE), F32)],
        compiler_params=_cp("parallel"),
        name="short_conv",
    )(p, p, p, p, p, p, p, hy_w, sc_w, gn)


def _filter_kernel(feat_ref, w1_ref, b1_ref, w2_ref, b2_ref, w3_ref, fr_ref, dc_ref, a_ref, b_ref):
    hp = lax.Precision.HIGHEST
    feats = feat_ref[...]
    fr = fr_ref[...]
    h = jnp.sin(fr * (jnp.dot(feats, w1_ref[...], precision=hp, preferred_element_type=F32) + b1_ref[...]))
    h = jnp.sin(fr * (jnp.dot(h, w2_ref[...], precision=hp, preferred_element_type=F32) + b2_ref[...]))
    split = lambda t: (t.astype(BF16), (t - t.astype(BF16).astype(F32)).astype(BF16))
    (h_hi, h_lo), (w_hi, w_lo) = split(h), split(w3_ref[...])
    h = (jnp.dot(h_hi, w_hi, preferred_element_type=F32) + jnp.dot(h_hi, w_lo, preferred_element_type=F32)
         + jnp.dot(h_lo, w_hi, preferred_element_type=F32))
    h = h * jnp.exp(-feats[:, 0:1] * jnp.abs(dc_ref[...]))
    tl = h.shape[0]
    pos = lax.broadcasted_iota(jnp.int32, (tl, GROUP_W), 0) + pl.program_id(1) * tl
    for o in range(2):
        hf = h[:, (2 * o) * GROUP_W:(2 * o + 1) * GROUP_W]
        hb = jnp.where(pos == 0, 0.0, h[:, (2 * o + 1) * GROUP_W:(2 * o + 2) * GROUP_W])
        a_ref[:, o * GROUP_W:(o + 1) * GROUP_W] = (hf + hb).astype(a_ref.dtype)
        b_ref[:, o * GROUP_W:(o + 1) * GROUP_W] = (hb - hf).astype(b_ref.dtype)


def _filters(feats, w1, b1, w2, b2, w3, fr, dc, seq):
    depth = w1.shape[0]
    tl = min(512, seq)
    lay = lambda r, c: pl.BlockSpec((None, r, c), lambda l, i: (l, 0, 0))
    out = pl.BlockSpec((None, tl, 2 * GROUP_W), lambda l, i: (l, i, 0))
    return pl.pallas_call(
        _filter_kernel,
        grid=(depth, seq // tl),
        in_specs=[pl.BlockSpec((tl, LANE), lambda l, i: (i, 0)),
                  lay(LANE, HY_FFN), lay(1, HY_FFN), lay(HY_FFN, HY_FFN), lay(1, HY_FFN),
                  lay(HY_FFN, 4 * GROUP_W), lay(1, HY_FFN), lay(1, 4 * GROUP_W)],
        out_specs=[out, out],
        out_shape=[jax.ShapeDtypeStruct((depth, seq, 2 * GROUP_W), BF16)] * 2,
        compiler_params=_cp("parallel", "parallel"),
        name="hyena_filter",
    )(feats, w1, b1, w2, b2, w3, fr, dc)


def _hyena_classes(a, b):
    re, nim = _butterflies(a, b)
    return re, [nim[0], nim[1], -nim[2], -nim[3]]


def _spectrum_kernel(tab_ref, z0, z1, z2, z3, o_ref, ab_ref):
    tq = o_ref.shape[1]
    _stage_products(_table_rows(tab_ref, pl.program_id(1), tq), (z0, z1, z2, z3), ab_ref)
    for rows, cols in _groups(tq, GROUP_W):
        zr, zs = _hyena_classes(*_load_products(ab_ref, rows, cols))
        for m in range(RADIX):
            o_ref[m, rows, cols] = zr[m].astype(o_ref.dtype)
            o_ref[RADIX + m, rows, cols] = zs[m].astype(o_ref.dtype)


def _filter_spectrum(tab, filt, seq):
    depth, _, _, cols = filt.shape
    q = seq // RADIX
    tq = min(DFT_ROWS, q)
    ncb = cols // GROUP_W
    res = lambda r: pl.BlockSpec((None, None, q, GROUP_W), lambda n, i: (n // ncb, r, 0, n % ncb))
    return pl.pallas_call(
        _spectrum_kernel,
        grid=(depth * ncb, q // tq),
        in_specs=[_resident((2 * RADIX, q, q), lambda n, i: (0, 0, 0)), res(0), res(1), res(2), res(3)],
        out_specs=pl.BlockSpec((None, 2 * RADIX, tq, GROUP_W), lambda n, i: (n // ncb, 0, i, n % ncb)),
        out_shape=jax.ShapeDtypeStruct((depth, 2 * RADIX, q, cols), BF16),
        scratch_shapes=[pltpu.VMEM((2 * RADIX, tq, GROUP_W), F32)],
        compiler_params=_cp("parallel", "arbitrary"),
        name="filter_spectrum",
    )(tab, filt, filt, filt, filt)


def _conv_spectrum_kernel(tab_ref, z0, z1, z2, z3, hr_ref, hi_ref, p_ref, ab_ref):
    tq = p_ref.shape[1]
    _stage_products(_table_rows(tab_ref, pl.program_id(1), tq), (z0, z1, z2, z3), ab_ref)
    for rows, cols in _groups(tq, GROUP_W):
        zr, zs = _hyena_classes(*_load_products(ab_ref, rows, cols))
        gr, gi = [], []
        for m in range(RADIX):
            hr, hi = hr_ref[m, rows, cols].astype(F32), hi_ref[m, rows, cols].astype(F32)
            gr.append(zr[m] * hr + zs[m] * hi)
            gi.append(zr[m] * hi - zs[m] * hr)
        s02, d02, s13, d13 = gr[0] + gr[2], gr[0] - gr[2], gr[1] + gr[3], gr[1] - gr[3]
        t02, e02, t13, e13 = gi[0] + gi[2], gi[0] - gi[2], gi[1] + gi[3], gi[1] - gi[3]
        pc = [s02 + s13, d02 - t13, s02 - s13, d02 + t13]
        ps = [-e02 - e13, -t02 - d13, e13 - e02, d13 - t02]
        for r in range(RADIX):
            p_ref[r, rows, cols] = pc[r].astype(p_ref.dtype)
            p_ref[RADIX + r, rows, cols] = ps[r].astype(p_ref.dtype)


def _conv_spectrum(tab, z, h_cos, h_sin, layer, order, bsz, seq):
    q = seq // RADIX
    tq = min(DFT_ROWS, q)
    res = lambda r: pl.BlockSpec((None, None, q, GROUP_W), lambda b, i: (b, r, 0, 0))
    filt = lambda half: pl.BlockSpec((None, RADIX, tq, GROUP_W), lambda b, i: (layer, half, i, order))
    return pl.pallas_call(
        _conv_spectrum_kernel,
        grid=(bsz, q // tq),
        in_specs=[_resident((2 * RADIX, q, q), lambda b, i: (0, 0, 0)), res(0), res(1), res(2), res(3),
                  filt(0), filt(1)],
        out_specs=pl.BlockSpec((None, 2 * RADIX, tq, GROUP_W), lambda b, i: (b, 0, i, 0)),
        out_shape=jax.ShapeDtypeStruct((bsz, 2 * RADIX, q, GROUP_W), BF16),
        scratch_shapes=[pltpu.VMEM((2 * RADIX, tq, GROUP_W), F32)],
        compiler_params=_cp("parallel", "arbitrary"),
        name="hyena_spectrum",
    )(tab, z, z, z, z, h_cos, h_sin)


def _inverse_kernel(tab_ref, p_ref, xm_ref, w_ref, skip_ref, gn_ref, o_ref, *scratch, inv_len, final):
    tj = xm_ref.shape[1]
    tab = _table_rows(tab_ref, pl.program_id(1), tj)
    for r in range(RADIX):
        y = jnp.dot(tab(r), p_ref[r], preferred_element_type=F32)
        y = y + jnp.dot(tab(RADIX + r), p_ref[RADIX + r], preferred_element_type=F32)
        out = xm_ref[r].astype(F32) * (y * inv_len + w_ref[r].astype(F32) * skip_ref[...])
        if final:
            out = _rms(out, gn_ref[...])
            for c in range(GROUP_W // LANE):
                scratch[0][c, pl.ds(r, tj, stride=RADIX), :] = out[:, c * LANE:(c + 1) * LANE]
        else:
            o_ref[r] = out.astype(o_ref.dtype)
    if final:
        for c in range(GROUP_W // LANE):
            o_ref[:, c * LANE:(c + 1) * LANE] = scratch[0][c].astype(o_ref.dtype)


def _inverse(tab_t, pcs, xm, w, skip, gn, bsz, seq, final):
    q = seq // RADIX
    tj = min(DFT_ROWS, q)
    nj = q // tj
    tile = pl.BlockSpec((None, RADIX, tj, GROUP_W), lambda b, i: (b, 0, i, 0))
    vec = pl.BlockSpec((1, GROUP_W), lambda b, i: (0, 0))
    if final:
        out_spec = pl.BlockSpec((RADIX * tj, GROUP_W), lambda b, i: (b * nj + i, 0))
        out_shape = jax.ShapeDtypeStruct((bsz * seq, GROUP_W), BF16)
        scratch = [pltpu.VMEM((GROUP_W // LANE, RADIX * tj, LANE), F32)]
    else:
        out_spec, out_shape, scratch = tile, jax.ShapeDtypeStruct((bsz, RADIX, q, GROUP_W), BF16), []
    return pl.pallas_call(
        functools.partial(_inverse_kernel, inv_len=1.0 / seq, final=final),
        grid=(bsz, nj),
        in_specs=[_resident((2 * RADIX, q, q), lambda b, i: (0, 0, 0)),
                  pl.BlockSpec((None, 2 * RADIX, q, GROUP_W), lambda b, i: (b, 0, 0, 0)),
                  tile, tile, vec, vec],
        out_specs=out_spec,
        out_shape=out_shape,
        scratch_shapes=scratch,
        compiler_params=_cp("parallel", "arbitrary"),
        name="hyena_inverse",
    )(tab_t, pcs, xm, w, skip, gn)


def _out_kernel(ya_ref, yb_ref, yc_ref, yd_ref, w_ref, x_ref, o_ref):
    acc = x_ref[...]
    for g, y_ref in enumerate((ya_ref, yb_ref, yc_ref, yd_ref)):
        acc = acc + jnp.dot(y_ref[...], w_ref[g * GROUP_W:(g + 1) * GROUP_W, :], preferred_element_type=F32)
    o_ref[...] = acc


def _out_proj(ys, w, layer, x):
    m, n = x.shape
    tm = min(512, m)
    piece = pl.BlockSpec((tm, GROUP_W), lambda i: (i, 0))
    xs = pl.BlockSpec((tm, n), lambda i: (i, 0))
    return pl.pallas_call(
        _out_kernel,
        grid=(m // tm,),
        in_specs=[piece, piece, piece, piece, _resident((None,) + w.shape[1:], lambda i: (layer, 0, 0)), xs],
        out_specs=xs,
        out_shape=jax.ShapeDtypeStruct((m, n), F32),
        compiler_params=_cp("parallel"),
        name="out_proj",
    )(*ys, w, x)


def _attn_kernel(q_ref, k_ref, v_ref, wo_ref, x_ref, o_ref, att_ref):
    heads = [slice(h * XA_HD, (h + 1) * XA_HD) for h in range(XA_HEADS)]
    scores = [lax.dot_general(q_ref[:, hs], k_ref[:, hs], NT, preferred_element_type=F32) for hs in heads]
    for hs, s in zip(heads, scores):
        s = s * (XA_HD ** -0.5)
        e = jnp.exp(s - jnp.max(s, axis=-1, keepdims=True))
        p = e / jnp.sum(e, axis=-1, keepdims=True)
        att_ref[:, hs] = jnp.dot(p.astype(BF16), v_ref[:, hs], preferred_element_type=F32).astype(BF16)
    o_ref[...] = x_ref[...] + jnp.dot(att_ref[...], wo_ref[...], preferred_element_type=F32)


def _attn(q, kv, wo, layer, x, seq):
    m, d = x.shape
    tm = min(512, seq)
    per = seq // tm
    xs = pl.BlockSpec((tm, d), lambda i: (i, 0))
    return pl.pallas_call(
        _attn_kernel,
        grid=(m // tm,),
        in_specs=[xs,
                  pl.BlockSpec((N_MEM, d), lambda i: (i // per, 0)),
                  pl.BlockSpec((N_MEM, d), lambda i: (i // per, 1)),
                  _resident((None, d, d), lambda i: (layer, 0, 0)),
                  xs],
        out_specs=xs,
        out_shape=jax.ShapeDtypeStruct((m, d), F32),
        scratch_shapes=[pltpu.VMEM((tm, d), BF16)],
        compiler_params=_cp("parallel"),
        name="cross_attn",
    )(q, kv, kv, wo, x)


def _swiglu_kernel(x_hbm, g_ref, wg_ref, wu_ref, wd_ref, fg_ref, o_ref, x_ref, hn_ref, sem, *, final, row_tiles):
    i, f = pl.program_id(0), pl.program_id(1)
    tm = x_ref.shape[0]

    def x_copy(tile):
        return pltpu.make_async_copy(x_hbm.at[pl.ds(pl.multiple_of(tile * tm, tm), tm), :], x_ref, sem)

    @pl.when((i == 0) & (f == 0))
    def _():
        x_copy(0).start()

    @pl.when(f == 0)
    def _():
        x_copy(i).wait()
        _norm_rows_to(x_ref, g_ref, hn_ref)
        o_ref[...] = x_ref[...]

    @pl.when((f == 1) & (i + 1 < row_tiles))
    def _():
        x_copy(i + 1).start()

    hn = hn_ref[...]
    gate = jnp.dot(hn, wg_ref[...], preferred_element_type=F32)
    up = jnp.dot(hn, wu_ref[...], preferred_element_type=F32)
    act = (gate * jax.nn.sigmoid(gate) * up).astype(BF16)
    o_ref[...] += jnp.dot(act, wd_ref[...], preferred_element_type=F32)

    if final:
        @pl.when(f == pl.num_programs(1) - 1)
        def _():
            _norm_rows_to(o_ref, fg_ref, o_ref)


def _swiglu(x, g, w_gu, w_d, layer, final_g, final):
    m, d = x.shape
    dff = w_d.shape[1]
    tm, tf = min(1024, m), 512
    nf = dff // tf
    assert nf >= 2
    vec = pl.BlockSpec((1, d), lambda i, f: (0, 0))
    return pl.pallas_call(
        functools.partial(_swiglu_kernel, final=final, row_tiles=m // tm),
        grid=(m // tm, nf),
        in_specs=[pl.BlockSpec(memory_space=pl.ANY), vec,
                  pl.BlockSpec((None, d, tf), lambda i, f: (layer, 0, f)),
                  pl.BlockSpec((None, d, tf), lambda i, f: (layer, 0, nf + f)),
                  pl.BlockSpec((None, tf, d), lambda i, f: (layer, f, 0)),
                  vec],
        out_specs=pl.BlockSpec((tm, d), lambda i, f: (i, 0)),
        out_shape=jax.ShapeDtypeStruct((m, d), F32),
        scratch_shapes=[pltpu.VMEM((tm, d), F32), pltpu.VMEM((tm, d), BF16), pltpu.SemaphoreType.DMA(())],
        compiler_params=_cp("arbitrary", "arbitrary"),
        name="swiglu",
    )(x, g.reshape(1, d), w_gu, w_gu, w_d, final_g.reshape(1, d))


def _angles(rows, cols, period):
    m = (rows[..., :, None] * cols[..., None, :]) % period
    return m.astype(F32) * (2.0 * math.pi / period)


def _cos_sin(ang, sin_sign=1.0):
    return jnp.concatenate([jnp.cos(ang), sin_sign * jnp.sin(ang)], axis=0).astype(BF16)


def _tables(seq):
    q = seq // RADIX
    base = jnp.arange(q, dtype=jnp.int32)
    times = RADIX * base[None, :] + jnp.arange(RADIX, dtype=jnp.int32)[:, None]
    rep = jnp.broadcast_to(base, (RADIX, q))
    fnet_tab = _cos_sin(_angles(rep, times, seq))
    ch = jnp.arange(GROUP_W, dtype=jnp.int32)
    same = (ch[:, None] // FNET_CH) == (ch[None, :] // FNET_CH)
    angc = _angles(ch % FNET_CH, ch % FNET_CH, FNET_CH)
    cc = jnp.where(same, jnp.cos(angc), 0.0).astype(BF16)
    sc = jnp.where(same, jnp.sin(angc), 0.0).astype(BF16)
    hy_tab = _cos_sin(_angles(2 * rep + 1, times, 4 * seq))
    hy_tab_t = jnp.swapaxes(hy_tab, 1, 2)
    return fnet_tab, cc, sc, hy_tab, hy_tab_t


def _position_features(seq):
    pos = jnp.arange(seq, dtype=F32)
    t = pos / seq
    f = jnp.linspace(1e-4, HY_BANDS - 1, HY_BANDS, dtype=F32)
    ang = (2.0 * math.pi * t)[:, None] * f[None, :]
    feats = jnp.concatenate([t[:, None], jnp.cos(ang), -jnp.sin(ang)], axis=-1)
    return jnp.pad(feats, ((0, 0), (0, LANE - HY_EMB)))


def kernel(x, mem, norm_g, w_in, gla_gk_w, gla_gk_b, gla_norm_g, hy_conv_w, hy_ffn_w1, hy_ffn_b1, hy_ffn_w2, hy_ffn_b2, hy_ffn_w3, hy_sin_freq, hy_decay, hy_skip, sc_conv_w, grp_norm_g, w_out, mem_norm_g, w_xq, w_xkv, w_xo, w_gate_up, w_down, final_norm_g):
    bsz, seq, d = x.shape
    depth = norm_g.shape[0]
    m = bsz * seq
    assert d == D_MODEL and seq % (RADIX * GLA_CHUNK) == 0 and mem.shape[1] == N_MEM

    q_w, k_w, v_w, g_w, lr_w, uf_w, uh_w, us_w = jnp.split(
        w_in.astype(BF16), [256, 512, 1024, 1536, 1568, 2080, 3616], axis=-1)
    w_p = jnp.concatenate([uh_w, us_w, q_w, k_w, v_w, g_w, uf_w], axis=-1)
    w_lr = jnp.pad(lr_w, ((0, 0), (0, 0), (0, LANE - 2 * GLA_RANK)))
    wgf = jnp.pad(gla_gk_w[:, 0], ((0, 0), (0, LANE - GLA_RANK), (0, 0))).astype(BF16)
    wgb = jnp.pad(gla_gk_w[:, 1], ((0, 0), (GLA_RANK, LANE - 2 * GLA_RANK), (0, 0))).astype(BF16)
    gkb = gla_gk_b.astype(F32)
    w_out_b, w_xq_b, w_xkv_b, w_xo_b = (t.astype(BF16) for t in (w_out, w_xq, w_xkv, w_xo))
    w_gu_b, w_d_b = w_gate_up.astype(BF16), w_down.astype(BF16)
    fnet_tab, cc, sc, hy_tab, hy_tab_t = _tables(seq)
    qlen = seq // RADIX
    feats = _position_features(seq).reshape(qlen, RADIX, LANE).transpose(1, 0, 2).reshape(seq, LANE)

    w1 = jnp.pad(hy_ffn_w1.astype(F32), ((0, 0), (0, LANE - HY_EMB), (0, 0)))
    fa, fb = _filters(feats, w1, hy_ffn_b1.reshape(depth, 1, HY_FFN).astype(F32), hy_ffn_w2.astype(F32),
                      hy_ffn_b2.reshape(depth, 1, HY_FFN).astype(F32), hy_ffn_w3.astype(F32),
                      hy_sin_freq.reshape(depth, 1, HY_FFN).astype(F32),
                      hy_decay.reshape(depth, 1, 4 * GROUP_W).astype(F32), seq)
    res_major = lambda t: t.reshape(depth, RADIX, qlen, t.shape[-1])
    h_cos = _filter_spectrum(hy_tab, res_major(fa), seq)
    h_sin = _filter_spectrum(hy_tab, res_major(fb), seq)

    mem2 = mem.reshape(bsz * N_MEM, d)
    xf = x.reshape(m, d)
    for l in range(depth):
        p, lr = _norm_matmul(xf, norm_g[l, 0], w_p, l, 1024, 1280, w_side=w_lr)
        y_a = _gla(p, lr, wgf[l], wgb[l], gkb[l, 0:1], gkb[l, 1:2], gla_norm_g[l].reshape(1, GLA_DV).astype(F32),
                   bsz, seq)
        hv, hx1, hx2, uf, y_d = _short(p, hy_conv_w[l].astype(F32), sc_conv_w[l].astype(F32),
                                       grp_norm_g[l, 2:3].astype(F32), bsz, seq)
        y_b = _fnet(uf, fnet_tab, cc, sc, grp_norm_g[l, 0:1].astype(F32), bsz, seq)
        gn_c = grp_norm_g[l, 1:2].astype(F32)
        pcs = _conv_spectrum(hy_tab, hv, h_cos, h_sin, l, 0, bsz, seq)
        z = _inverse(hy_tab_t, pcs, hx1, hv, hy_skip[l, 0:1].astype(F32), gn_c, bsz, seq, final=False)
        pcs = _conv_spectrum(hy_tab, z, h_cos, h_sin, l, 1, bsz, seq)
        y_c = _inverse(hy_tab_t, pcs, hx2, z, hy_skip[l, 1:2].astype(F32), gn_c, bsz, seq, final=True)
        xf = _out_proj((y_a, y_b, y_c, y_d), w_out_b, l, xf)
        q = _norm_matmul(xf, norm_g[l, 1], w_xq_b, l, 1024, D_MODEL)
        kv = _norm_matmul(mem2, mem_norm_g, w_xkv_b, l, 1024, 1024)
        xf = _attn(q, kv, w_xo_b, l, xf, seq)
        xf = _swiglu(xf, norm_g[l, 2], w_gu_b, w_d_b, l, final_norm_g, final=(l == depth - 1))
    return xf.reshape(bsz, seq, d)
```

```python
import functools
import math

import jax
import jax.numpy as jnp
from jax import lax
from jax.experimental import pallas as pl
from jax.experimental.pallas import tpu as pltpu

F32 = jnp.float32
BF16 = jnp.bfloat16
EPS = 1e-6

D_MODEL = 2048
GROUP_W = 512
GLA_HEADS = 4
GLA_DK = 64
GLA_DV = 128
GLA_RANK = 16
GLA_GATE_NORM = 16.0
GLA_CHUNK = 64
FNET_CH = 128
HY_BANDS = 16
HY_EMB = 2 * HY_BANDS + 1
HY_FFN = 64
XA_HEADS = 4
XA_HD = D_MODEL // XA_HEADS
N_MEM = 256
D_FF = 5632
RADIX = 4
DFT_ROWS = 512
LANE = 128
BF16_ROWS = 16
VMEM_LIMIT = 56 * 1024 * 1024

GLA_QK = GLA_HEADS * GLA_DK
P_UH, P_US, P_Q, P_K, P_V, P_G, P_UF = 0, 1536, 3072, 3328, 3584, 4096, 4608
P_W = 5120

NT = (((1,), (1,)), ((), ()))
TN = (((0,), (0,)), ((), ()))


def _cp(*sem):
    return pltpu.CompilerParams(dimension_semantics=sem, vmem_limit_bytes=VMEM_LIMIT)


def _resident(shape, index_map):
    return pl.BlockSpec(shape, index_map, pipeline_mode=pl.Buffered(1))


def _rms(x, g):
    ms = jnp.mean(x * x, axis=-1, keepdims=True)
    return x * lax.rsqrt(ms + EPS) * g


def _norm_rows_to(x_ref, g_ref, hn_ref):
    rows = min(256, x_ref.shape[0])
    n = x_ref.shape[0] // rows

    def body(i, c):
        r = pl.ds(pl.multiple_of(i * rows, rows), rows)
        hn_ref[r, :] = _rms(x_ref[r, :], g_ref[...]).astype(hn_ref.dtype)
        return c

    lax.fori_loop(0, n, body, 0)


def _norm_matmul_kernel(x_hbm, g_ref, w_ref, *rest, with_side, row_tiles, col_tiles):
    if with_side:
        w2_ref, o_ref, o2_ref, hn_ref, x_ref, sem = rest
    else:
        o_ref, hn_ref, x_ref, sem = rest
    i, j = pl.program_id(0), pl.program_id(1)
    part = x_ref.shape[1]
    c = i * col_tiles + j

    def x_copy(k):
        rows = pl.ds(pl.multiple_of(k * part, part), part)
        return pltpu.make_async_copy(x_hbm.at[rows, :], x_ref.at[k % 2], sem.at[k % 2])

    @pl.when(c == 0)
    def _():
        x_copy(0).start()

    @pl.when(i < row_tiles)
    def _():
        x_copy(c).wait()

    @pl.when(c + 1 < row_tiles * col_tiles)
    def _():
        x_copy(c + 1).start()

    def norm_part():
        g = g_ref[...]
        for r in range(0, part, BF16_ROWS):
            rows = pl.ds(pl.multiple_of(j * part + r, BF16_ROWS), BF16_ROWS)
            hn_ref[i % 2, rows, :] = _rms(x_ref[c % 2, r:r + BF16_ROWS, :], g).astype(BF16)

    def matmul():
        o_ref[...] = jnp.dot(hn_ref[(i + 1) % 2], w_ref[...], preferred_element_type=F32).astype(o_ref.dtype)

    @pl.when(i == 0)
    def _():
        norm_part()

    @pl.when((i > 0) & (i < row_tiles))
    def _():
        matmul()
        norm_part()

    @pl.when(i == row_tiles)
    def _():
        matmul()

    if with_side:
        @pl.when((i > 0) & (j == 0))
        def _():
            o2_ref[...] = jnp.dot(hn_ref[(i + 1) % 2], w2_ref[...], preferred_element_type=F32)


def _norm_matmul(x, g, w, layer, tm, tn, w_side=None):
    m, k = x.shape
    n = w.shape[2]
    tm, tn = min(tm, m), min(tn, n)
    ni, nj = m // tm, n // tn
    assert (tm // nj) % BF16_ROWS == 0
    prev = lambda i: jnp.maximum(i - 1, 0)
    in_specs = [
        pl.BlockSpec(memory_space=pl.ANY),
        pl.BlockSpec((1, k), lambda i, j: (0, 0)),
        (_resident if tn == n else pl.BlockSpec)((None, k, tn), lambda i, j: (layer, 0, j)),
    ]
    out_shape = [jax.ShapeDtypeStruct((m, n), BF16)]
    out_specs = [pl.BlockSpec((tm, tn), lambda i, j: (prev(i), jnp.where(i == 0, 0, j)))]
    args = [x, g.reshape(1, k), w]
    if w_side is not None:
        n2 = w_side.shape[2]
        in_specs.append(pl.BlockSpec((None, k, n2), lambda i, j: (layer, 0, 0)))
        out_shape.append(jax.ShapeDtypeStruct((m, n2), F32))
        out_specs.append(pl.BlockSpec((tm, n2), lambda i, j: (prev(i), 0)))
        args.append(w_side)
    res = pl.pallas_call(
        functools.partial(_norm_matmul_kernel, with_side=w_side is not None, row_tiles=ni, col_tiles=nj),
        grid=(ni + 1, nj),
        in_specs=in_specs,
        out_specs=out_specs,
        out_shape=out_shape,
        scratch_shapes=[pltpu.VMEM((2, tm, k), BF16), pltpu.VMEM((2, tm // nj, k), F32),
                        pltpu.SemaphoreType.DMA((2,))],
        compiler_params=_cp("arbitrary", "arbitrary"),
        name="norm_matmul",
    )(*args)
    return res if w_side is not None else res[0]


def _gla_pass(fwd, blk, q_ref, k_ref, v_ref, lr_ref, g_ref, wg_ref, bg_ref, gn_ref, y_ref, st_ref, ob_ref):
    ns, ts = q_ref.shape[0], q_ref.shape[1]
    c_sz = GLA_CHUNK
    nc = ts // c_sz
    ii = lax.broadcasted_iota(jnp.int32, (c_sz, c_sz), 0)
    jj = lax.broadcasted_iota(jnp.int32, (c_sz, c_sz), 1)
    incl = (jj <= ii) if fwd else (jj >= ii)
    mask = incl if fwd else (jj > ii)
    tri = jnp.where(incl, 1.0, 0.0).astype(BF16)
    gn = gn_ref[...]
    chunk = lambda c: slice(c * c_sz, (c + 1) * c_sz)
    order = list(range(nc)) if fwd else list(reversed(range(nc)))
    problems = [(s, h) for s in range(ns) for h in range(GLA_HEADS)]
    pair = lambda h: slice((h // 2) * LANE, (h // 2 + 1) * LANE)
    val = lambda h: slice(h * GLA_DV, (h + 1) * GLA_DV)
    first = lax.broadcasted_iota(jnp.int32, (c_sz, LANE), 1) < GLA_DK
    own = lambda h: first if h % 2 == 0 else jnp.logical_not(first)

    qe, ke, qi, ku, dec = {}, {}, {}, {}, {}
    for s in range(ns):
        x = jnp.dot(lr_ref[s].astype(BF16), wg_ref[...], preferred_element_type=F32) + bg_ref[...]
        gk = (jnp.minimum(x, 0.0) - jnp.log1p(jnp.exp(-jnp.abs(x)))) * (1.0 / GLA_GATE_NORM)
        hi = gk.astype(BF16)
        lo = (gk - hi.astype(F32)).astype(BF16)
        for c in range(nc):
            r = chunk(c)
            b = jnp.dot(tri, hi[r], preferred_element_type=F32) + jnp.dot(tri, lo[r], preferred_element_type=F32)
            bref = b[c_sz // 2:c_sz // 2 + 1] if fwd else b[c_sz // 2 - 1:c_sz // 2]
            btot = b[c_sz - 1:c_sz] if fwd else b[0:1]
            qe_f = q_ref[s, r, :].astype(F32) * (GLA_DK ** -0.5) * jnp.exp(b - bref)
            ke_f = k_ref[s, r, :].astype(F32) * jnp.exp(bref - b)
            qe[s, c] = qe_f
            ke[s, c] = ke_f.astype(BF16)
            qi[s, c] = (qe_f * jnp.exp(bref)).astype(BF16)
            ku[s, c] = ke_f * jnp.exp(btot - bref)
            dec[s, c] = jnp.exp(btot)

    kv = {}
    for c in order:
        for s, h in problems:
            kv[s, h, c] = lax.dot_general(v_ref[s, chunk(c), val(h)],
                                          jnp.where(own(h), ku[s, c][:, pair(h)], 0.0).astype(BF16), TN,
                                          preferred_element_type=F32)
    st = {(s, h): st_ref[s, h] for s, h in problems}
    st_in = {}
    for c in order:
        for s, h in problems:
            st_in[s, h, c] = st[s, h].astype(BF16)
            st[s, h] = st[s, h] * dec[s, c][:, pair(h)] + kv[s, h, c]
    for s, h in problems:
        st_ref[s, h] = st[s, h]
    for c in order:
        r = chunk(c)
        rows = pl.ds(pl.multiple_of(blk * ts + c * c_sz, c_sz), c_sz)
        scores = {}
        for s, h in problems:
            sc = lax.dot_general(jnp.where(own(h), qe[s, c][:, pair(h)], 0.0).astype(BF16), ke[s, c][:, pair(h)], NT,
                                 preferred_element_type=F32)
            scores[s, h] = jnp.where(mask, sc, 0.0).astype(BF16)
        for s, h in problems:
            o = jnp.dot(scores[s, h], v_ref[s, r, val(h)], preferred_element_type=F32)
            o = o + lax.dot_general(qi[s, c][:, pair(h)], st_in[s, h, c], NT, preferred_element_type=F32)
            if fwd:
                tot = o + ob_ref[s, rows, val(h)]
                gate = g_ref[s, r, val(h)].astype(F32)
                y_ref[s, r, val(h)] = (_rms(tot, gn) * (gate * jax.nn.sigmoid(gate))).astype(y_ref.dtype)
            else:
                ob_ref[s, rows, val(h)] = o


def _gla_kernel(q_ref, k_ref, v_ref, lr_ref, g_ref, wgf_ref, wgb_ref, bgf_ref, bgb_ref, gn_ref,
                y_ref, st_ref, ob_ref, *, nb):
    n = pl.program_id(1)

    @pl.when((n == 0) | (n == nb))
    def _():
        st_ref[...] = jnp.zeros_like(st_ref)

    @pl.when(n < nb)
    def _():
        _gla_pass(False, nb - 1 - n, q_ref, k_ref, v_ref, lr_ref, g_ref, wgb_ref, bgb_ref, gn_ref, y_ref, st_ref, ob_ref)

    @pl.when(n >= nb)
    def _():
        _gla_pass(True, n - nb, q_ref, k_ref, v_ref, lr_ref, g_ref, wgf_ref, bgf_ref, gn_ref, y_ref, st_ref, ob_ref)


def _gla(p, lr, wgf, wgb, bgf, bgb, gn, bsz, seq):
    ts = min(512, seq)
    nb = seq // ts
    ns = 2 if bsz % 2 == 0 else 1
    blk = lambda n: jnp.where(n < nb, nb - 1 - n, n - nb)
    oblk = lambda n: jnp.where(n < nb, 0, n - nb)
    in_specs = [
        pl.BlockSpec((ns, ts, GLA_QK), lambda b, n: (b, blk(n), P_Q // GLA_QK)),
        pl.BlockSpec((ns, ts, GLA_QK), lambda b, n: (b, blk(n), P_K // GLA_QK)),
        pl.BlockSpec((ns, ts, GROUP_W), lambda b, n: (b, blk(n), P_V // GROUP_W)),
        pl.BlockSpec((ns, ts, LANE), lambda b, n: (b, blk(n), 0)),
        pl.BlockSpec((ns, ts, GROUP_W), lambda b, n: (b, oblk(n), P_G // GROUP_W)),
        pl.BlockSpec((LANE, GLA_QK), lambda b, n: (0, 0)),
        pl.BlockSpec((LANE, GLA_QK), lambda b, n: (0, 0)),
        pl.BlockSpec((1, GLA_QK), lambda b, n: (0, 0)),
        pl.BlockSpec((1, GLA_QK), lambda b, n: (0, 0)),
        pl.BlockSpec((1, GLA_DV), lambda b, n: (0, 0)),
    ]
    p3 = p.reshape(bsz, seq, P_W)
    y = pl.pallas_call(
        functools.partial(_gla_kernel, nb=nb),
        grid=(bsz // ns, 2 * nb),
        in_specs=in_specs,
        out_specs=pl.BlockSpec((ns, ts, GROUP_W), lambda b, n: (b, oblk(n), 0)),
        out_shape=jax.ShapeDtypeStruct((bsz, seq, GROUP_W), BF16),
        scratch_shapes=[pltpu.VMEM((ns, GLA_HEADS, GLA_DV, LANE), F32), pltpu.VMEM((ns, seq, GROUP_W), F32)],
        compiler_params=_cp("parallel", "arbitrary"),
        name="gla",
    )(p3, p3, p3, lr.reshape(bsz, seq, LANE), p3, wgf, wgb, bgf, bgb, gn)
    return y.reshape(bsz * seq, GROUP_W)


def _table_rows(tab_ref, tile, rows):
    window = pl.ds(pl.multiple_of(tile * rows, rows), rows)
    return lambda k: tab_ref[k, window, :]


def _butterflies(a, b):
    a02p, a02m, a13p, a13m = a[0] + a[2], a[0] - a[2], a[1] + a[3], a[1] - a[3]
    b02p, b02m, b13p, b13m = b[0] + b[2], b[0] - b[2], b[1] + b[3], b[1] - b[3]
    re = [a02p + a13p, a02m - b13m, a02p - a13p, a02m + b13m]
    nim = [b02p + b13p, b02m + a13m, b02p - b13p, b02m - a13m]
    return re, nim


def _stage_products(tab, z_refs, ab_ref):
    for k in range(2 * RADIX):
        ab_ref[k] = jnp.dot(tab(k), z_refs[k % RADIX][...], preferred_element_type=F32)


def _load_products(ab_ref, rows, cols):
    return ([ab_ref[r, rows, cols] for r in range(RADIX)], [ab_ref[RADIX + r, rows, cols] for r in range(RADIX)])


def _groups(n_rows, n_cols, cols_per_group=2 * LANE):
    return [(slice(r, r + BF16_ROWS), slice(c, c + cols_per_group))
            for r in range(0, n_rows, BF16_ROWS) for c in range(0, n_cols, cols_per_group)]


def _fnet_kernel(tab_ref, x0, x1, x2, x3, cc_ref, sc_ref, gn_ref, y_ref, ab_ref, bf_ref, *, scale):
    tq = y_ref.shape[1]
    _stage_products(_table_rows(tab_ref, pl.program_id(1), tq), (x0, x1, x2, x3), ab_ref)
    for rows, cols in _groups(tq, GROUP_W):
        re, nim = _butterflies(*_load_products(ab_ref, rows, cols))
        for m in range(RADIX):
            bf_ref[m, rows, cols] = re[m].astype(BF16)
            bf_ref[RADIX + m, rows, cols] = nim[m].astype(BF16)
    for m in range(RADIX):
        y = jnp.dot(bf_ref[m], cc_ref[...], preferred_element_type=F32)
        y = y - jnp.dot(bf_ref[RADIX + m], sc_ref[...], preferred_element_type=F32)
        y_ref[m] = _rms(y * scale, gn_ref[...]).astype(y_ref.dtype)


def _fnet(uf, tab, cc, sc, gn, bsz, seq):
    q = seq // RADIX
    tq = min(DFT_ROWS, q)
    nq = q // tq
    res = lambda r: pl.BlockSpec((None, None, q, GROUP_W), lambda b, i: (b, r, 0, 0))
    sq = pl.BlockSpec((GROUP_W, GROUP_W), lambda b, i: (0, 0))
    out = pl.pallas_call(
        functools.partial(_fnet_kernel, scale=1.0 / math.sqrt(seq * FNET_CH)),
        grid=(bsz, nq),
        in_specs=[_resident((2 * RADIX, q, q), lambda b, i: (0, 0, 0)),
                  res(0), res(1), res(2), res(3), sq, sq,
                  pl.BlockSpec((1, GROUP_W), lambda b, i: (0, 0))],
        out_specs=pl.BlockSpec((None, RADIX, tq, GROUP_W), lambda b, i: (b, 0, i, 0)),
        out_shape=jax.ShapeDtypeStruct((bsz, RADIX, q, GROUP_W), BF16),
        scratch_shapes=[pltpu.VMEM((2 * RADIX, tq, GROUP_W), F32), pltpu.VMEM((2 * RADIX, tq, GROUP_W), BF16)],
        compiler_params=_cp("parallel", "arbitrary"),
        name="fnet",
    )(tab, uf, uf, uf, uf, cc, sc, gn)
    return out.reshape(bsz * seq, GROUP_W)


def _conv3(main, prev_row, next_row, w):
    ts = main.shape[0]
    rid = lax.broadcasted_iota(jnp.int32, main.shape, 0)
    up = jnp.where(rid == 0, prev_row, pltpu.roll(main, 1, 0))
    dn = jnp.where(rid == ts - 1, next_row, pltpu.roll(main, ts - 1, 0))
    return up * w[0:1] + main * w[1:2] + dn * w[2:3]


def _short_kernel(hm_ref, hp_ref, hn_ref, sm_ref, sp_ref, sn_ref, uf_ref, hw_ref, sw_ref, gn_ref,
                  v_ref, x1_ref, x2_ref, ufr_ref, yd_ref, nat_ref, *, nblk):
    i = pl.program_id(0)
    keep_prev = jnp.where(i % nblk == 0, 0.0, 1.0)
    keep_next = jnp.where(i % nblk == nblk - 1, 0.0, 1.0)
    last = BF16_ROWS - 1
    w = GROUP_W
    ts = hm_ref.shape[0]
    per = w // LANE

    def neighbours(ref, halo_prev, halo_next, r0, rows, cols):
        above = (ref[r0 - 1:r0, cols].astype(F32) if r0 > 0
                 else halo_prev[last:last + 1, cols].astype(F32) * keep_prev)
        below = (ref[r0 + rows:r0 + rows + 1, cols].astype(F32) if r0 + rows < ts
                 else halo_next[0:1, cols].astype(F32) * keep_next)
        return above, below

    rows_h, lanes_h = 2 * BF16_ROWS, 2 * LANE
    for r0 in range(0, ts, rows_h):
        for c0 in range(0, 3 * w, lanes_h):
            cols = slice(c0, c0 + lanes_h)
            above, below = neighbours(hm_ref, hp_ref, hn_ref, r0, rows_h, cols)
            conv = _conv3(hm_ref[r0:r0 + rows_h, cols].astype(F32), above, below, hw_ref[:, cols])
            for k in range(lanes_h // LANE):
                nat_ref[c0 // LANE + k, r0:r0 + rows_h, :] = conv[:, k * LANE:(k + 1) * LANE]
    for c in range(per):
        nat_ref[3 * per + c] = uf_ref[:, c * LANE:(c + 1) * LANE].astype(F32)
    for g, dst in enumerate((v_ref, x1_ref, x2_ref, ufr_ref)):
        for r in range(RADIX):
            for c in range(per):
                piece = nat_ref[g * per + c, pl.ds(r, ts // RADIX, stride=RADIX), :]
                dst[r, :, c * LANE:(c + 1) * LANE] = piece.astype(dst.dtype)

    bcol, ccol, hcol = slice(0, w), slice(w, 2 * w), slice(2 * w, 3 * w)
    rows_s = BF16_ROWS
    for r0 in range(0, ts, rows_s):
        rs = slice(r0, r0 + rows_s)
        ca, cb = neighbours(sm_ref, sp_ref, sn_ref, r0, rows_s, ccol)
        ha, hb = neighbours(sm_ref, sp_ref, sn_ref, r0, rows_s, hcol)
        prod = sm_ref[rs, ccol].astype(F32) * sm_ref[rs, hcol].astype(F32)
        conv = _conv3(prod, ca * ha, cb * hb, sw_ref[...])
        yd_ref[rs, :] = _rms(sm_ref[rs, bcol].astype(F32) * conv, gn_ref[...]).astype(yd_ref.dtype)


def _short(p, hy_w, sc_w, gn, bsz, seq):
    m = bsz * seq
    ts = min(256, seq)
    nblk = seq // ts
    hb = ts // BF16_ROWS
    nh = m // BF16_ROWS
    wide = 3 * GROUP_W
    main = lambda cb: pl.BlockSpec((ts, wide), lambda i: (i, cb))
    prev = lambda cb: pl.BlockSpec((BF16_ROWS, wide), lambda i: (jnp.maximum(i * hb - 1, 0), cb))
    nxt = lambda cb: pl.BlockSpec((BF16_ROWS, wide), lambda i: (jnp.minimum((i + 1) * hb, nh - 1), cb))
    nat = pl.BlockSpec((ts, GROUP_W), lambda i: (i, 0))
    res = pl.BlockSpec((None, RADIX, ts // RADIX, GROUP_W), lambda i: (i // nblk, 0, i % nblk, 0))
    res_shape = jax.ShapeDtypeStruct((bsz, RADIX, seq // RADIX, GROUP_W), BF16)
    return pl.pallas_call(
        functools.partial(_short_kernel, nblk=nblk),
        grid=(m // ts,),
        in_specs=[main(0), prev(0), nxt(0), main(1), prev(1), nxt(1),
                  pl.BlockSpec((ts, GROUP_W), lambda i: (i, P_UF // GROUP_W)),
                  pl.BlockSpec((3, wide), lambda i: (0, 0)),
                  pl.BlockSpec((3, GROUP_W), lambda i: (0, 0)),
                  pl.BlockSpec((1, GROUP_W), lambda i: (0, 0))],
        out_specs=[res, res, res, res, nat],
        out_shape=[res_shape] * 4 + [jax.ShapeDtypeStruct((m, GROUP_W), BF16)],
        scratch_shapes=[pltpu.VMEM((4 * GROUP_W // LANE, ts, LANE), F32)],
        compiler_params=_cp("parallel"),
        name="short_conv",
    )(p, p, p, p, p, p, p, hy_w, sc_w, gn)


def _filter_kernel(feat_ref, w1_ref, b1_ref, w2_ref, b2_ref, w3_ref, fr_ref, dc_ref, a_ref, b_ref):
    hp = lax.Precision.HIGHEST
    feats = feat_ref[...]
    fr = fr_ref[...]
    h = jnp.sin(fr * (jnp.dot(feats, w1_ref[...], precision=hp, preferred_element_type=F32) + b1_ref[...]))
    h = jnp.sin(fr * (jnp.dot(h, w2_ref[...], precision=hp, preferred_element_type=F32) + b2_ref[...]))
    split = lambda t: (t.astype(BF16), (t - t.astype(BF16).astype(F32)).astype(BF16))
    (h_hi, h_lo), (w_hi, w_lo) = split(h), split(w3_ref[...])
    h = (jnp.dot(h_hi, w_hi, preferred_element_type=F32) + jnp.dot(h_hi, w_lo, preferred_element_type=F32)
         + jnp.dot(h_lo, w_hi, preferred_element_type=F32))
    h = h * jnp.exp(-feats[:, 0:1] * jnp.abs(dc_ref[...]))
    tl = h.shape[0]
    pos = lax.broadcasted_iota(jnp.int32, (tl, GROUP_W), 0) + pl.program_id(1) * tl
    for o in range(2):
        hf = h[:, (2 * o) * GROUP_W:(2 * o + 1) * GROUP_W]
        hb = jnp.where(pos == 0, 0.0, h[:, (2 * o + 1) * GROUP_W:(2 * o + 2) * GROUP_W])
        a_ref[:, o * GROUP_W:(o + 1) * GROUP_W] = (hf + hb).astype(a_ref.dtype)
        b_ref[:, o * GROUP_W:(o + 1) * GROUP_W] = (hb - hf).astype(b_ref.dtype)


def _filters(feats, w1, b1, w2, b2, w3, fr, dc, seq):
    depth = w1.shape[0]
    tl = min(512, seq)
    lay = lambda r, c: pl.BlockSpec((None, r, c), lambda l, i: (l, 0, 0))
    out = pl.BlockSpec((None, tl, 2 * GROUP_W), lambda l, i: (l, i, 0))
    return pl.pallas_call(
        _filter_kernel,
        grid=(depth, seq // tl),
        in_specs=[pl.BlockSpec((tl, LANE), lambda l, i: (i, 0)),
                  lay(LANE, HY_FFN), lay(1, HY_FFN), lay(HY_FFN, HY_FFN), lay(1, HY_FFN),
                  lay(HY_FFN, 4 * GROUP_W), lay(1, HY_FFN), lay(1, 4 * GROUP_W)],
        out_specs=[out, out],
        out_shape=[jax.ShapeDtypeStruct((depth, seq, 2 * GROUP_W), BF16)] * 2,
        compiler_params=_cp("parallel", "parallel"),
        name="hyena_filter",
    )(feats, w1, b1, w2, b2, w3, fr, dc)


def _hyena_classes(a, b):
    re, nim = _butterflies(a, b)
    return re, [nim[0], nim[1], -nim[2], -nim[3]]


def _spectrum_kernel(tab_ref, z0, z1, z2, z3, o_ref, ab_ref):
    tq = o_ref.shape[1]
    _stage_products(_table_rows(tab_ref, pl.program_id(1), tq), (z0, z1, z2, z3), ab_ref)
    for rows, cols in _groups(tq, GROUP_W):
        zr, zs = _hyena_classes(*_load_products(ab_ref, rows, cols))
        for m in range(RADIX):
            o_ref[m, rows, cols] = zr[m].astype(o_ref.dtype)
            o_ref[RADIX + m, rows, cols] = zs[m].astype(o_ref.dtype)


def _filter_spectrum(tab, filt, seq):
    depth, _, _, cols = filt.shape
    q = seq // RADIX
    tq = min(DFT_ROWS, q)
    ncb = cols // GROUP_W
    res = lambda r: pl.BlockSpec((None, None, q, GROUP_W), lambda n, i: (n // ncb, r, 0, n % ncb))
    return pl.pallas_call(
        _spectrum_kernel,
        grid=(depth * ncb, q // tq),
        in_specs=[_resident((2 * RADIX, q, q), lambda n, i: (0, 0, 0)), res(0), res(1), res(2), res(3)],
        out_specs=pl.BlockSpec((None, 2 * RADIX, tq, GROUP_W), lambda n, i: (n // ncb, 0, i, n % ncb)),
        out_shape=jax.ShapeDtypeStruct((depth, 2 * RADIX, q, cols), BF16),
        scratch_shapes=[pltpu.VMEM((2 * RADIX, tq, GROUP_W), F32)],
        compiler_params=_cp("parallel", "arbitrary"),
        name="filter_spectrum",
    )(tab, filt, filt, filt, filt)


def _conv_spectrum_kernel(tab_ref, z0, z1, z2, z3, hr_ref, hi_ref, p_ref, ab_ref):
    tq = p_ref.shape[1]
    _stage_products(_table_rows(tab_ref, pl.program_id(1), tq), (z0, z1, z2, z3), ab_ref)
    for rows, cols in _groups(tq, GROUP_W):
        zr, zs = _hyena_classes(*_load_products(ab_ref, rows, cols))
        gr, gi = [], []
        for m in range(RADIX):
            hr, hi = hr_ref[m, rows, cols].astype(F32), hi_ref[m, rows, cols].astype(F32)
            gr.append(zr[m] * hr + zs[m] * hi)
            gi.append(zr[m] * hi - zs[m] * hr)
        s02, d02, s13, d13 = gr[0] + gr[2], gr[0] - gr[2], gr[1] + gr[3], gr[1] - gr[3]
        t02, e02, t13, e13 = gi[0] + gi[2], gi[0] - gi[2], gi[1] + gi[3], gi[1] - gi[3]
        pc = [s02 + s13, d02 - t13, s02 - s13, d02 + t13]
        ps = [-e02 - e13, -t02 - d13, e13 - e02, d13 - t02]
        for r in range(RADIX):
            p_ref[r, rows, cols] = pc[r].astype(p_ref.dtype)
            p_ref[RADIX + r, rows, cols] = ps[r].astype(p_ref.dtype)


def _conv_spectrum(tab, z, h_cos, h_sin, layer, order, bsz, seq):
    q = seq // RADIX
    tq = min(DFT_ROWS, q)
    res = lambda r: pl.BlockSpec((None, None, q, GROUP_W), lambda b, i: (b, r, 0, 0))
    filt = lambda half: pl.BlockSpec((None, RADIX, tq, GROUP_W), lambda b, i: (layer, half, i, order))
    return pl.pallas_call(
        _conv_spectrum_kernel,
        grid=(bsz, q // tq),
        in_specs=[_resident((2 * RADIX, q, q), lambda b, i: (0, 0, 0)), res(0), res(1), res(2), res(3),
                  filt(0), filt(1)],
        out_specs=pl.BlockSpec((None, 2 * RADIX, tq, GROUP_W), lambda b, i: (b, 0, i, 0)),
        out_shape=jax.ShapeDtypeStruct((bsz, 2 * RADIX, q, GROUP_W), BF16),
        scratch_shapes=[pltpu.VMEM((2 * RADIX, tq, GROUP_W), F32)],
        compiler_params=_cp("parallel", "arbitrary"),
        name="hyena_spectrum",
    )(tab, z, z, z, z, h_cos, h_sin)


def _inverse_kernel(tab_ref, p_ref, xm_ref, w_ref, skip_ref, gn_ref, o_ref, *scratch, inv_len, final):
    tj = xm_ref.shape[1]
    tab = _table_rows(tab_ref, pl.program_id(1), tj)
    for r in range(RADIX):
        y = jnp.dot(tab(r), p_ref[r], preferred_element_type=F32)
        y = y + jnp.dot(tab(RADIX + r), p_ref[RADIX + r], preferred_element_type=F32)
        out = xm_ref[r].astype(F32) * (y * inv_len + w_ref[r].astype(F32) * skip_ref[...])
        if final:
            out = _rms(out, gn_ref[...])
            for c in range(GROUP_W // LANE):
                scratch[0][c, pl.ds(r, tj, stride=RADIX), :] = out[:, c * LANE:(c + 1) * LANE]
        else:
            o_ref[r] = out.astype(o_ref.dtype)
    if final:
        for c in range(GROUP_W // LANE):
            o_ref[:, c * LANE:(c + 1) * LANE] = scratch[0][c].astype(o_ref.dtype)


def _inverse(tab_t, pcs, xm, w, skip, gn, bsz, seq, final):
    q = seq // RADIX
    tj = min(DFT_ROWS, q)
    nj = q // tj
    tile = pl.BlockSpec((None, RADIX, tj, GROUP_W), lambda b, i: (b, 0, i, 0))
    vec = pl.BlockSpec((1, GROUP_W), lambda b, i: (0, 0))
    if final:
        out_spec = pl.BlockSpec((RADIX * tj, GROUP_W), lambda b, i: (b * nj + i, 0))
        out_shape = jax.ShapeDtypeStruct((bsz * seq, GROUP_W), BF16)
        scratch = [pltpu.VMEM((GROUP_W // LANE, RADIX * tj, LANE), F32)]
    else:
        out_spec, out_shape, scratch = tile, jax.ShapeDtypeStruct((bsz, RADIX, q, GROUP_W), BF16), []
    return pl.pallas_call(
        functools.partial(_inverse_kernel, inv_len=1.0 / seq, final=final),
        grid=(bsz, nj),
        in_specs=[_resident((2 * RADIX, q, q), lambda b, i: (0, 0, 0)),
                  pl.BlockSpec((None, 2 * RADIX, q, GROUP_W), lambda b, i: (b, 0, 0, 0)),
                  tile, tile, vec, vec],
        out_specs=out_spec,
        out_shape=out_shape,
        scratch_shapes=scratch,
        compiler_params=_cp("parallel", "arbitrary"),
        name="hyena_inverse",
    )(tab_t, pcs, xm, w, skip, gn)


def _out_kernel(ya_ref, yb_ref, yc_ref, yd_ref, w_ref, x_ref, o_ref):
    acc = x_ref[...]
    for g, y_ref in enumerate((ya_ref, yb_ref, yc_ref, yd_ref)):
        acc = acc + jnp.dot(y_ref[...], w_ref[g * GROUP_W:(g + 1) * GROUP_W, :], preferred_element_type=F32)
    o_ref[...] = acc


def _out_proj(ys, w, layer, x):
    m, n = x.shape
    tm = min(512, m)
    piece = pl.BlockSpec((tm, GROUP_W), lambda i: (i, 0))
    xs = pl.BlockSpec((tm, n), lambda i: (i, 0))
    return pl.pallas_call(
        _out_kernel,
        grid=(m // tm,),
        in_specs=[piece, piece, piece, piece, _resident((None,) + w.shape[1:], lambda i: (layer, 0, 0)), xs],
        out_specs=xs,
        out_shape=jax.ShapeDtypeStruct((m, n), F32),
        compiler_params=_cp("parallel"),
        name="out_proj",
    )(*ys, w, x)


def _attn_kernel(q_ref, k_ref, v_ref, wo_ref, x_ref, o_ref, att_ref):
    heads = [slice(h * XA_HD, (h + 1) * XA_HD) for h in range(XA_HEADS)]
    scores = [lax.dot_general(q_ref[:, hs], k_ref[:, hs], NT, preferred_element_type=F32) for hs in heads]
    for hs, s in zip(heads, scores):
        s = s * (XA_HD ** -0.5)
        e = jnp.exp(s - jnp.max(s, axis=-1, keepdims=True))
        p = e / jnp.sum(e, axis=-1, keepdims=True)
        att_ref[:, hs] = jnp.dot(p.astype(BF16), v_ref[:, hs], preferred_element_type=F32).astype(BF16)
    o_ref[...] = x_ref[...] + jnp.dot(att_ref[...], wo_ref[...], preferred_element_type=F32)


def _attn(q, kv, wo, layer, x, seq):
    m, d = x.shape
    tm = min(512, seq)
    per = seq // tm
    xs = pl.BlockSpec((tm, d), lambda i: (i, 0))
    return pl.pallas_call(
        _attn_kernel,
        grid=(m // tm,),
        in_specs=[xs,
                  pl.BlockSpec((N_MEM, d), lambda i: (i // per, 0)),
                  pl.BlockSpec((N_MEM, d), lambda i: (i // per, 1)),
                  _resident((None, d, d), lambda i: (layer, 0, 0)),
                  xs],
        out_specs=xs,
        out_shape=jax.ShapeDtypeStruct((m, d), F32),
        scratch_shapes=[pltpu.VMEM((tm, d), BF16)],
        compiler_params=_cp("parallel"),
        name="cross_attn",
    )(q, kv, kv, wo, x)


def _swiglu_kernel(x_hbm, g_ref, wg_ref, wu_ref, wd_ref, fg_ref, o_ref, x_ref, hn_ref, sem, *, final, row_tiles):
    i, f = pl.program_id(0), pl.program_id(1)
    tm = x_ref.shape[0]

    def x_copy(tile):
        return pltpu.make_async_copy(x_hbm.at[pl.ds(pl.multiple_of(tile * tm, tm), tm), :], x_ref, sem)

    @pl.when((i == 0) & (f == 0))
    def _():
        x_copy(0).start()

    @pl.when(f == 0)
    def _():
        x_copy(i).wait()
        _norm_rows_to(x_ref, g_ref, hn_ref)
        o_ref[...] = x_ref[...]

    @pl.when((f == 1) & (i + 1 < row_tiles))
    def _():
        x_copy(i + 1).start()

    hn = hn_ref[...]
    gate = jnp.dot(hn, wg_ref[...], preferred_element_type=F32)
    up = jnp.dot(hn, wu_ref[...], preferred_element_type=F32)
    act = (gate * jax.nn.sigmoid(gate) * up).astype(BF16)
    o_ref[...] += jnp.dot(act, wd_ref[...], preferred_element_type=F32)

    if final:
        @pl.when(f == pl.num_programs(1) - 1)
        def _():
            _norm_rows_to(o_ref, fg_ref, o_ref)


def _swiglu(x, g, w_gu, w_d, layer, final_g, final):
    m, d = x.shape
    dff = w_d.shape[1]
    tm, tf = min(1024, m), 512
    nf = dff // tf
    assert nf >= 2
    vec = pl.BlockSpec((1, d), lambda i, f: (0, 0))
    return pl.pallas_call(
        functools.partial(_swiglu_kernel, final=final, row_tiles=m // tm),
        grid=(m // tm, nf),
        in_specs=[pl.BlockSpec(memory_space=pl.ANY), vec,
                  pl.BlockSpec((None, d, tf), lambda i, f: (layer, 0, f)),
                  pl.BlockSpec((None, d, tf), lambda i, f: (layer, 0, nf + f)),
                  pl.BlockSpec((None, tf, d), lambda i, f: (layer, f, 0)),
                  vec],
        out_specs=pl.BlockSpec((tm, d), lambda i, f: (i, 0)),
        out_shape=jax.ShapeDtypeStruct((m, d), F32),
        scratch_shapes=[pltpu.VMEM((tm, d), F32), pltpu.VMEM((tm, d), BF16), pltpu.SemaphoreType.DMA(())],
        compiler_params=_cp("arbitrary", "arbitrary"),
        name="swiglu",
    )(x, g.reshape(1, d), w_gu, w_gu, w_d, final_g.reshape(1, d))


def _angles(rows, cols, period):
    m = (rows[..., :, None] * cols[..., None, :]) % period
    return m.astype(F32) * (2.0 * math.pi / period)


def _cos_sin(ang, sin_sign=1.0):
    return jnp.concatenate([jnp.cos(ang), sin_sign * jnp.sin(ang)], axis=0).astype(BF16)


def _tables(seq):
    q = seq // RADIX
    base = jnp.arange(q, dtype=jnp.int32)
    times = RADIX * base[None, :] + jnp.arange(RADIX, dtype=jnp.int32)[:, None]
    rep = jnp.broadcast_to(base, (RADIX, q))
    fnet_tab = _cos_sin(_angles(rep, times, seq))
    ch = jnp.arange(GROUP_W, dtype=jnp.int32)
    same = (ch[:, None] // FNET_CH) == (ch[None, :] // FNET_CH)
    angc = _angles(ch % FNET_CH, ch % FNET_CH, FNET_CH)
    cc = jnp.where(same, jnp.cos(angc), 0.0).astype(BF16)
    sc = jnp.where(same, jnp.sin(angc), 0.0).astype(BF16)
    hy_tab = _cos_sin(_angles(2 * rep + 1, times, 4 * seq))
    hy_tab_t = jnp.swapaxes(hy_tab, 1, 2)
    return fnet_tab, cc, sc, hy_tab, hy_tab_t


def _position_features(seq):
    pos = jnp.arange(seq, dtype=F32)
    t = pos / seq
    f = jnp.linspace(1e-4, HY_BANDS - 1, HY_BANDS, dtype=F32)
    ang = (2.0 * math.pi * t)[:, None] * f[None, :]
    feats = jnp.concatenate([t[:, None], jnp.cos(ang), -jnp.sin(ang)], axis=-1)
    return jnp.pad(feats, ((0, 0), (0, LANE - HY_EMB)))


def kernel(x, mem, norm_g, w_in, gla_gk_w, gla_gk_b, gla_norm_g, hy_conv_w, hy_ffn_w1, hy_ffn_b1, hy_ffn_w2, hy_ffn_b2, hy_ffn_w3, hy_sin_freq, hy_decay, hy_skip, sc_conv_w, grp_norm_g, w_out, mem_norm_g, w_xq, w_xkv, w_xo, w_gate_up, w_down, final_norm_g):
    bsz, seq, d = x.shape
    depth = norm_g.shape[0]
    m = bsz * seq
    assert d == D_MODEL and seq % (RADIX * GLA_CHUNK) == 0 and mem.shape[1] == N_MEM

    q_w, k_w, v_w, g_w, lr_w, uf_w, uh_w, us_w = jnp.split(
        w_in.astype(BF16), [256, 512, 1024, 1536, 1568, 2080, 3616], axis=-1)
    w_p = jnp.concatenate([uh_w, us_w, q_w, k_w, v_w, g_w, uf_w], axis=-1)
    w_lr = jnp.pad(lr_w, ((0, 0), (0, 0), (0, LANE - 2 * GLA_RANK)))
    wgf = jnp.pad(gla_gk_w[:, 0], ((0, 0), (0, LANE - GLA_RANK), (0, 0))).astype(BF16)
    wgb = jnp.pad(gla_gk_w[:, 1], ((0, 0), (GLA_RANK, LANE - 2 * GLA_RANK), (0, 0))).astype(BF16)
    gkb = gla_gk_b.astype(F32)
    w_out_b, w_xq_b, w_xkv_b, w_xo_b = (t.astype(BF16) for t in (w_out, w_xq, w_xkv, w_xo))
    w_gu_b, w_d_b = w_gate_up.astype(BF16), w_down.astype(BF16)
    fnet_tab, cc, sc, hy_tab, hy_tab_t = _tables(seq)
    qlen = seq // RADIX
    feats = _position_features(seq).reshape(qlen, RADIX, LANE).transpose(1, 0, 2).reshape(seq, LANE)

    w1 = jnp.pad(hy_ffn_w1.astype(F32), ((0, 0), (0, LANE - HY_EMB), (0, 0)))
    fa, fb = _filters(feats, w1, hy_ffn_b1.reshape(depth, 1, HY_FFN).astype(F32), hy_ffn_w2.astype(F32),
                      hy_ffn_b2.reshape(depth, 1, HY_FFN).astype(F32), hy_ffn_w3.astype(F32),
                      hy_sin_freq.reshape(depth, 1, HY_FFN).astype(F32),
                      hy_decay.reshape(depth, 1, 4 * GROUP_W).astype(F32), seq)
    res_major = lambda t: t.reshape(depth, RADIX, qlen, t.shape[-1])
    h_cos = _filter_spectrum(hy_tab, res_major(fa), seq)
    h_sin = _filter_spectrum(hy_tab, res_major(fb), seq)

    mem2 = mem.reshape(bsz * N_MEM, d)
    xf = x.reshape(m, d)
    for l in range(depth):
        p, lr = _norm_matmul(xf, norm_g[l, 0], w_p, l, 1024, 2560, w_side=w_lr)
        y_a = _gla(p, lr, wgf[l], wgb[l], gkb[l, 0:1], gkb[l, 1:2], gla_norm_g[l].reshape(1, GLA_DV).astype(F32),
                   bsz, seq)
        hv, hx1, hx2, uf, y_d = _short(p, hy_conv_w[l].astype(F32), sc_conv_w[l].astype(F32),
                                       grp_norm_g[l, 2:3].astype(F32), bsz, seq)
        y_b = _fnet(uf, fnet_tab, cc, sc, grp_norm_g[l, 0:1].astype(F32), bsz, seq)
        gn_c = grp_norm_g[l, 1:2].astype(F32)
        pcs = _conv_spectrum(hy_tab, hv, h_cos, h_sin, l, 0, bsz, seq)
        z = _inverse(hy_tab_t, pcs, hx1, hv, hy_skip[l, 0:1].astype(F32), gn_c, bsz, seq, final=False)
        pcs = _conv_spectrum(hy_tab, z, h_cos, h_sin, l, 1, bsz, seq)
        y_c = _inverse(hy_tab_t, pcs, hx2, z, hy_skip[l, 1:2].astype(F32), gn_c, bsz, seq, final=True)
        xf = _out_proj((y_a, y_b, y_c, y_d), w_out_b, l, xf)
        q = _norm_matmul(xf, norm_g[l, 1], w_xq_b, l, 1024, D_MODEL)
        kv = _norm_matmul(mem2, mem_norm_g, w_xkv_b, l, 1024, 1024)
        xf = _attn(q, kv, w_xo_b, l, xf, seq)
        xf = _swiglu(xf, norm_g[l, 2], w_gu_b, w_d_b, l, final_norm_g, final=(l == depth - 1))
    return xf.reshape(bsz, seq, d)
```

```python
import functools
import math

import jax
import jax.numpy as jnp
from jax import lax
from jax.experimental import pallas as pl
from jax.experimental.pallas import tpu as pltpu

F32 = jnp.float32
BF16 = jnp.bfloat16
EPS = 1e-6

D_MODEL = 2048
GROUP_W = 512
GLA_HEADS = 4
GLA_DK = 64
GLA_DV = 128
GLA_RANK = 16
GLA_GATE_NORM = 16.0
GLA_CHUNK = 64
FNET_CH = 128
HY_BANDS = 16
HY_EMB = 2 * HY_BANDS + 1
HY_FFN = 64
XA_HEADS = 4
XA_HD = D_MODEL // XA_HEADS
N_MEM = 256
RADIX = 4
LANE = 128
BF16_ROWS = 16
VMEM_LIMIT = 56 * 1024 * 1024

ROWS_WIDE = 1024
ROWS_MID = 512
ROWS_ELEMENTWISE = 256
PROJ_COLS = 2560
KV_COLS = 1024
FF_COLS = 512
GLA_ROWS = 512
DFT_ROWS = 512

GLA_QK = GLA_HEADS * GLA_DK
P_UH, P_US, P_Q, P_K, P_V, P_G, P_UF = 0, 1536, 3072, 3328, 3584, 4096, 4608
P_W = 5120

NT = (((1,), (1,)), ((), ()))
TN = (((0,), (0,)), ((), ()))


def _cp(*sem):
    return pltpu.CompilerParams(dimension_semantics=sem, vmem_limit_bytes=VMEM_LIMIT)


def _resident(shape, index_map):
    return pl.BlockSpec(shape, index_map, pipeline_mode=pl.Buffered(1))


def _rms(x, g):
    ms = jnp.mean(x * x, axis=-1, keepdims=True)
    return x * lax.rsqrt(ms + EPS) * g


def _norm_rows_to(x_ref, g_ref, hn_ref):
    rows = min(256, x_ref.shape[0])
    n = x_ref.shape[0] // rows

    def body(i, c):
        r = pl.ds(pl.multiple_of(i * rows, rows), rows)
        hn_ref[r, :] = _rms(x_ref[r, :], g_ref[...]).astype(hn_ref.dtype)
        return c

    lax.fori_loop(0, n, body, 0)


def _norm_matmul_kernel(x_hbm, g_ref, w_ref, *rest, with_side, row_tiles, col_tiles):
    if with_side:
        w2_ref, o_ref, o2_ref, hn_ref, x_ref, sem = rest
    else:
        o_ref, hn_ref, x_ref, sem = rest
    i, j = pl.program_id(0), pl.program_id(1)
    part = x_ref.shape[1]
    c = i * col_tiles + j

    def x_copy(k):
        rows = pl.ds(pl.multiple_of(k * part, part), part)
        return pltpu.make_async_copy(x_hbm.at[rows, :], x_ref.at[k % 2], sem.at[k % 2])

    @pl.when(c == 0)
    def _():
        x_copy(0).start()

    @pl.when(i < row_tiles)
    def _():
        x_copy(c).wait()

    @pl.when(c + 1 < row_tiles * col_tiles)
    def _():
        x_copy(c + 1).start()

    def norm_part():
        g = g_ref[...]
        for r in range(0, part, BF16_ROWS):
            rows = pl.ds(pl.multiple_of(j * part + r, BF16_ROWS), BF16_ROWS)
            hn_ref[i % 2, rows, :] = _rms(x_ref[c % 2, r:r + BF16_ROWS, :], g).astype(BF16)

    def matmul():
        o_ref[...] = jnp.dot(hn_ref[(i + 1) % 2], w_ref[...], preferred_element_type=F32).astype(o_ref.dtype)

    @pl.when(i == 0)
    def _():
        norm_part()

    @pl.when((i > 0) & (i < row_tiles))
    def _():
        matmul()
        norm_part()

    @pl.when(i == row_tiles)
    def _():
        matmul()

    if with_side:
        @pl.when((i > 0) & (j == 0))
        def _():
            o2_ref[...] = jnp.dot(hn_ref[(i + 1) % 2], w2_ref[...], preferred_element_type=F32)


def _norm_matmul(x, g, w, layer, tm, tn, w_side=None):
    m, k = x.shape
    n = w.shape[2]
    tm, tn = min(tm, m), min(tn, n)
    ni, nj = m // tm, n // tn
    assert (tm // nj) % BF16_ROWS == 0
    prev = lambda i: jnp.maximum(i - 1, 0)
    in_specs = [
        pl.BlockSpec(memory_space=pl.ANY),
        pl.BlockSpec((1, k), lambda i, j: (0, 0)),
        (_resident if tn == n else pl.BlockSpec)((None, k, tn), lambda i, j: (layer, 0, j)),
    ]
    out_shape = [jax.ShapeDtypeStruct((m, n), BF16)]
    out_specs = [pl.BlockSpec((tm, tn), lambda i, j: (prev(i), jnp.where(i == 0, 0, j)))]
    args = [x, g.reshape(1, k), w]
    if w_side is not None:
        n2 = w_side.shape[2]
        in_specs.append(pl.BlockSpec((None, k, n2), lambda i, j: (layer, 0, 0)))
        out_shape.append(jax.ShapeDtypeStruct((m, n2), F32))
        out_specs.append(pl.BlockSpec((tm, n2), lambda i, j: (prev(i), 0)))
        args.append(w_side)
    res = pl.pallas_call(
        functools.partial(_norm_matmul_kernel, with_side=w_side is not None, row_tiles=ni, col_tiles=nj),
        grid=(ni + 1, nj),
        in_specs=in_specs,
        out_specs=out_specs,
        out_shape=out_shape,
        scratch_shapes=[pltpu.VMEM((2, tm, k), BF16), pltpu.VMEM((2, tm // nj, k), F32),
                        pltpu.SemaphoreType.DMA((2,))],
        compiler_params=_cp("arbitrary", "arbitrary"),
        name="norm_matmul",
    )(*args)
    return res if w_side is not None else res[0]


def _gla_pass(fwd, blk, q_ref, k_ref, v_ref, lr_ref, g_ref, wg_ref, bg_ref, gn_ref, y_ref, st_ref, ob_ref):
    ns, ts = q_ref.shape[0], q_ref.shape[1]
    c_sz = GLA_CHUNK
    nc = ts // c_sz
    ii = lax.broadcasted_iota(jnp.int32, (c_sz, c_sz), 0)
    jj = lax.broadcasted_iota(jnp.int32, (c_sz, c_sz), 1)
    incl = (jj <= ii) if fwd else (jj >= ii)
    mask = incl if fwd else (jj > ii)
    tri = jnp.where(incl, 1.0, 0.0).astype(BF16)
    gn = gn_ref[...]
    chunk = lambda c: slice(c * c_sz, (c + 1) * c_sz)
    order = list(range(nc)) if fwd else list(reversed(range(nc)))
    problems = [(s, h) for s in range(ns) for h in range(GLA_HEADS)]
    pair = lambda h: slice((h // 2) * LANE, (h // 2 + 1) * LANE)
    val = lambda h: slice(h * GLA_DV, (h + 1) * GLA_DV)
    first = lax.broadcasted_iota(jnp.int32, (c_sz, LANE), 1) < GLA_DK
    own = lambda h: first if h % 2 == 0 else jnp.logical_not(first)

    qe, ke, qi, ku, dec = {}, {}, {}, {}, {}
    for s in range(ns):
        x = jnp.dot(lr_ref[s].astype(BF16), wg_ref[...], preferred_element_type=F32) + bg_ref[...]
        gk = (jnp.minimum(x, 0.0) - jnp.log1p(jnp.exp(-jnp.abs(x)))) * (1.0 / GLA_GATE_NORM)
        hi = gk.astype(BF16)
        lo = (gk - hi.astype(F32)).astype(BF16)
        for c in range(nc):
            r = chunk(c)
            b = jnp.dot(tri, hi[r], preferred_element_type=F32) + jnp.dot(tri, lo[r], preferred_element_type=F32)
            bref = b[c_sz // 2:c_sz // 2 + 1] if fwd else b[c_sz // 2 - 1:c_sz // 2]
            btot = b[c_sz - 1:c_sz] if fwd else b[0:1]
            qe_f = q_ref[s, r, :].astype(F32) * (GLA_DK ** -0.5) * jnp.exp(b - bref)
            ke_f = k_ref[s, r, :].astype(F32) * jnp.exp(bref - b)
            qe[s, c] = qe_f
            ke[s, c] = ke_f.astype(BF16)
            qi[s, c] = (qe_f * jnp.exp(bref)).astype(BF16)
            ku[s, c] = ke_f * jnp.exp(btot - bref)
            dec[s, c] = jnp.exp(btot)

    kv = {}
    for c in order:
        for s, h in problems:
            kv[s, h, c] = lax.dot_general(v_ref[s, chunk(c), val(h)],
                                          jnp.where(own(h), ku[s, c][:, pair(h)], 0.0).astype(BF16), TN,
                                          preferred_element_type=F32)
    st = {(s, h): st_ref[s, h] for s, h in problems}
    st_in = {}
    for c in order:
        for s, h in problems:
            st_in[s, h, c] = st[s, h].astype(BF16)
            st[s, h] = st[s, h] * dec[s, c][:, pair(h)] + kv[s, h, c]
    for s, h in problems:
        st_ref[s, h] = st[s, h]
    for c in order:
        r = chunk(c)
        rows = pl.ds(pl.multiple_of(blk * ts + c * c_sz, c_sz), c_sz)
        scores = {}
        for s, h in problems:
            sc = lax.dot_general(jnp.where(own(h), qe[s, c][:, pair(h)], 0.0).astype(BF16), ke[s, c][:, pair(h)], NT,
                                 preferred_element_type=F32)
            scores[s, h] = jnp.where(mask, sc, 0.0).astype(BF16)
        for s, h in problems:
            o = jnp.dot(scores[s, h], v_ref[s, r, val(h)], preferred_element_type=F32)
            o = o + lax.dot_general(qi[s, c][:, pair(h)], st_in[s, h, c], NT, preferred_element_type=F32)
            if fwd:
                tot = o + ob_ref[s, rows, val(h)]
                gate = g_ref[s, r, val(h)].astype(F32)
                y_ref[s, r, val(h)] = (_rms(tot, gn) * (gate * jax.nn.sigmoid(gate))).astype(y_ref.dtype)
            else:
                ob_ref[s, rows, val(h)] = o


def _gla_kernel(q_ref, k_ref, v_ref, lr_ref, g_ref, wgf_ref, wgb_ref, bgf_ref, bgb_ref, gn_ref,
                y_ref, st_ref, ob_ref, *, nb):
    n = pl.program_id(1)

    @pl.when((n == 0) | (n == nb))
    def _():
        st_ref[...] = jnp.zeros_like(st_ref)

    @pl.when(n < nb)
    def _():
        _gla_pass(False, nb - 1 - n, q_ref, k_ref, v_ref, lr_ref, g_ref, wgb_ref, bgb_ref, gn_ref, y_ref, st_ref, ob_ref)

    @pl.when(n >= nb)
    def _():
        _gla_pass(True, n - nb, q_ref, k_ref, v_ref, lr_ref, g_ref, wgf_ref, bgf_ref, gn_ref, y_ref, st_ref, ob_ref)


def _gla(p, lr, wgf, wgb, bgf, bgb, gn, bsz, seq):
    ts = min(GLA_ROWS, seq)
    nb = seq // ts
    ns = 2 if bsz % 2 == 0 else 1
    blk = lambda n: jnp.where(n < nb, nb - 1 - n, n - nb)
    oblk = lambda n: jnp.where(n < nb, 0, n - nb)
    in_specs = [
        pl.BlockSpec((ns, ts, GLA_QK), lambda b, n: (b, blk(n), P_Q // GLA_QK)),
        pl.BlockSpec((ns, ts, GLA_QK), lambda b, n: (b, blk(n), P_K // GLA_QK)),
        pl.BlockSpec((ns, ts, GROUP_W), lambda b, n: (b, blk(n), P_V // GROUP_W)),
        pl.BlockSpec((ns, ts, LANE), lambda b, n: (b, blk(n), 0)),
        pl.BlockSpec((ns, ts, GROUP_W), lambda b, n: (b, oblk(n), P_G // GROUP_W)),
        pl.BlockSpec((LANE, GLA_QK), lambda b, n: (0, 0)),
        pl.BlockSpec((LANE, GLA_QK), lambda b, n: (0, 0)),
        pl.BlockSpec((1, GLA_QK), lambda b, n: (0, 0)),
        pl.BlockSpec((1, GLA_QK), lambda b, n: (0, 0)),
        pl.BlockSpec((1, GLA_DV), lambda b, n: (0, 0)),
    ]
    p3 = p.reshape(bsz, seq, P_W)
    y = pl.pallas_call(
        functools.partial(_gla_kernel, nb=nb),
        grid=(bsz // ns, 2 * nb),
        in_specs=in_specs,
        out_specs=pl.BlockSpec((ns, ts, GROUP_W), lambda b, n: (b, oblk(n), 0)),
        out_shape=jax.ShapeDtypeStruct((bsz, seq, GROUP_W), BF16),
        scratch_shapes=[pltpu.VMEM((ns, GLA_HEADS, GLA_DV, LANE), F32), pltpu.VMEM((ns, seq, GROUP_W), F32)],
        compiler_params=_cp("parallel", "arbitrary"),
        name="gla",
    )(p3, p3, p3, lr.reshape(bsz, seq, LANE), p3, wgf, wgb, bgf, bgb, gn)
    return y.reshape(bsz * seq, GROUP_W)


def _table_rows(tab_ref, tile, rows):
    window = pl.ds(pl.multiple_of(tile * rows, rows), rows)
    return lambda k: tab_ref[k, window, :]


def _butterflies(a, b):
    a02p, a02m, a13p, a13m = a[0] + a[2], a[0] - a[2], a[1] + a[3], a[1] - a[3]
    b02p, b02m, b13p, b13m = b[0] + b[2], b[0] - b[2], b[1] + b[3], b[1] - b[3]
    re = [a02p + a13p, a02m - b13m, a02p - a13p, a02m + b13m]
    nim = [b02p + b13p, b02m + a13m, b02p - b13p, b02m - a13m]
    return re, nim


def _stage_products(tab, z_refs, ab_ref):
    for k in range(2 * RADIX):
        ab_ref[k] = jnp.dot(tab(k), z_refs[k % RADIX][...], preferred_element_type=F32)


def _load_products(ab_ref, rows, cols):
    return ([ab_ref[r, rows, cols] for r in range(RADIX)], [ab_ref[RADIX + r, rows, cols] for r in range(RADIX)])


def _groups(n_rows, n_cols, cols_per_group=2 * LANE):
    return [(slice(r, r + BF16_ROWS), slice(c, c + cols_per_group))
            for r in range(0, n_rows, BF16_ROWS) for c in range(0, n_cols, cols_per_group)]


def _fnet_kernel(tab_ref, x0, x1, x2, x3, cc_ref, sc_ref, gn_ref, y_ref, ab_ref, bf_ref, *, scale):
    tq = y_ref.shape[1]
    _stage_products(_table_rows(tab_ref, pl.program_id(1), tq), (x0, x1, x2, x3), ab_ref)
    for rows, cols in _groups(tq, GROUP_W):
        re, nim = _butterflies(*_load_products(ab_ref, rows, cols))
        for m in range(RADIX):
            bf_ref[m, rows, cols] = re[m].astype(BF16)
            bf_ref[RADIX + m, rows, cols] = nim[m].astype(BF16)
    for m in range(RADIX):
        y = jnp.dot(bf_ref[m], cc_ref[...], preferred_element_type=F32)
        y = y - jnp.dot(bf_ref[RADIX + m], sc_ref[...], preferred_element_type=F32)
        y_ref[m] = _rms(y * scale, gn_ref[...]).astype(y_ref.dtype)


def _fnet(uf, tab, cc, sc, gn, bsz, seq):
    q = seq // RADIX
    tq = min(DFT_ROWS, q)
    nq = q // tq
    res = lambda r: pl.BlockSpec((None, None, q, GROUP_W), lambda b, i: (b, r, 0, 0))
    sq = pl.BlockSpec((GROUP_W, GROUP_W), lambda b, i: (0, 0))
    out = pl.pallas_call(
        functools.partial(_fnet_kernel, scale=1.0 / math.sqrt(seq * FNET_CH)),
        grid=(bsz, nq),
        in_specs=[_resident((2 * RADIX, q, q), lambda b, i: (0, 0, 0)),
                  res(0), res(1), res(2), res(3), sq, sq,
                  pl.BlockSpec((1, GROUP_W), lambda b, i: (0, 0))],
        out_specs=pl.BlockSpec((None, RADIX, tq, GROUP_W), lambda b, i: (b, 0, i, 0)),
        out_shape=jax.ShapeDtypeStruct((bsz, RADIX, q, GROUP_W), BF16),
        scratch_shapes=[pltpu.VMEM((2 * RADIX, tq, GROUP_W), F32), pltpu.VMEM((2 * RADIX, tq, GROUP_W), BF16)],
        compiler_params=_cp("parallel", "arbitrary"),
        name="fnet",
    )(tab, uf, uf, uf, uf, cc, sc, gn)
    return out.reshape(bsz * seq, GROUP_W)


def _conv3(main, prev_row, next_row, w):
    ts = main.shape[0]
    rid = lax.broadcasted_iota(jnp.int32, main.shape, 0)
    up = jnp.where(rid == 0, prev_row, pltpu.roll(main, 1, 0))
    dn = jnp.where(rid == ts - 1, next_row, pltpu.roll(main, ts - 1, 0))
    return up * w[0:1] + main * w[1:2] + dn * w[2:3]


def _short_kernel(hm_ref, hp_ref, hn_ref, sm_ref, sp_ref, sn_ref, uf_ref, hw_ref, sw_ref, gn_ref,
                  v_ref, x1_ref, x2_ref, ufr_ref, yd_ref, nat_ref, *, nblk):
    i = pl.program_id(0)
    keep_prev = jnp.where(i % nblk == 0, 0.0, 1.0)
    keep_next = jnp.where(i % nblk == nblk - 1, 0.0, 1.0)
    last = BF16_ROWS - 1
    w = GROUP_W
    ts = hm_ref.shape[0]
    per = w // LANE

    def neighbours(ref, halo_prev, halo_next, r0, rows, cols):
        above = (ref[r0 - 1:r0, cols].astype(F32) if r0 > 0
                 else halo_prev[last:last + 1, cols].astype(F32) * keep_prev)
        below = (ref[r0 + rows:r0 + rows + 1, cols].astype(F32) if r0 + rows < ts
                 else halo_next[0:1, cols].astype(F32) * keep_next)
        return above, below

    rows_h, lanes_h = 2 * BF16_ROWS, 2 * LANE
    for r0 in range(0, ts, rows_h):
        for c0 in range(0, 3 * w, lanes_h):
            cols = slice(c0, c0 + lanes_h)
            above, below = neighbours(hm_ref, hp_ref, hn_ref, r0, rows_h, cols)
            conv = _conv3(hm_ref[r0:r0 + rows_h, cols].astype(F32), above, below, hw_ref[:, cols])
            for k in range(lanes_h // LANE):
                nat_ref[c0 // LANE + k, r0:r0 + rows_h, :] = conv[:, k * LANE:(k + 1) * LANE]
    for c in range(per):
        nat_ref[3 * per + c] = uf_ref[:, c * LANE:(c + 1) * LANE].astype(F32)
    for g, dst in enumerate((v_ref, x1_ref, x2_ref, ufr_ref)):
        for r in range(RADIX):
            for c in range(per):
                piece = nat_ref[g * per + c, pl.ds(r, ts // RADIX, stride=RADIX), :]
                dst[r, :, c * LANE:(c + 1) * LANE] = piece.astype(dst.dtype)

    bcol, ccol, hcol = slice(0, w), slice(w, 2 * w), slice(2 * w, 3 * w)
    rows_s = BF16_ROWS
    for r0 in range(0, ts, rows_s):
        rs = slice(r0, r0 + rows_s)
        ca, cb = neighbours(sm_ref, sp_ref, sn_ref, r0, rows_s, ccol)
        ha, hb = neighbours(sm_ref, sp_ref, sn_ref, r0, rows_s, hcol)
        prod = sm_ref[rs, ccol].astype(F32) * sm_ref[rs, hcol].astype(F32)
        conv = _conv3(prod, ca * ha, cb * hb, sw_ref[...])
        yd_ref[rs, :] = _rms(sm_ref[rs, bcol].astype(F32) * conv, gn_ref[...]).astype(yd_ref.dtype)


def _short(p, hy_w, sc_w, gn, bsz, seq):
    m = bsz * seq
    ts = min(ROWS_ELEMENTWISE, seq)
    nblk = seq // ts
    hb = ts // BF16_ROWS
    nh = m // BF16_ROWS
    wide = 3 * GROUP_W
    main = lambda cb: pl.BlockSpec((ts, wide), lambda i: (i, cb))
    prev = lambda cb: pl.BlockSpec((BF16_ROWS, wide), lambda i: (jnp.maximum(i * hb - 1, 0), cb))
    nxt = lambda cb: pl.BlockSpec((BF16_ROWS, wide), lambda i: (jnp.minimum((i + 1) * hb, nh - 1), cb))
    nat = pl.BlockSpec((ts, GROUP_W), lambda i: (i, 0))
    res = pl.BlockSpec((None, RADIX, ts // RADIX, GROUP_W), lambda i: (i // nblk, 0, i % nblk, 0))
    res_shape = jax.ShapeDtypeStruct((bsz, RADIX, seq // RADIX, GROUP_W), BF16)
    return pl.pallas_call(
        functools.partial(_short_kernel, nblk=nblk),
        grid=(m // ts,),
        in_specs=[main(P_UH // wide), prev(P_UH // wide), nxt(P_UH // wide),
                  main(P_US // wide), prev(P_US // wide), nxt(P_US // wide),
                  pl.BlockSpec((ts, GROUP_W), lambda i: (i, P_UF // GROUP_W)),
                  pl.BlockSpec((3, wide), lambda i: (0, 0)),
                  pl.BlockSpec((3, GROUP_W), lambda i: (0, 0)),
                  pl.BlockSpec((1, GROUP_W), lambda i: (0, 0))],
        out_specs=[res, res, res, res, nat],
        out_shape=[res_shape] * 4 + [jax.ShapeDtypeStruct((m, GROUP_W), BF16)],
        scratch_shapes=[pltpu.VMEM((4 * GROUP_W // LANE, ts, LANE), F32)],
        compiler_params=_cp("parallel"),
        name="short_conv",
    )(p, p, p, p, p, p, p, hy_w, sc_w, gn)


def _filter_kernel(feat_ref, w1_ref, b1_ref, w2_ref, b2_ref, w3_ref, fr_ref, dc_ref, a_ref, b_ref):
    hp = lax.Precision.HIGHEST
    feats = feat_ref[...]
    fr = fr_ref[...]
    h = jnp.sin(fr * (jnp.dot(feats, w1_ref[...], precision=hp, preferred_element_type=F32) + b1_ref[...]))
    h = jnp.sin(fr * (jnp.dot(h, w2_ref[...], precision=hp, preferred_element_type=F32) + b2_ref[...]))
    split = lambda t: (t.astype(BF16), (t - t.astype(BF16).astype(F32)).astype(BF16))
    (h_hi, h_lo), (w_hi, w_lo) = split(h), split(w3_ref[...])
    h = (jnp.dot(h_hi, w_hi, preferred_element_type=F32) + jnp.dot(h_hi, w_lo, preferred_element_type=F32)
         + jnp.dot(h_lo, w_hi, preferred_element_type=F32))
    h = h * jnp.exp(-feats[:, 0:1] * jnp.abs(dc_ref[...]))
    tl = h.shape[0]
    pos = lax.broadcasted_iota(jnp.int32, (tl, GROUP_W), 0) + pl.program_id(1) * tl
    for o in range(2):
        hf = h[:, (2 * o) * GROUP_W:(2 * o + 1) * GROUP_W]
        hb = jnp.where(pos == 0, 0.0, h[:, (2 * o + 1) * GROUP_W:(2 * o + 2) * GROUP_W])
        a_ref[:, o * GROUP_W:(o + 1) * GROUP_W] = (hf + hb).astype(a_ref.dtype)
        b_ref[:, o * GROUP_W:(o + 1) * GROUP_W] = (hb - hf).astype(b_ref.dtype)


def _filters(feats, w1, b1, w2, b2, w3, fr, dc, seq):
    depth = w1.shape[0]
    tl = min(ROWS_MID, seq)
    lay = lambda r, c: pl.BlockSpec((None, r, c), lambda l, i: (l, 0, 0))
    out = pl.BlockSpec((None, tl, 2 * GROUP_W), lambda l, i: (l, i, 0))
    return pl.pallas_call(
        _filter_kernel,
        grid=(depth, seq // tl),
        in_specs=[pl.BlockSpec((tl, LANE), lambda l, i: (i, 0)),
                  lay(LANE, HY_FFN), lay(1, HY_FFN), lay(HY_FFN, HY_FFN), lay(1, HY_FFN),
                  lay(HY_FFN, 4 * GROUP_W), lay(1, HY_FFN), lay(1, 4 * GROUP_W)],
        out_specs=[out, out],
        out_shape=[jax.ShapeDtypeStruct((depth, seq, 2 * GROUP_W), BF16)] * 2,
        compiler_params=_cp("parallel", "parallel"),
        name="hyena_filter",
    )(feats, w1, b1, w2, b2, w3, fr, dc)


def _hyena_classes(a, b):
    re, nim = _butterflies(a, b)
    return re, [nim[0], nim[1], -nim[2], -nim[3]]


def _spectrum_kernel(tab_ref, z0, z1, z2, z3, o_ref, ab_ref):
    tq = o_ref.shape[1]
    _stage_products(_table_rows(tab_ref, pl.program_id(1), tq), (z0, z1, z2, z3), ab_ref)
    for rows, cols in _groups(tq, GROUP_W):
        zr, zs = _hyena_classes(*_load_products(ab_ref, rows, cols))
        for m in range(RADIX):
            o_ref[m, rows, cols] = zr[m].astype(o_ref.dtype)
            o_ref[RADIX + m, rows, cols] = zs[m].astype(o_ref.dtype)


def _filter_spectrum(tab, filt, seq):
    depth, _, _, cols = filt.shape
    q = seq // RADIX
    tq = min(DFT_ROWS, q)
    ncb = cols // GROUP_W
    res = lambda r: pl.BlockSpec((None, None, q, GROUP_W), lambda n, i: (n // ncb, r, 0, n % ncb))
    return pl.pallas_call(
        _spectrum_kernel,
        grid=(depth * ncb, q // tq),
        in_specs=[_resident((2 * RADIX, q, q), lambda n, i: (0, 0, 0)), res(0), res(1), res(2), res(3)],
        out_specs=pl.BlockSpec((None, 2 * RADIX, tq, GROUP_W), lambda n, i: (n // ncb, 0, i, n % ncb)),
        out_shape=jax.ShapeDtypeStruct((depth, 2 * RADIX, q, cols), BF16),
        scratch_shapes=[pltpu.VMEM((2 * RADIX, tq, GROUP_W), F32)],
        compiler_params=_cp("parallel", "arbitrary"),
        name="filter_spectrum",
    )(tab, filt, filt, filt, filt)


def _conv_spectrum_kernel(tab_ref, z0, z1, z2, z3, hr_ref, hi_ref, p_ref, ab_ref):
    tq = p_ref.shape[1]
    _stage_products(_table_rows(tab_ref, pl.program_id(1), tq), (z0, z1, z2, z3), ab_ref)
    for rows, cols in _groups(tq, GROUP_W):
        zr, zs = _hyena_classes(*_load_products(ab_ref, rows, cols))
        gr, gi = [], []
        for m in range(RADIX):
            hr, hi = hr_ref[m, rows, cols].astype(F32), hi_ref[m, rows, cols].astype(F32)
            gr.append(zr[m] * hr + zs[m] * hi)
            gi.append(zr[m] * hi - zs[m] * hr)
        s02, d02, s13, d13 = gr[0] + gr[2], gr[0] - gr[2], gr[1] + gr[3], gr[1] - gr[3]
        t02, e02, t13, e13 = gi[0] + gi[2], gi[0] - gi[2], gi[1] + gi[3], gi[1] - gi[3]
        pc = [s02 + s13, d02 - t13, s02 - s13, d02 + t13]
        ps = [-e02 - e13, -t02 - d13, e13 - e02, d13 - t02]
        for r in range(RADIX):
            p_ref[r, rows, cols] = pc[r].astype(p_ref.dtype)
            p_ref[RADIX + r, rows, cols] = ps[r].astype(p_ref.dtype)


def _conv_spectrum(tab, z, h_cos, h_sin, layer, order, bsz, seq):
    q = seq // RADIX
    tq = min(DFT_ROWS, q)
    res = lambda r: pl.BlockSpec((None, None, q, GROUP_W), lambda b, i: (b, r, 0, 0))
    filt = lambda half: pl.BlockSpec((None, RADIX, tq, GROUP_W), lambda b, i: (layer, half, i, order))
    return pl.pallas_call(
        _conv_spectrum_kernel,
        grid=(bsz, q // tq),
        in_specs=[_resident((2 * RADIX, q, q), lambda b, i: (0, 0, 0)), res(0), res(1), res(2), res(3),
                  filt(0), filt(1)],
        out_specs=pl.BlockSpec((None, 2 * RADIX, tq, GROUP_W), lambda b, i: (b, 0, i, 0)),
        out_shape=jax.ShapeDtypeStruct((bsz, 2 * RADIX, q, GROUP_W), BF16),
        scratch_shapes=[pltpu.VMEM((2 * RADIX, tq, GROUP_W), F32)],
        compiler_params=_cp("parallel", "arbitrary"),
        name="hyena_spectrum",
    )(tab, z, z, z, z, h_cos, h_sin)


def _inverse_kernel(tab_ref, p_ref, xm_ref, w_ref, skip_ref, gn_ref, o_ref, *scratch, inv_len, final):
    tj = xm_ref.shape[1]
    tab = _table_rows(tab_ref, pl.program_id(1), tj)
    for r in range(RADIX):
        y = jnp.dot(tab(r), p_ref[r], preferred_element_type=F32)
        y = y + jnp.dot(tab(RADIX + r), p_ref[RADIX + r], preferred_element_type=F32)
        out = xm_ref[r].astype(F32) * (y * inv_len + w_ref[r].astype(F32) * skip_ref[...])
        if final:
            out = _rms(out, gn_ref[...])
            for c in range(GROUP_W // LANE):
                scratch[0][c, pl.ds(r, tj, stride=RADIX), :] = out[:, c * LANE:(c + 1) * LANE]
        else:
            o_ref[r] = out.astype(o_ref.dtype)
    if final:
        for c in range(GROUP_W // LANE):
            o_ref[:, c * LANE:(c + 1) * LANE] = scratch[0][c].astype(o_ref.dtype)


def _inverse(tab_t, pcs, xm, w, skip, gn, bsz, seq, final):
    q = seq // RADIX
    tj = min(DFT_ROWS, q)
    nj = q // tj
    tile = pl.BlockSpec((None, RADIX, tj, GROUP_W), lambda b, i: (b, 0, i, 0))
    vec = pl.BlockSpec((1, GROUP_W), lambda b, i: (0, 0))
    if final:
        out_spec = pl.BlockSpec((RADIX * tj, GROUP_W), lambda b, i: (b * nj + i, 0))
        out_shape = jax.ShapeDtypeStruct((bsz * seq, GROUP_W), BF16)
        scratch = [pltpu.VMEM((GROUP_W // LANE, RADIX * tj, LANE), F32)]
    else:
        out_spec, out_shape, scratch = tile, jax.ShapeDtypeStruct((bsz, RADIX, q, GROUP_W), BF16), []
    return pl.pallas_call(
        functools.partial(_inverse_kernel, inv_len=1.0 / seq, final=final),
        grid=(bsz, nj),
        in_specs=[_resident((2 * RADIX, q, q), lambda b, i: (0, 0, 0)),
                  pl.BlockSpec((None, 2 * RADIX, q, GROUP_W), lambda b, i: (b, 0, 0, 0)),
                  tile, tile, vec, vec],
        out_specs=out_spec,
        out_shape=out_shape,
        scratch_shapes=scratch,
        compiler_params=_cp("parallel", "arbitrary"),
        name="hyena_inverse",
    )(tab_t, pcs, xm, w, skip, gn)


def _out_kernel(ya_ref, yb_ref, yc_ref, yd_ref, w_ref, x_ref, o_ref):
    acc = x_ref[...]
    for g, y_ref in enumerate((ya_ref, yb_ref, yc_ref, yd_ref)):
        acc = acc + jnp.dot(y_ref[...], w_ref[g * GROUP_W:(g + 1) * GROUP_W, :], preferred_element_type=F32)
    o_ref[...] = acc


def _out_proj(ys, w, layer, x):
    m, n = x.shape
    tm = min(ROWS_MID, m)
    piece =pl.BlockSpec((tm, GROUP_W), lambda i: (i, 0))
    xs = pl.BlockSpec((tm, n), lambda i: (i, 0))
    return pl.pallas_call(
        _out_kernel,
        grid=(m // tm,),
        in_specs=[piece, piece, piece, piece, _resident((None,) + w.shape[1:], lambda i: (layer, 0, 0)), xs],
        out_specs=xs,
        out_shape=jax.ShapeDtypeStruct((m, n), F32),
        compiler_params=_cp("parallel"),
        name="out_proj",
    )(*ys, w, x)


def _attn_kernel(q_ref, k_ref, v_ref, wo_ref, x_ref, o_ref, att_ref):
    heads = [slice(h * XA_HD, (h + 1) * XA_HD) for h in range(XA_HEADS)]
    scores = [lax.dot_general(q_ref[:, hs], k_ref[:, hs], NT, preferred_element_type=F32) for hs in heads]
    for hs, s in zip(heads, scores):
        s = s * (XA_HD ** -0.5)
        e = jnp.exp(s - jnp.max(s, axis=-1, keepdims=True))
        p = e / jnp.sum(e, axis=-1, keepdims=True)
        att_ref[:, hs] = jnp.dot(p.astype(BF16), v_ref[:, hs], preferred_element_type=F32).astype(BF16)
    o_ref[...] = x_ref[...] + jnp.dot(att_ref[...], wo_ref[...], preferred_element_type=F32)


def _attn(q, kv, wo, layer, x, seq):
    m, d = x.shape
    tm = min(ROWS_MID, seq)
    per = seq // tm
    xs = pl.BlockSpec((tm, d), lambda i: (i, 0))
    return pl.pallas_call(
        _attn_kernel,
        grid=(m // tm,),
        in_specs=[xs,
                  pl.BlockSpec((N_MEM, d), lambda i: (i // per, 0)),
                  pl.BlockSpec((N_MEM, d), lambda i: (i // per, 1)),
                  _resident((None, d, d), lambda i: (layer, 0, 0)),
                  xs],
        out_specs=xs,
        out_shape=jax.ShapeDtypeStruct((m, d), F32),
        scratch_shapes=[pltpu.VMEM((tm, d), BF16)],
        compiler_params=_cp("parallel"),
        name="cross_attn",
    )(q, kv, kv, wo, x)


def _swiglu_kernel(x_hbm, g_ref, wg_ref, wu_ref, wd_ref, fg_ref, o_ref, x_ref, hn_ref, sem, *, final, row_tiles):
    i, f = pl.program_id(0), pl.program_id(1)
    tm = x_ref.shape[0]

    def x_copy(tile):
        return pltpu.make_async_copy(x_hbm.at[pl.ds(pl.multiple_of(tile * tm, tm), tm), :], x_ref, sem)

    @pl.when((i == 0) & (f == 0))
    def _():
        x_copy(0).start()

    @pl.when(f == 0)
    def _():
        x_copy(i).wait()
        _norm_rows_to(x_ref, g_ref, hn_ref)
        o_ref[...] = x_ref[...]

    @pl.when((f == 1) & (i + 1 < row_tiles))
    def _():
        x_copy(i + 1).start()

    hn = hn_ref[...]
    gate = jnp.dot(hn, wg_ref[...], preferred_element_type=F32)
    up = jnp.dot(hn, wu_ref[...], preferred_element_type=F32)
    act = (gate * jax.nn.sigmoid(gate) * up).astype(BF16)
    o_ref[...] += jnp.dot(act, wd_ref[...], preferred_element_type=F32)

    if final:
        @pl.when(f == pl.num_programs(1) - 1)
        def _():
            _norm_rows_to(o_ref, fg_ref, o_ref)


def _swiglu(x, g, w_gu, w_d, layer, final_g, final):
    m, d = x.shape
    dff = w_d.shape[1]
    tm, tf = min(ROWS_WIDE, m), FF_COLS
    nf = dff // tf
    assert nf >= 2
    vec = pl.BlockSpec((1, d), lambda i, f: (0, 0))
    return pl.pallas_call(
        functools.partial(_swiglu_kernel, final=final, row_tiles=m // tm),
        grid=(m // tm, nf),
        in_specs=[pl.BlockSpec(memory_space=pl.ANY), vec,
                  pl.BlockSpec((None, d, tf), lambda i, f: (layer, 0, f)),
                  pl.BlockSpec((None, d, tf), lambda i, f: (layer, 0, nf + f)),
                  pl.BlockSpec((None, tf, d), lambda i, f: (layer, f, 0)),
                  vec],
        out_specs=pl.BlockSpec((tm, d), lambda i, f: (i, 0)),
        out_shape=jax.ShapeDtypeStruct((m, d), F32),
        scratch_shapes=[pltpu.VMEM((tm, d), F32), pltpu.VMEM((tm, d), BF16), pltpu.SemaphoreType.DMA(())],
        compiler_params=_cp("arbitrary", "arbitrary"),
        name="swiglu",
    )(x, g.reshape(1, d), w_gu, w_gu, w_d, final_g.reshape(1, d))


def _angles(rows, cols, period):
    m = (rows[..., :, None] * cols[..., None, :]) % period
    return m.astype(F32) * (2.0 * math.pi / period)


def _cos_sin(ang, sin_sign=1.0):
    return jnp.concatenate([jnp.cos(ang), sin_sign * jnp.sin(ang)], axis=0).astype(BF16)


def _tables(seq):
    q = seq // RADIX
    base = jnp.arange(q, dtype=jnp.int32)
    times = RADIX * base[None, :] + jnp.arange(RADIX, dtype=jnp.int32)[:, None]
    rep = jnp.broadcast_to(base, (RADIX, q))
    fnet_tab = _cos_sin(_angles(rep, times, seq))
    ch = jnp.arange(GROUP_W, dtype=jnp.int32)
    same = (ch[:, None] // FNET_CH) == (ch[None, :] // FNET_CH)
    angc = _angles(ch % FNET_CH, ch % FNET_CH, FNET_CH)
    cc = jnp.where(same, jnp.cos(angc), 0.0).astype(BF16)
    sc = jnp.where(same, jnp.sin(angc), 0.0).astype(BF16)
    hy_tab = _cos_sin(_angles(2 * rep + 1, times, 4 * seq))
    hy_tab_t = jnp.swapaxes(hy_tab, 1, 2)
    return fnet_tab, cc, sc, hy_tab, hy_tab_t


def _position_features(seq):
    pos = jnp.arange(seq, dtype=F32)
    t = pos / seq
    f = jnp.linspace(1e-4, HY_BANDS - 1, HY_BANDS, dtype=F32)
    ang = (2.0 * math.pi * t)[:, None] * f[None, :]
    feats = jnp.concatenate([t[:, None], jnp.cos(ang), -jnp.sin(ang)], axis=-1)
    return jnp.pad(feats, ((0, 0), (0, LANE - HY_EMB)))


def kernel(x, mem, norm_g, w_in, gla_gk_w, gla_gk_b, gla_norm_g, hy_conv_w, hy_ffn_w1, hy_ffn_b1, hy_ffn_w2, hy_ffn_b2, hy_ffn_w3, hy_sin_freq, hy_decay, hy_skip, sc_conv_w, grp_norm_g, w_out, mem_norm_g, w_xq, w_xkv, w_xo, w_gate_up, w_down, final_norm_g):
    bsz, seq, d = x.shape
    depth = norm_g.shape[0]
    m = bsz * seq
    assert d == D_MODEL and seq % (RADIX * GLA_CHUNK) == 0 and mem.shape[1] == N_MEM

    q_w, k_w, v_w, g_w, lr_w, uf_w, uh_w, us_w = jnp.split(
        w_in.astype(BF16), [256, 512, 1024, 1536, 1568, 2080, 3616], axis=-1)
    w_p = jnp.concatenate([uh_w, us_w, q_w, k_w, v_w, g_w, uf_w], axis=-1)
    w_lr = jnp.pad(lr_w, ((0, 0), (0, 0), (0, LANE - 2 * GLA_RANK)))
    wgf = jnp.pad(gla_gk_w[:, 0], ((0, 0), (0, LANE - GLA_RANK), (0, 0))).astype(BF16)
    wgb = jnp.pad(gla_gk_w[:, 1], ((0, 0), (GLA_RANK, LANE - 2 * GLA_RANK), (0, 0))).astype(BF16)
    gkb = gla_gk_b.astype(F32)
    w_out_b, w_xq_b, w_xkv_b, w_xo_b = (t.astype(BF16) for t in (w_out, w_xq, w_xkv, w_xo))
    w_gu_b, w_d_b = w_gate_up.astype(BF16), w_down.astype(BF16)
    fnet_tab, cc, sc, hy_tab, hy_tab_t = _tables(seq)
    qlen = seq // RADIX
    feats = _position_features(seq).reshape(qlen, RADIX, LANE).transpose(1, 0, 2).reshape(seq, LANE)

    w1 = jnp.pad(hy_ffn_w1.astype(F32), ((0, 0), (0, LANE - HY_EMB), (0, 0)))
    fa, fb = _filters(feats, w1, hy_ffn_b1.reshape(depth, 1, HY_FFN).astype(F32), hy_ffn_w2.astype(F32),
                      hy_ffn_b2.reshape(depth, 1, HY_FFN).astype(F32), hy_ffn_w3.astype(F32),
                      hy_sin_freq.reshape(depth, 1, HY_FFN).astype(F32),
                      hy_decay.reshape(depth, 1, 4 * GROUP_W).astype(F32), seq)
    res_major = lambda t: t.reshape(depth, RADIX, qlen, t.shape[-1])
    h_cos = _filter_spectrum(hy_tab, res_major(fa), seq)
    h_sin = _filter_spectrum(hy_tab, res_major(fb), seq)

    mem2 = mem.reshape(bsz * N_MEM, d)
    xf = x.reshape(m, d)
    for l in range(depth):
        p, lr = _norm_matmul(xf, norm_g[l, 0], w_p, l, ROWS_WIDE, PROJ_COLS, w_side=w_lr)
        y_a = _gla(p, lr, wgf[l], wgb[l], gkb[l, 0:1], gkb[l, 1:2], gla_norm_g[l].reshape(1, GLA_DV).astype(F32),
                   bsz, seq)
        hv, hx1, hx2, uf, y_d = _short(p, hy_conv_w[l].astype(F32), sc_conv_w[l].astype(F32),
                                       grp_norm_g[l, 2:3].astype(F32), bsz, seq)
        y_b = _fnet(uf, fnet_tab, cc, sc, grp_norm_g[l, 0:1].astype(F32), bsz, seq)
        gn_c = grp_norm_g[l, 1:2].astype(F32)
        pcs = _conv_spectrum(hy_tab, hv, h_cos, h_sin, l, 0, bsz, seq)
        z = _inverse(hy_tab_t, pcs, hx1, hv, hy_skip[l, 0:1].astype(F32), gn_c, bsz, seq, final=False)
        pcs = _conv_spectrum(hy_tab, z, h_cos, h_sin, l, 1, bsz, seq)
        y_c = _inverse(hy_tab_t, pcs, hx2, z, hy_skip[l, 1:2].astype(F32), gn_c, bsz, seq, final=True)
        xf = _out_proj((y_a, y_b, y_c, y_d), w_out_b, l, xf)
        q = _norm_matmul(xf, norm_g[l, 1], w_xq_b, l, ROWS_WIDE, D_MODEL)
        kv = _norm_matmul(mem2, mem_norm_g, w_xkv_b, l, ROWS_WIDE, KV_COLS)
        xf = _attn(q, kv, w_xo_b, l, xf, seq)
        xf = _swiglu(xf, norm_g[l, 2], w_gu_b, w_d_b, l, final_norm_g, final=(l == depth - 1))
    return xf.reshape(bsz, seq, d)
```
